```python
import jax, jax.numpy as jnp
from jax import lax
import numpy as np

D_MODEL = 2048
BATCH = 16
SEQ = 2048
DEPTH = 2

GRID_W = 64
CTX_LEN = 256
EPS = 1e-6
ROPE_THETA = 10000.0
BLOCK = 128
NEG_INF = -1e30
MLA_HEADS = 8
MLA_Q_RANK = 512
MLA_KV_RANK = 256
MLA_NOPE = 128
MLA_ROPE = 64
MLA_V = 128
MLA_IN = MLA_Q_RANK + MLA_KV_RANK + MLA_ROPE
CONV_CH = 1024
CONV_WIDTH = 31
EVEN_IN = MLA_IN + 2 * CONV_CH
EVEN_MIX = MLA_HEADS * MLA_V + CONV_CH
HEAD_DIM = 128
WIN_Q_HEADS = 12
WIN_KV_HEADS = 4
WINDOW = 128
FNET_GROUPS = 4
FNET_CH = 128
ODD_Q = WIN_Q_HEADS * HEAD_DIM
ODD_KV = WIN_KV_HEADS * HEAD_DIM
ODD_IN = ODD_Q + 2 * ODD_KV + FNET_GROUPS * FNET_CH
ODD_MIX = ODD_Q + FNET_GROUPS * FNET_CH
DENSE_FF = 5632
N_EXPERTS = 8
TOP_K = 2
EXPERT_FF = 7168

kernel_name = "hybrid_mla_conv_swa_fnet_moe_dit"


def _rmsnorm(x, g):
    xf = x.astype(jnp.float32)
    y = xf * lax.rsqrt(jnp.mean(xf * xf, axis=-1, keepdims=True) + EPS)
    return (y * g.astype(jnp.float32)).astype(x.dtype)


def _layernorm(x, g, b):
    xf = x.astype(jnp.float32)
    mu = jnp.mean(xf, axis=-1, keepdims=True)
    xc = xf - mu
    y = xc * lax.rsqrt(jnp.mean(xc * xc, axis=-1, keepdims=True) + EPS)
    return (y * g.astype(jnp.float32) + b.astype(jnp.float32)).astype(x.dtype)


def _modulation(cvec, w, b):
    return jnp.split(jax.nn.silu(cvec) @ w + b, 6, axis=-1)


def _modulate(h, shift, scale):
    return h * (1.0 + scale) + shift


def _axial_rope_tables(rows, rot_dim):
    row = jnp.repeat(jnp.arange(rows, dtype=jnp.float32), GRID_W)
    col = jnp.tile(jnp.arange(GRID_W, dtype=jnp.float32), rows)
    half = rot_dim // 2
    inv = ROPE_THETA ** (-jnp.arange(0, half, 2, dtype=jnp.float32) / half)
    ang_r = row[:, None] * inv[None, :]
    ang_c = col[:, None] * inv[None, :]
    return (jnp.cos(ang_r), jnp.sin(ang_r), jnp.cos(ang_c), jnp.sin(ang_c))


def _rope_rotate(x, cos, sin):
    x1, x2 = jnp.split(x, 2, axis=-1)
    cos = cos[None, :, None, :]
    sin = sin[None, :, None, :]
    return jnp.concatenate([x1 * cos - x2 * sin, x1 * sin + x2 * cos], axis=-1)


def _apply_axial_rope(x, tables):
    cos_r, sin_r, cos_c, sin_c = tables
    x_row, x_col = jnp.split(x.astype(jnp.float32), 2, axis=-1)
    y = jnp.concatenate([_rope_rotate(x_row, cos_r, sin_r), _rope_rotate(x_col, cos_c, sin_c)], axis=-1)
    return y.astype(x.dtype)


def _rope_tail(x, rope):
    if rope is None:
        return x
    return jnp.concatenate([x[..., :MLA_NOPE], _apply_axial_rope(x[..., MLA_NOPE:], rope)], axis=-1)


def _dense_block_attention(q, k, v, scale):
    B, T, H, dk = q.shape
    qb = q.reshape(B, T // BLOCK, BLOCK, H, dk).transpose(1, 0, 2, 3, 4)

    def one_block(q_blk):
        s = jnp.einsum("bqhd,bkhd->bhqk", q_blk, k, preferred_element_type=jnp.float32) * scale
        p = jax.nn.softmax(s, axis=-1).astype(v.dtype)
        return jnp.einsum("bhqk,bkhd->bqhd", p, v)

    out = lax.map(one_block, qb)
    return out.transpose(1, 0, 2, 3, 4).reshape(B, T, -1)


def _sink_softmax(s, sink):
    m = jnp.maximum(jnp.max(s, axis=-1, keepdims=True), sink)
    e = jnp.exp(s - m)
    return e / (jnp.sum(e, axis=-1, keepdims=True) + jnp.exp(sink - m))


def _window_sink_attention(q, k, v, k_ctx, v_ctx, sink, scale):
    B, S, H, d = q.shape
    n_kv = k.shape[2]
    g = H // n_kv
    span = BLOCK + 2 * WINDOW
    pad = ((0, 0), (WINDOW, WINDOW), (0, 0), (0, 0))
    k_pad = jnp.pad(k, pad)
    v_pad = jnp.pad(v, pad)
    qg = q.reshape(B, S, n_kv, g, d)
    sink_g = sink.astype(jnp.float32).reshape(1, n_kv, g, 1, 1)
    offs_q = jnp.arange(BLOCK)
    offs_k = jnp.arange(span)

    def one_block(blk):
        start = blk * BLOCK
        qb = lax.dynamic_slice_in_dim(qg, start, BLOCK, axis=1)
        kb = lax.dynamic_slice_in_dim(k_pad, start, span, axis=1)
        vb = lax.dynamic_slice_in_dim(v_pad, start, span, axis=1)
        q_pos = start + offs_q
        k_pos = start - WINDOW + offs_k
        valid = ((jnp.abs(q_pos[:, None] - k_pos[None, :]) <= WINDOW)
                 & (k_pos >= 0)[None, :] & (k_pos < S)[None, :])
        s_win = jnp.einsum("bqngd,btnd->bngqt", qb, kb, preferred_element_type=jnp.float32) * scale
        s_win = jnp.where(valid, s_win, NEG_INF)
        s_ctx = jnp.einsum("bqngd,bcnd->bngqc", qb, k_ctx, preferred_element_type=jnp.float32) * scale
        p = _sink_softmax(jnp.concatenate([s_win, s_ctx], axis=-1), sink_g).astype(v.dtype)
        o = (jnp.einsum("bngqt,btnd->bqngd", p[..., :span], vb)
             + jnp.einsum("bngqc,bcnd->bqngd", p[..., span:], v_ctx))
        return o.reshape(B, BLOCK, H * d)

    out = lax.map(one_block, jnp.arange(S // BLOCK))
    return out.transpose(1, 0, 2, 3).reshape(B, S, H * d)


def _full_sink_attention(q, k, v, sink, scale):
    B, T, H, d = q.shape
    n_kv = k.shape[2]
    g = H // n_kv
    qg = q.reshape(B, T, n_kv, g, d)
    s = jnp.einsum("bqngd,bcnd->bngqc", qg, k, preferred_element_type=jnp.float32) * scale
    p = _sink_softmax(s, sink.astype(jnp.float32).reshape(1, n_kv, g, 1, 1)).astype(v.dtype)
    return jnp.einsum("bngqc,bcnd->bqngd", p, v).reshape(B, T, H * d)


def _mla_queries(p, q_a_g, w_q_b, q_norm_g, rope):
    B, T, _ = p.shape
    q = (_rmsnorm(p[..., :MLA_Q_RANK], q_a_g) @ w_q_b).reshape(B, T, MLA_HEADS, MLA_NOPE + MLA_ROPE)
    return _rope_tail(_rmsnorm(q, q_norm_g), rope)


def _mla_keys_values(p, kv_a_g, w_kv_b, k_norm_g, rope):
    B, T, _ = p.shape
    kv_a = p[..., MLA_Q_RANK:MLA_Q_RANK + MLA_KV_RANK]
    k_pe = p[..., MLA_Q_RANK + MLA_KV_RANK:MLA_IN]
    kv = (_rmsnorm(kv_a, kv_a_g) @ w_kv_b).reshape(B, T, MLA_HEADS, MLA_NOPE + MLA_V)
    k_nope, v = kv[..., :MLA_NOPE], kv[..., MLA_NOPE:]
    k_pe = jnp.broadcast_to(k_pe[:, :, None, :], (B, T, MLA_HEADS, MLA_ROPE))
    k = _rmsnorm(jnp.concatenate([k_nope, k_pe], axis=-1), k_norm_g)
    return _rope_tail(k, rope), v


def _conformer_conv(u, dw_w, dw_b, ln_g, ln_b):
    a, gate = jnp.split(u, 2, axis=-1)
    h = a * jax.nn.sigmoid(gate)
    h = lax.conv_general_dilated(h, dw_w[:, None, :], window_strides=(1,),
                                 padding=[(CONV_WIDTH // 2, CONV_WIDTH // 2)],
                                 dimension_numbers=("NWC", "WIO", "NWC"),
                                 feature_group_count=CONV_CH) + dw_b
    return jax.nn.silu(_layernorm(h, ln_g, ln_b))


def _fourier(f):
    B, T, _ = f.shape
    fg = f.reshape(B, T, FNET_GROUPS, FNET_CH).astype(jnp.float32)
    y = jnp.fft.fft2(fg, axes=(1, 3), norm="ortho").real
    return y.reshape(B, T, FNET_GROUPS * FNET_CH).astype(f.dtype)


def _even_mixer(u_lat, u_ctx, need_ctx, rope, w_in, q_a_g, w_q_b, kv_a_g, w_kv_b, q_norm_g, k_norm_g,
                dw_w, dw_b, ln_g, ln_b, w_out):
    p_lat = u_lat @ w_in
    p_ctx = u_ctx @ w_in
    scale = (MLA_NOPE + MLA_ROPE) ** -0.5
    q_l = _mla_queries(p_lat, q_a_g, w_q_b, q_norm_g, rope)
    k_l, v_l = _mla_keys_values(p_lat, kv_a_g, w_kv_b, k_norm_g, rope)
    k_c, v_c = _mla_keys_values(p_ctx, kv_a_g, w_kv_b, k_norm_g, None)
    att_l = _dense_block_attention(q_l, jnp.concatenate([k_l, k_c], axis=1),
                                   jnp.concatenate([v_l, v_c], axis=1), scale)
    conv_l = _conformer_conv(p_lat[..., MLA_IN:], dw_w, dw_b, ln_g, ln_b)
    out_l = jnp.concatenate([att_l, conv_l], axis=-1) @ w_out
    if not need_ctx:
        return out_l, None
    q_c = _mla_queries(p_ctx, q_a_g, w_q_b, q_norm_g, None)
    att_c = _dense_block_attention(q_c, k_c, v_c, scale)
    conv_c = _conformer_conv(p_ctx[..., MLA_IN:], dw_w, dw_b, ln_g, ln_b)
    out_c = jnp.concatenate([att_c, conv_c], axis=-1) @ w_out
    return out_l, out_c


def _odd_mixer(u_lat, u_ctx, need_ctx, rope, w_in, q_norm_g, k_norm_g, sink, w_out):
    B, S, _ = u_lat.shape
    L = u_ctx.shape[1]
    scale = HEAD_DIM ** -0.5
    p = u_lat @ w_in
    q = _apply_axial_rope(_rmsnorm(p[..., :ODD_Q].reshape(B, S, WIN_Q_HEADS, HEAD_DIM), q_norm_g), rope)
    k = _apply_axial_rope(_rmsnorm(p[..., ODD_Q:ODD_Q + ODD_KV].reshape(B, S, WIN_KV_HEADS, HEAD_DIM), k_norm_g), rope)
    v = p[..., ODD_Q + ODD_KV:ODD_Q + 2 * ODD_KV].reshape(B, S, WIN_KV_HEADS, HEAD_DIM)
    f = p[..., ODD_Q + 2 * ODD_KV:]
    p_c = u_ctx @ (w_in if need_ctx else w_in[:, ODD_Q:ODD_Q + 2 * ODD_KV])
    off = ODD_Q if need_ctx else 0
    k_c = _rmsnorm(p_c[..., off:off + ODD_KV].reshape(B, L, WIN_KV_HEADS, HEAD_DIM), k_norm_g)
    v_c = p_c[..., off + ODD_KV:off + 2 * ODD_KV].reshape(B, L, WIN_KV_HEADS, HEAD_DIM)
    att = _window_sink_attention(q, k, v, k_c, v_c, sink, scale)
    out_l = jnp.concatenate([att, _fourier(f)], axis=-1) @ w_out
    if not need_ctx:
        return out_l, None
    q_c = _rmsnorm(p_c[..., :ODD_Q].reshape(B, L, WIN_Q_HEADS, HEAD_DIM), q_norm_g)
    att_c = _full_sink_attention(q_c, k_c, v_c, sink, scale)
    out_c = jnp.concatenate([att_c, _fourier(p_c[..., ODD_Q + 2 * ODD_KV:])], axis=-1) @ w_out
    return out_l, out_c


def _swiglu(h, w_gate, w_up, w_down):
    return (jax.nn.silu(h @ w_gate) * (h @ w_up)) @ w_down


def _moe_swiglu(h, router_w, w_gate, w_up, w_down):
    logits = (h @ router_w).astype(jnp.float32)
    top_v, top_i = lax.top_k(logits, TOP_K)
    gates = jax.nn.softmax(top_v, axis=-1)
    flat_e = top_i.reshape(-1)
    order = jnp.argsort(flat_e)
    tok = order // TOP_K
    xs = jnp.take(h, tok, axis=0)
    sizes = jnp.bincount(flat_e, length=N_EXPERTS).astype(jnp.int32)
    hid = jax.nn.silu(lax.ragged_dot(xs, w_gate, sizes)) * lax.ragged_dot(xs, w_up, sizes)
    ys = lax.ragged_dot(hid, w_down, sizes)
    ys = ys * gates.reshape(-1)[order][:, None].astype(ys.dtype)
    return jnp.zeros_like(h).at[tok].add(ys.astype(h.dtype))


def setup_inputs(seed: int = 0) -> dict:
    key = jax.random.key(seed)
    keys = jax.random.split(key, 40)
    counter = [0]
    n_even = (DEPTH + 1) // 2
    n_odd = DEPTH // 2
    D = D_MODEL

    def nrm(shape, scale):
        k = keys[counter[0]]
        counter[0] += 1
        return jax.random.normal(k, shape, jnp.float32) * scale

    def gain(shape):
        return 1.0 + nrm(shape, 0.02)

    return {
        "x": nrm((BATCH, SEQ, D), 1.0),
        "c": nrm((BATCH, D), 1.0),
        "ctx": nrm((BATCH, CTX_LEN, D), 1.0),
        "c_ctx": nrm((D,), 1.0),
        "ada_w": nrm((DEPTH, D, 6 * D), 0.5 * D ** -0.5),
        "ada_b": nrm((DEPTH, 6 * D), 0.01),
        "mix_norm_g": gain((DEPTH, D)),
        "ffn_norm_g": gain((DEPTH, D)),
        "even_w_in": nrm((n_even, D, EVEN_IN), D ** -0.5),
        "mla_q_a_norm_g": gain((n_even, MLA_Q_RANK)),
        "mla_w_q_b": nrm((n_even, MLA_Q_RANK, MLA_HEADS * (MLA_NOPE + MLA_ROPE)), MLA_Q_RANK ** -0.5),
        "mla_kv_a_norm_g": gain((n_even, MLA_KV_RANK)),
        "mla_w_kv_b": nrm((n_even, MLA_KV_RANK, MLA_HEADS * (MLA_NOPE + MLA_V)), MLA_KV_RANK ** -0.5),
        "mla_q_norm_g": gain((n_even, MLA_NOPE + MLA_ROPE)),
        "mla_k_norm_g": gain((n_even, MLA_NOPE + MLA_ROPE)),
        "conv_dw_w": nrm((n_even, CONV_WIDTH, CONV_CH), CONV_WIDTH ** -0.5),
        "conv_dw_b": nrm((n_even, CONV_CH), 0.01),
        "conv_ln_g": gain((n_even, CONV_CH)),
        "conv_ln_b": nrm((n_even, CONV_CH), 0.01),
        "even_w_out": nrm((n_even, EVEN_MIX, D), EVEN_MIX ** -0.5),
        "dense_w_gate": nrm((n_even, D, DENSE_FF), D ** -0.5),
        "dense_w_up": nrm((n_even, D, DENSE_FF), D ** -0.5),
        "dense_w_down": nrm((n_even, DENSE_FF, D), DENSE_FF ** -0.5),
        "odd_w_in": nrm((n_odd, D, ODD_IN), D ** -0.5),
        "swa_q_norm_g": gain((n_odd, HEAD_DIM)),
        "swa_k_norm_g": gain((n_odd, HEAD_DIM)),
        "swa_sink": nrm((n_odd, WIN_Q_HEADS), 1.0),
        "odd_w_out": nrm((n_odd, ODD_MIX, D), ODD_MIX ** -0.5),
        "router_w": nrm((n_odd, D, N_EXPERTS), D ** -0.5),
        "expert_w_gate": nrm((n_odd, N_EXPERTS, D, EXPERT_FF), D ** -0.5),
        "expert_w_up": nrm((n_odd, N_EXPERTS, D, EXPERT_FF), D ** -0.5),
        "expert_w_down": nrm((n_odd, N_EXPERTS, EXPERT_FF, D), EXPERT_FF ** -0.5),
    }


def reference(x, c, ctx, c_ctx, ada_w, ada_b, mix_norm_g, ffn_norm_g,
              even_w_in, mla_q_a_norm_g, mla_w_q_b, mla_kv_a_norm_g, mla_w_kv_b, mla_q_norm_g, mla_k_norm_g,
              conv_dw_w, conv_dw_b, conv_ln_g, conv_ln_b, even_w_out,
              dense_w_gate, dense_w_up, dense_w_down,
              odd_w_in, swa_q_norm_g, swa_k_norm_g, swa_sink, odd_w_out,
              router_w, expert_w_gate, expert_w_up, expert_w_down):
    B, S, D = x.shape
    rows = S // GRID_W
    rope_mla = _axial_rope_tables(rows, MLA_ROPE)
    rope_swa = _axial_rope_tables(rows, HEAD_DIM)
    h_ctx = ctx
    for i in range(DEPTH):
        j = i // 2
        last = i == DEPTH - 1
        sm, cm, gm, sf, cf, gf = [m[:, None, :] for m in _modulation(c, ada_w[i], ada_b[i])]
        xsm, xcm, xgm, xsf, xcf, xgf = _modulation(c_ctx, ada_w[i], ada_b[i])
        u_lat = _modulate(_rmsnorm(x, mix_norm_g[i]), sm, cm)
        u_ctx = _modulate(_rmsnorm(h_ctx, mix_norm_g[i]), xsm, xcm)
        if i % 2 == 0:
            m_lat, m_ctx = _even_mixer(u_lat, u_ctx, not last, rope_mla, even_w_in[j],
                                       mla_q_a_norm_g[j], mla_w_q_b[j], mla_kv_a_norm_g[j], mla_w_kv_b[j],
                                       mla_q_norm_g[j], mla_k_norm_g[j], conv_dw_w[j], conv_dw_b[j],
                                       conv_ln_g[j], conv_ln_b[j], even_w_out[j])
        else:
            m_lat, m_ctx = _odd_mixer(u_lat, u_ctx, not last, rope_swa, odd_w_in[j],
                                      swa_q_norm_g[j], swa_k_norm_g[j], swa_sink[j], odd_w_out[j])
        x = x + gm * m_lat
        f_in = _modulate(_rmsnorm(x, ffn_norm_g[i]), sf, cf).reshape(B * S, D)
        if not last:
            h_ctx = h_ctx + xgm * m_ctx
            f_ctx = _modulate(_rmsnorm(h_ctx, ffn_norm_g[i]), xsf, xcf)
            f_in = jnp.concatenate([f_in, f_ctx.reshape(-1, D)], axis=0)
        if i % 2 == 0:
            f_out = _swiglu(f_in, dense_w_gate[j], dense_w_up[j], dense_w_down[j])
        else:
            f_out = _moe_swiglu(f_in, router_w[j], expert_w_gate[j], expert_w_up[j], expert_w_down[j])
        x = x + gf * f_out[:B * S].reshape(B, S, D)
        if not last:
            h_ctx = h_ctx + xgf * f_out[B * S:].reshape(h_ctx.shape)
    return x
```

```python
import functools

import jax
import jax.numpy as jnp
import numpy as np
from jax import lax
from jax.experimental import pallas as pl
from jax.experimental.pallas import tpu as pltpu

F32 = jnp.float32
BF16 = jnp.bfloat16
U32 = jnp.uint32

EPS = 1e-6
ROPE_THETA = 10000.0
GRID_W = 64
NEG_INF = -1e30
LANES = 128

MLA_HEADS = 8
MLA_Q_RANK = 512
MLA_KV_RANK = 256
MLA_NOPE = 128
MLA_ROPE = 64
MLA_V = 128
MLA_IN = MLA_Q_RANK + MLA_KV_RANK + MLA_ROPE
MLA_QK = MLA_NOPE + MLA_ROPE
MLA_QK_PAD = 2 * LANES
MLA_IN_PAD = 1024
CONV_CH = 1024
CONV_WIDTH = 31
CONV_HALO = 16
HEAD_DIM = 128
WIN_Q_HEADS = 12
WIN_KV_HEADS = 4
WIN_GROUP = WIN_Q_HEADS // WIN_KV_HEADS
WINDOW = 128
BLOCK = 128
FNET_GROUPS = 4
FNET_CH = 128
ODD_Q = WIN_Q_HEADS * HEAD_DIM
ODD_KV = WIN_KV_HEADS * HEAD_DIM
N_EXPERTS = 8

VMEM_LIMIT = 56 * 1024 * 1024


def _cparams(*sem):
    return pltpu.CompilerParams(dimension_semantics=sem, vmem_limit_bytes=VMEM_LIMIT)


def _tile(n, pref):
    if n <= pref:
        return n
    t = pref - pref % 8
    while n % t:
        t -= 8
    return t


def _sigmoid(x):
    return 1.0 / (1.0 + jnp.exp(-x))


def _silu(x):
    return x * _sigmoid(x)


def _rms_mod(x, g, shift, scale):
    ms = jnp.mean(x * x, axis=-1, keepdims=True)
    return (x * lax.rsqrt(ms + EPS) * g) * (1.0 + scale) + shift


def _rope_lanes(t, cos, sin, grp):
    lane = lax.broadcasted_iota(jnp.int32, t.shape, 1)
    first = (lane // grp) % 2 == 0
    swapped = jnp.where(first, pltpu.roll(t, LANES - grp, axis=1), pltpu.roll(t, grp, axis=1))
    return t * cos + swapped * sin


def _pack_pair(a, b):
    ai = lax.bitcast_convert_type(a.astype(BF16).astype(F32), U32)
    bi = lax.bitcast_convert_type(b.astype(BF16).astype(F32), U32)
    return (ai >> 16) | bi


def _unpack_pair(w):
    a = lax.bitcast_convert_type(w << 16, F32)
    b = lax.bitcast_convert_type(w & jnp.uint32(0xFFFF0000), F32)
    return a, b


def _modulation_kernel(c_ref, w_ref, b_ref, o_ref):
    a = _silu(c_ref[...]).astype(BF16)
    acc = jnp.dot(a, w_ref[...].astype(BF16), preferred_element_type=F32)
    o_ref[...] = acc + b_ref[...]


def _modulation(cvec, ada_w, ada_b):
    depth, d, n = ada_w.shape
    r = cvec.shape[0]
    tn = _tile(n, 1024)
    return pl.pallas_call(
        _modulation_kernel,
        grid=(depth, n // tn),
        in_specs=[
            pl.BlockSpec((r, d), lambda l, j: (0, 0)),
            pl.BlockSpec((None, d, tn), lambda l, j: (l, 0, j)),
            pl.BlockSpec((None, 1, tn), lambda l, j: (l, 0, j)),
        ],
        out_specs=pl.BlockSpec((None, r, tn), lambda l, j: (l, 0, j)),
        out_shape=jax.ShapeDtypeStruct((depth, r, n), F32),
        compiler_params=_cparams("parallel", "parallel"),
        name="modulation",
    )(cvec, ada_w, ada_b.reshape(depth, 1, n))


def _nmm_kernel(x_ref, sh_ref, sc_ref, g_ref, w_ref, o_ref, u_ref):
    @pl.when(pl.program_id(1) == 0)
    def _():
        u_ref[...] = _rms_mod(x_ref[...], g_ref[...], sh_ref[0], sc_ref[0]).astype(BF16)

    o_ref[...] = jnp.dot(u_ref[...], w_ref[...], preferred_element_type=F32).astype(o_ref.dtype)


def _norm_mod_matmul(x2, mod, mod_row, which, g, w, tm, tn):
    m, d = x2.shape
    n = w.shape[1]
    return pl.pallas_call(
        _nmm_kernel,
        grid=(m // tm, n // tn),
        in_specs=[
            pl.BlockSpec((tm, d), lambda i, j: (i, 0)),
            pl.BlockSpec((1, 1, d), lambda i, j: (mod_row(i, tm) * 6 + which, 0, 0)),
            pl.BlockSpec((1, 1, d), lambda i, j: (mod_row(i, tm) * 6 + which + 1, 0, 0)),
            pl.BlockSpec((1, d), lambda i, j: (0, 0)),
            pl.BlockSpec((d, tn), lambda i, j: (0, j)),
        ],
        out_specs=pl.BlockSpec((tm, tn), lambda i, j: (i, j)),
        out_shape=jax.ShapeDtypeStruct((m, n), BF16),
        scratch_shapes=[pltpu.VMEM((tm, d), BF16)],
        compiler_params=_cparams("parallel", "arbitrary"),
        name="norm_mod_matmul",
    )(x2, mod, mod, g.reshape(1, d), w)


def _mla_prep_kernel(p_ref, cos_ref, sin_ref, qag_ref, kvag_ref, wq_ref, wkv_ref, qg_ref, kg_ref,
                     q_ref, k_ref, v_ref):
    p = p_ref[...].astype(F32)
    qa = p[:, :MLA_Q_RANK]
    kva = p[:, MLA_Q_RANK:MLA_Q_RANK + MLA_KV_RANK]
    kpe = p[:, MLA_Q_RANK + MLA_KV_RANK:MLA_Q_RANK + MLA_KV_RANK + LANES]
    qn = qa * lax.rsqrt(jnp.mean(qa * qa, axis=-1, keepdims=True) + EPS) * qag_ref[...]
    kvn = kva * lax.rsqrt(jnp.mean(kva * kva, axis=-1, keepdims=True) + EPS) * kvag_ref[...]
    q = jnp.dot(qn.astype(BF16), wq_ref[...], preferred_element_type=F32)
    kv = jnp.dot(kvn.astype(BF16), wkv_ref[...], preferred_element_type=F32)
    cos = cos_ref[...]
    sin = sin_ref[...]
    qg = qg_ref[...]
    kg = kg_ref[...]
    kpe_ss = jnp.sum(kpe * kpe, axis=-1, keepdims=True)
    for h in range(MLA_HEADS):
        lo = h * MLA_QK_PAD
        qh = q[:, lo:lo + MLA_QK_PAD]
        rs = lax.rsqrt(jnp.sum(qh * qh, axis=-1, keepdims=True) * (1.0 / MLA_QK) + EPS)
        q_ref[0, h, :, :LANES] = (qh[:, :LANES] * rs * qg[:, :LANES]).astype(BF16)
        tail = _rope_lanes(qh[:, LANES:] * rs * qg[:, LANES:], cos, sin, MLA_ROPE // 4)
        q_ref[0, h, :, LANES:] = tail.astype(BF16)
        kn = kv[:, lo:lo + MLA_NOPE]
        rs = lax.rsqrt((jnp.sum(kn * kn, axis=-1, keepdims=True) + kpe_ss) * (1.0 / MLA_QK) + EPS)
        k_ref[0, h, :, :LANES] = (kn * rs * kg[:, :LANES]).astype(BF16)
        tail = _rope_lanes(kpe * rs * kg[:, LANES:], cos, sin, MLA_ROPE // 4)
        k_ref[0, h, :, LANES:] = tail.astype(BF16)
        v_ref[0, h] = kv[:, lo + MLA_NOPE:lo + MLA_NOPE + MLA_V].astype(BF16)


def _mla_prep(p, b, t, col_block, cos, sin, qag, kvag, wq, wkv, qg, kg):
    tm = _tile(t, 256)
    nt = t // tm
    const = lambda bb, i: (0, 0)
    hs = lambda w: pl.BlockSpec((1, MLA_HEADS, tm, w), lambda bb, i: (bb, 0, i, 0))
    return pl.pallas_call(
        _mla_prep_kernel,
        grid=(b, nt),
        in_specs=[
            pl.BlockSpec((tm, MLA_IN_PAD), lambda bb, i: (bb * nt + i, col_block)),
            pl.BlockSpec((tm, LANES), lambda bb, i: (i, 0)),
            pl.BlockSpec((tm, LANES), lambda bb, i: (i, 0)),
            pl.BlockSpec((1, MLA_Q_RANK), const),
            pl.BlockSpec((1, MLA_KV_RANK), const),
            pl.BlockSpec(wq.shape, const),
            pl.BlockSpec(wkv.shape, const),
            pl.BlockSpec((1, MLA_QK_PAD), const),
            pl.BlockSpec((1, MLA_QK_PAD), const),
        ],
        out_specs=[hs(MLA_QK_PAD), hs(MLA_QK_PAD), hs(MLA_V)],
        out_shape=[
            jax.ShapeDtypeStruct((b, MLA_HEADS, t, MLA_QK_PAD), BF16),
            jax.ShapeDtypeStruct((b, MLA_HEADS, t, MLA_QK_PAD), BF16),
            jax.ShapeDtypeStruct((b, MLA_HEADS, t, MLA_V), BF16),
        ],
        compiler_params=_cparams("parallel", "parallel"),
        name="mla_prep",
    )(p, cos, sin, qag, kvag, wq, wkv, qg, kg)


def _nt_dot(a, b):
    return lax.dot_general(a, b, (((1,), (1,)), ((), ())), preferred_element_type=F32)


def _mla_attn_kernel(nseg, q_ref, *refs):
    o_ref = refs[2 * nseg]
    q = q_ref[0, 0]
    s = [_nt_dot(q, refs[2 * i][0, 0]) for i in range(nseg)]
    m = s[0].max(axis=-1, keepdims=True)
    for si in s[1:]:
        m = jnp.maximum(m, si.max(axis=-1, keepdims=True))
    den = 0.0
    acc = 0.0
    for i in range(nseg):
        e = jnp.exp(s[i] - m)
        den = den + jnp.sum(e, axis=-1, keepdims=True)
        acc = acc + jnp.dot(e.astype(BF16), refs[2 * i + 1][0, 0], preferred_element_type=F32)
    o_ref[0] = (acc / den).astype(o_ref.dtype)


def _mla_attn(q, kvs):
    b, h, t, _ = q.shape
    tq = _tile(t, 256)
    in_specs = [pl.BlockSpec((1, 1, tq, MLA_QK_PAD), lambda bb, hh, i: (bb, hh, i, 0))]
    args = [q]
    for k, v in kvs:
        n = k.shape[2]
        in_specs.append(pl.BlockSpec((1, 1, n, MLA_QK_PAD), lambda bb, hh, i: (bb, hh, 0, 0)))
        in_specs.append(pl.BlockSpec((1, 1, n, MLA_V), lambda bb, hh, i: (bb, hh, 0, 0)))
        args += [k, v]
    return pl.pallas_call(
        functools.partial(_mla_attn_kernel, len(kvs)),
        grid=(b, h, t // tq),
        in_specs=in_specs,
        out_specs=pl.BlockSpec((1, tq, MLA_V), lambda bb, hh, i: (bb, i, hh)),
        out_shape=jax.ShapeDtypeStruct((b, t, h * MLA_V), BF16),
        compiler_params=_cparams("parallel", "parallel", "parallel"),
        name="mla_attn",
    )(*args)


CONV_ROWS = 64
CONV_WIN = CONV_ROWS + 2 * CONV_HALO


def _conv_kernel(t, p_ref, w_ref, b_ref, g_ref, beta_ref, o_ref, hp_ref, cv_ref):
    zeros = jnp.zeros((CONV_HALO, CONV_CH), F32)
    hp_ref[0:CONV_HALO, :] = zeros
    hp_ref[CONV_HALO + t:CONV_HALO + t + CONV_HALO, :] = zeros

    def glu(i, c):
        r0 = pl.multiple_of(i * CONV_ROWS, CONV_ROWS)
        a = p_ref[pl.ds(r0, CONV_ROWS), :CONV_CH].astype(F32)
        gate = p_ref[pl.ds(r0, CONV_ROWS), CONV_CH:].astype(F32)
        hp_ref[pl.ds(r0 + CONV_HALO, CONV_ROWS), :] = a * _sigmoid(gate)
        return c

    lax.fori_loop(0, t // CONV_ROWS, glu, 0)

    def tile(i, c):
        r0 = pl.multiple_of(i * CONV_ROWS, CONV_ROWS)
        for cc in range(CONV_CH // LANES):
            cs = slice(cc * LANES, (cc + 1) * LANES)
            win = hp_ref[pl.ds(r0, CONV_WIN), cs]
            acc = jnp.zeros((CONV_ROWS, LANES), F32) + b_ref[:, cs]
            for r in range(8):
                rolled = win if r == 0 else pltpu.roll(win, CONV_WIN - r, axis=0)
                for k in range(CONV_WIDTH):
                    off = k + CONV_HALO - CONV_WIDTH // 2
                    if off % 8 == r:
                        acc = acc + rolled[off - r:off - r + CONV_ROWS] * w_ref[k:k + 1, cs]
            cv_ref[:, cs] = acc
        h = cv_ref[...]
        mu = jnp.mean(h, axis=-1, keepdims=True)
        hc = h - mu
        y = hc * lax.rsqrt(jnp.mean(hc * hc, axis=-1, keepdims=True) + EPS) * g_ref[...] + beta_ref[...]
        o_ref[pl.ds(r0, CONV_ROWS), :] = _silu(y).astype(o_ref.dtype)
        return c

    lax.fori_loop(0, t // CONV_ROWS, tile, 0)


def _conv_module(p, b, t, dw_w, dw_b, ln_g, ln_b):
    const = lambda bb: (0, 0)
    return pl.pallas_call(
        functools.partial(_conv_kernel, t),
        grid=(b,),
        in_specs=[
            pl.BlockSpec((t, 2 * CONV_CH), lambda bb: (bb, 0)),
            pl.BlockSpec((CONV_WIDTH, CONV_CH), const),
            pl.BlockSpec((1, CONV_CH), const),
            pl.BlockSpec((1, CONV_CH), const),
            pl.BlockSpec((1, CONV_CH), const),
        ],
        out_specs=pl.BlockSpec((t, CONV_CH), lambda bb: (bb, 0)),
        out_shape=jax.ShapeDtypeStruct((b * t, CONV_CH), BF16),
        scratch_shapes=[pltpu.VMEM((t + 2 * CONV_HALO, CONV_CH), F32), pltpu.VMEM((CONV_ROWS, CONV_CH), F32)],
        compiler_params=_cparams("parallel"),
        name="conv_module",
    )(p, dw_w, dw_b.reshape(1, -1), ln_g.reshape(1, -1), ln_b.reshape(1, -1))


def _oproj_kernel(a1_ref, a2_ref, w1_ref, w2_ref, x_ref, gate_ref, o_ref):
    acc = jnp.dot(a1_ref[...], w1_ref[...], preferred_element_type=F32)
    acc = acc + jnp.dot(a2_ref[...], w2_ref[...], preferred_element_type=F32)
    o_ref[...] = x_ref[...] + gate_ref[0] * acc


def _out_proj_residual(a1, a2, w1, w2, x2, mod, mod_row, which, tm):
    m, d = x2.shape
    k1, k2 = w1.shape[0], w2.shape[0]
    const = lambda i: (0, 0)
    return pl.pallas_call(
        _oproj_kernel,
        grid=(m // tm,),
        in_specs=[
            pl.BlockSpec((tm, k1), lambda i: (i, 0)),
            pl.BlockSpec((tm, k2), lambda i: (i, 0)),
            pl.BlockSpec((k1, d), const),
            pl.BlockSpec((k2, d), const),
            pl.BlockSpec((tm, d), lambda i: (i, 0)),
            pl.BlockSpec((1, 1, d), lambda i: (mod_row(i, tm) * 6 + which, 0, 0)),
        ],
        out_specs=pl.BlockSpec((tm, d), lambda i: (i, 0)),
        out_shape=jax.ShapeDtypeStruct((m, d), F32),
        compiler_params=_cparams("parallel"),
        name="out_proj_residual",
    )(a1, a2, w1, w2, x2, mod)


def _ffn_kernel(x_ref, sh_ref, sc_ref, gate_ref, g_ref, wg_ref, wu_ref, wd_ref, o_ref, u_ref, acc_ref):
    f = pl.program_id(1)

    @pl.when(f == 0)
    def _():
        u_ref[...] = _rms_mod(x_ref[...], g_ref[...], sh_ref[0], sc_ref[0]).astype(BF16)
        acc_ref[...] = jnp.zeros_like(acc_ref)

    u = u_ref[...]
    hg = jnp.dot(u, wg_ref[...], preferred_element_type=F32)
    hu = jnp.dot(u, wu_ref[...], preferred_element_type=F32)
    hid = (_silu(hg) * hu).astype(BF16)
    acc_ref[...] += jnp.dot(hid, wd_ref[...], preferred_element_type=F32)

    @pl.when(f == pl.num_programs(1) - 1)
    def _():
        o_ref[...] = x_ref[...] + gate_ref[0] * acc_ref[...]


def _dense_ffn(x2, mod, mod_row, g, wg, wu, wd, tm, tf):
    m, d = x2.shape
    ff = wg.shape[1]
    row = lambda w: pl.BlockSpec((1, 1, d), lambda i, f: (mod_row(i, tm) * 6 + w, 0, 0))
    return pl.pallas_call(
        _ffn_kernel,
        grid=(m // tm, ff // tf),
        in_specs=[
            pl.BlockSpec((tm, d), lambda i, f: (i, 0)),
            row(3), row(4), row(5),
            pl.BlockSpec((1, d), lambda i, f: (0, 0)),
            pl.BlockSpec((d, tf), lambda i, f: (0, f)),
            pl.BlockSpec((d, tf), lambda i, f: (0, f)),
            pl.BlockSpec((tf, d), lambda i, f: (f, 0)),
        ],
        out_specs=pl.BlockSpec((tm, d), lambda i, f: (i, 0)),
        out_shape=jax.ShapeDtypeStruct((m, d), F32),
        scratch_shapes=[pltpu.VMEM((tm, d), BF16), pltpu.VMEM((tm, d), F32)],
        compiler_params=_cparams("parallel", "arbitrary"),
        name="dense_ffn",
    )(x2, mod, mod, mod, g.reshape(1, d), wg, wu, wd)


def _swa_prep_kernel(nheads, p_ref, cos_ref, sin_ref, g_ref, o_ref):
    cos = cos_ref[...]
    sin = sin_ref[...]
    for h in range(nheads):
        cs = slice(h * HEAD_DIM, (h + 1) * HEAD_DIM)
        t = p_ref[:, cs].astype(F32)
        t = t * lax.rsqrt(jnp.mean(t * t, axis=-1, keepdims=True) + EPS) * g_ref[:, cs]
        o_ref[:, cs] = _rope_lanes(t, cos, sin, HEAD_DIM // 4).astype(BF16)


def _swa_prep(p, b, t, nheads, col_block, cos, sin, gains):
    tm = _tile(t, 256)
    nt = t // tm
    w = nheads * HEAD_DIM
    return pl.pallas_call(
        functools.partial(_swa_prep_kernel, nheads),
        grid=(b, nt),
        in_specs=[
            pl.BlockSpec((tm, w), lambda bb, i: (bb * nt + i, col_block)),
            pl.BlockSpec((tm, LANES), lambda bb, i: (i, 0)),
            pl.BlockSpec((tm, LANES), lambda bb, i: (i, 0)),
            pl.BlockSpec((1, w), lambda bb, i: (0, 0)),
        ],
        out_specs=pl.BlockSpec((tm, w), lambda bb, i: (bb * nt + i, 0)),
        out_shape=jax.ShapeDtypeStruct((b * t, w), BF16),
        compiler_params=_cparams("parallel", "parallel"),
        name="swa_prep",
    )(p, cos, sin, gains)


def _swa_attn_kernel(s_len, sink_ref, q_ref, k_ref, v_ref, kc_ref, vc_ref, o_ref):
    n = pl.program_id(1)
    blk = pl.program_id(2)
    span = BLOCK + 2 * WINDOW
    start = blk * BLOCK
    ws = pl.multiple_of(jnp.clip(start - WINDOW, 0, s_len - span), BLOCK)
    kw = k_ref[0, pl.ds(ws, span), :]
    vw = v_ref[0, pl.ds(ws, span), :]
    rows = WIN_GROUP * BLOCK
    q = jnp.concatenate([q_ref[0, :, g * HEAD_DIM:(g + 1) * HEAD_DIM] for g in range(WIN_GROUP)], axis=0)
    s_w = _nt_dot(q, kw)
    s_c = _nt_dot(q, kc_ref[0])
    row = lax.broadcasted_iota(jnp.int32, (rows, span), 0)
    col = lax.broadcasted_iota(jnp.int32, (rows, span), 1)
    q_pos = start + row % BLOCK
    k_pos = ws + col
    s_w = jnp.where(jnp.abs(q_pos - k_pos) <= WINDOW, s_w, NEG_INF)
    rcol = lax.broadcasted_iota(jnp.int32, (rows, 1), 0) // BLOCK
    sink = jnp.zeros((rows, 1), F32)
    for g in range(WIN_GROUP):
        sink = jnp.where(rcol == g, sink_ref[n * WIN_GROUP + g], sink)
    m = jnp.maximum(jnp.maximum(s_w.max(axis=-1, keepdims=True), s_c.max(axis=-1, keepdims=True)), sink)
    e_w = jnp.exp(s_w - m)
    e_c = jnp.exp(s_c - m)
    den = jnp.sum(e_w, axis=-1, keepdims=True) + jnp.sum(e_c, axis=-1, keepdims=True) + jnp.exp(sink - m)
    acc = jnp.dot(e_w.astype(BF16), vw, preferred_element_type=F32)
    acc = acc + jnp.dot(e_c.astype(BF16), vc_ref[0], preferred_element_type=F32)
    out = acc / den
    for g in range(WIN_GROUP):
        o_ref[0, :, g * HEAD_DIM:(g + 1) * HEAD_DIM] = out[g * BLOCK:(g + 1) * BLOCK].astype(o_ref.dtype)


def _swa_attn(qk, p, kc, pc, sink, b, s_len, ctx_len):
    qk3 = qk.reshape(b, s_len, -1)
    p3 = p.reshape(b, s_len, -1)
    kc3 = kc.reshape(b, ctx_len, -1)
    pc3 = pc.reshape(b, ctx_len, -1)
    gw = WIN_GROUP * HEAD_DIM
    kcol = ODD_Q // HEAD_DIM
    vcol = (ODD_Q + ODD_KV) // HEAD_DIM
    return pl.pallas_call(
        functools.partial(_swa_attn_kernel, s_len),
        grid=(b, WIN_KV_HEADS, s_len // BLOCK),
        in_specs=[
            pl.BlockSpec(memory_space=pltpu.SMEM),
            pl.BlockSpec((1, BLOCK, gw), lambda bb, n, i: (bb, i, n)),
            pl.BlockSpec((1, s_len, HEAD_DIM), lambda bb, n, i: (bb, 0, kcol + n)),
            pl.BlockSpec((1, s_len, HEAD_DIM), lambda bb, n, i: (bb, 0, vcol + n)),
            pl.BlockSpec((1, ctx_len, HEAD_DIM), lambda bb, n, i: (bb, 0, n)),
            pl.BlockSpec((1, ctx_len, HEAD_DIM), lambda bb, n, i: (bb, 0, WIN_KV_HEADS + n)),
        ],
        out_specs=pl.BlockSpec((1, BLOCK, gw), lambda bb, n, i: (bb, i, n)),
        out_shape=jax.ShapeDtypeStruct((b, s_len, ODD_Q), BF16),
        compiler_params=_cparams("parallel", "parallel", "parallel"),
        name="swa_attn",
    )(sink, qk3, qk3, p3, kc3, pc3)


def _fourier_kernel(scale, f_ref, cs_ref, ct_ref, st_ref, o_ref, xc_ref, xs_ref):
    @pl.when(pl.program_id(1) == 0)
    def _():
        for g in range(FNET_GROUPS):
            cs = slice(g * FNET_CH, (g + 1) * FNET_CH)
            r = jnp.dot(f_ref[0, :, cs], cs_ref[...], preferred_element_type=F32)
            xc_ref[:, cs] = r[:, :FNET_CH].astype(BF16)
            xs_ref[:, cs] = r[:, FNET_CH:].astype(BF16)

    y = jnp.dot(ct_ref[...], xc_ref[...], preferred_element_type=F32)
    y = y - jnp.dot(st_ref[...], xs_ref[...], preferred_element_type=F32)
    o_ref[0] = (y * scale).astype(o_ref.dtype)


def _dft_tables(n):
    k = jnp.arange(n, dtype=jnp.int32)
    ang = ((k[:, None] * k[None, :]) % n).astype(F32) * (2.0 * np.pi / n)
    return jnp.cos(ang), jnp.sin(ang)


def _fourier(p, b, t, col_block):
    w = FNET_GROUPS * FNET_CH
    p3 = p.reshape(b, t, -1)
    cc, sc = _dft_tables(FNET_CH)
    ct, st = _dft_tables(t)
    cs = jnp.concatenate([cc, sc], axis=1).astype(BF16)
    tk = _tile(t, 512)
    return pl.pallas_call(
        functools.partial(_fourier_kernel, float((t * FNET_CH) ** -0.5)),
        grid=(b, t // tk),
        in_specs=[
            pl.BlockSpec((1, t, w), lambda bb, i: (bb, 0, col_block)),
            pl.BlockSpec((FNET_CH, 2 * FNET_CH), lambda bb, i: (0, 0)),
            pl.BlockSpec((tk, t), lambda bb, i: (i, 0)),
            pl.BlockSpec((tk, t), lambda bb, i: (i, 0)),
        ],
        out_specs=pl.BlockSpec((1, tk, w), lambda bb, i: (bb, i, 0)),
        out_shape=jax.ShapeDtypeStruct((b, t, w), BF16),
        scratch_shapes=[pltpu.VMEM((t, w), BF16), pltpu.VMEM((t, w), BF16)],
        compiler_params=_cparams("parallel", "arbitrary"),
        name="fourier",
    )(p3, cs, ct.astype(BF16), st.astype(BF16))


def _router_kernel(x_ref, sh_ref, sc_ref, g_ref, wh_ref, wl_ref, fin_ref, info_ref, cnt_ref, run_ref):
    @pl.when(pl.program_id(0) == 0)
    def _():
        run_ref[...] = jnp.zeros_like(run_ref)

    u = _rms_mod(x_ref[...], g_ref[...], sh_ref[0], sc_ref[0])
    d = u.shape[1]
    fin_ref[...] = _pack_pair(u[:, :d // 2], u[:, d // 2:])
    u_hi = u.astype(BF16)
    u_lo = (u - u_hi.astype(F32)).astype(BF16)
    logits = (jnp.dot(u_hi, wh_ref[...], preferred_element_type=F32)
              + (jnp.dot(u_lo, wh_ref[...], preferred_element_type=F32)
                 + jnp.dot(u_hi, wl_ref[...], preferred_element_type=F32)))
    tr = logits.shape[0]
    lane = lax.broadcasted_iota(jnp.int32, (tr, LANES), 1).astype(F32)
    logits = jnp.where(lane < N_EXPERTS, logits, -jnp.inf)
    m1 = logits.max(axis=-1, keepdims=True)
    i1 = jnp.where(logits == m1, lane, float(LANES)).min(axis=-1, keepdims=True)
    rest = jnp.where(lane == i1, -jnp.inf, logits)
    m2 = rest.max(axis=-1, keepdims=True)
    i2 = jnp.where(rest == m2, lane, float(LANES)).min(axis=-1, keepdims=True)
    e21 = jnp.exp(m2 - m1)
    g1 = 1.0 / (1.0 + e21)
    g2 = e21 / (1.0 + e21)
    oh1 = lane == i1
    oh2 = lane == i2
    oh = (oh1 | oh2).astype(F32)
    r = lax.broadcasted_iota(jnp.int32, (tr, tr), 0)
    c = lax.broadcasted_iota(jnp.int32, (tr, tr), 1)
    before = (r > c).astype(BF16)
    prefix = jnp.dot(before, oh.astype(BF16), preferred_element_type=F32) + run_ref[...]
    r1 = jnp.sum(jnp.where(oh1, prefix, 0.0), axis=-1, keepdims=True)
    r2 = jnp.sum(jnp.where(oh2, prefix, 0.0), axis=-1, keepdims=True)
    run = run_ref[...] + jnp.sum(oh, axis=0, keepdims=True)
    run_ref[...] = run
    cnt_ref[...] = run
    info = jnp.zeros((tr, LANES), F32)
    for j, val in enumerate((i1, i2, r1, r2, g1, g2)):
        info = jnp.where(lane == j, val, info)
    info_ref[...] = info[:, :8]


def _router(x2, mod, mod_row, g, wr_hi, wr_lo, tr):
    m, d = x2.shape
    row = lambda w: pl.BlockSpec((1, 1, d), lambda i: (mod_row(i, tr) * 6 + w, 0, 0))
    const = lambda i: (0, 0)
    return pl.pallas_call(
        _router_kernel,
        grid=(m // tr,),
        in_specs=[
            pl.BlockSpec((tr, d), lambda i: (i, 0)),
            row(3), row(4),
            pl.BlockSpec((1, d), const),
            pl.BlockSpec((d, LANES), const),
            pl.BlockSpec((d, LANES), const),
        ],
        out_specs=[
            pl.BlockSpec((tr, d // 2), lambda i: (i, 0)),
            pl.BlockSpec((tr, 8), lambda i: (i, 0)),
            pl.BlockSpec((1, LANES), const),
        ],
        out_shape=[
            jax.ShapeDtypeStruct((m, d // 2), U32),
            jax.ShapeDtypeStruct((m, 8), F32),
            jax.ShapeDtypeStruct((1, LANES), F32),
        ],
        scratch_shapes=[pltpu.VMEM((1, LANES), F32)],
        compiler_params=_cparams("arbitrary"),
        name="moe_router",
    )(x2, mod, mod, g.reshape(1, d), wr_hi, wr_lo)


def _row_copy(src, src_row, dst, dst_row, sem):
    return pltpu.make_async_copy(src.at[pl.ds(src_row, 1)], dst.at[pl.ds(dst_row, 1)], sem)


def _dispatch_kernel(tg, pos_ref, fin_ref, init_ref, xs_ref, sem):
    del init_ref

    def issue(t, c):
        _row_copy(fin_ref, t, xs_ref, pos_ref[0, 2 * t], sem).start()
        _row_copy(fin_ref, t, xs_ref, pos_ref[0, 2 * t + 1], sem).start()
        return c

    lax.fori_loop(0, tg, issue, 0)

    def drain(t, c):
        _row_copy(fin_ref, 0, xs_ref, 0, sem).wait()
        return c

    lax.fori_loop(0, 2 * tg, drain, 0)


def _dispatch(fin, pos, cap, tg):
    m, w = fin.shape
    return pl.pallas_call(
        functools.partial(_dispatch_kernel, tg),
        grid=(m // tg,),
        in_specs=[
            pl.BlockSpec((None, 1, 2 * tg), lambda i: (i, 0, 0), memory_space=pltpu.SMEM),
            pl.BlockSpec((tg, w), lambda i: (i, 0)),
            pl.BlockSpec(memory_space=pl.ANY),
        ],
        out_specs=pl.BlockSpec(memory_space=pl.ANY),
        out_shape=jax.ShapeDtypeStruct((cap, w), U32),
        scratch_shapes=[pltpu.SemaphoreType.DMA(())],
        input_output_aliases={2: 0},
        compiler_params=_cparams("arbitrary"),
        name="moe_dispatch",
    )(pos.reshape(m // tg, 1, 2 * tg), fin, jnp.zeros((cap, w), U32))


def _moe_ffn_kernel(te_ref, act_ref, xs_ref, wg_ref, wu_ref, wd_ref, ys_ref, u_ref, acc_ref):
    i = pl.program_id(0)
    f = pl.program_id(1)
    d = u_ref.shape[1]

    @pl.when(f == 0)
    def _():
        a, b = _unpack_pair(xs_ref[...])
        u_ref[:, :d // 2] = a.astype(BF16)
        u_ref[:, d // 2:] = b.astype(BF16)
        acc_ref[...] = jnp.zeros_like(acc_ref)

    @pl.when(act_ref[i] == 1)
    def _():
        u = u_ref[...]
        hg = jnp.dot(u, wg_ref[...], preferred_element_type=F32)
        hu = jnp.dot(u, wu_ref[...], preferred_element_type=F32)
        hid = (_silu(hg) * hu).astype(BF16)
        acc_ref[...] += jnp.dot(hid, wd_ref[...], preferred_element_type=F32)

    @pl.when(f == pl.num_programs(1) - 1)
    def _():
        ys_ref[...] = _pack_pair(acc_ref[:, :d // 2], acc_ref[:, d // 2:])


def _moe_ffn(xs, tile_expert, tile_active, wg, wu, wd, tm, tf):
    cap, w = xs.shape
    d = 2 * w
    ff = wg.shape[2]
    nf = ff // tf

    def f_eff(i, f, act):
        return jnp.where(act[i] == 1, f, nf - 1)

    grid_spec = pltpu.PrefetchScalarGridSpec(
        num_scalar_prefetch=2,
        grid=(cap // tm, nf),
        in_specs=[
            pl.BlockSpec((tm, w), lambda i, f, te, act: (i, 0)),
            pl.BlockSpec((None, d, tf), lambda i, f, te, act: (te[i], 0, f_eff(i, f, act))),
            pl.BlockSpec((None, d, tf), lambda i, f, te, act: (te[i], 0, f_eff(i, f, act))),
            pl.BlockSpec((None, tf, d), lambda i, f, te, act: (te[i], f_eff(i, f, act), 0)),
        ],
        out_specs=pl.BlockSpec((tm, w), lambda i, f, te, act: (i, 0)),
        scratch_shapes=[pltpu.VMEM((tm, d), BF16), pltpu.VMEM((tm, d), F32)],
    )
    return pl.pallas_call(
        _moe_ffn_kernel,
        grid_spec=grid_spec,
        out_shape=jax.ShapeDtypeStruct((cap, w), U32),
        compiler_params=_cparams("parallel", "arbitrary"),
        name="moe_ffn",
    )(tile_expert, tile_active, xs, wg, wu, wd)


def _combine_kernel(tc, pos_ref, x_ref, info_ref, gate_ref, ys_ref, o_ref, buf_ref, sem):
    def issue(t, c):
        _row_copy(ys_ref, pos_ref[0, 2 * t], buf_ref.at[0], t, sem).start()
        _row_copy(ys_ref, pos_ref[0, 2 * t + 1], buf_ref.at[1], t, sem).start()
        return c

    lax.fori_loop(0, tc, issue, 0)

    def drain(t, c):
        _row_copy(ys_ref, 0, buf_ref.at[0], 0, sem).wait()
        return c

    lax.fori_loop(0, 2 * tc, drain, 0)
    w = buf_ref.shape[2]
    g1 = info_ref[:, 4:5]
    g2 = info_ref[:, 5:6]
    a1, b1 = _unpack_pair(buf_ref[0])
    a2, b2 = _unpack_pair(buf_ref[1])
    gate = gate_ref[0]
    o_ref[:, :w] = x_ref[:, :w] + gate[:, :w] * (g1 * a1 + g2 * a2)
    o_ref[:, w:] = x_ref[:, w:] + gate[:, w:] * (g1 * b1 + g2 * b2)


def _combine(x2, info, pos, ys, mod, mod_row, tc):
    m, d = x2.shape
    w = ys.shape[1]
    return pl.pallas_call(
        functools.partial(_combine_kernel, tc),
        grid=(m // tc,),
        in_specs=[
            pl.BlockSpec((None, 1, 2 * tc), lambda i: (i, 0, 0), memory_space=pltpu.SMEM),
            pl.BlockSpec((tc, d), lambda i: (i, 0)),
            pl.BlockSpec((tc, 8), lambda i: (i, 0)),
            pl.BlockSpec((1, 1, d), lambda i: (mod_row(i, tc) * 6 + 5, 0, 0)),
            pl.BlockSpec(memory_space=pl.ANY),
        ],
        out_specs=pl.BlockSpec((tc, d), lambda i: (i, 0)),
        out_shape=jax.ShapeDtypeStruct((m, d), F32),
        scratch_shapes=[pltpu.VMEM((2, tc, w), U32), pltpu.SemaphoreType.DMA(())],
        compiler_params=_cparams("arbitrary"),
        name="moe_combine",
    )(pos.reshape(m // tc, 1, 2 * tc), x2, info, mod, ys)


def _moe(x2, s, mod, mod_row, g, router_w, wg, wu, wd):
    m, d = x2.shape
    tm = _tile(m, 512)
    tf = _tile(wg.shape[2], 512)
    tr = _tile(s, 512)
    tg = _tile(s, 256)
    wr = jnp.zeros((d, LANES), F32).at[:, :N_EXPERTS].set(router_w)
    wr_hi = wr.astype(BF16)
    wr_lo = (wr - wr_hi.astype(F32)).astype(BF16)
    fin, info, cnt = _router(x2, mod, mod_row, g, wr_hi, wr_lo, tr)

    counts = cnt[0, :N_EXPERTS].astype(jnp.int32)
    padded = (counts + tm - 1) // tm * tm
    ends = jnp.cumsum(padded)
    starts = ends - padded
    experts = info[:, 0:2].astype(jnp.int32)
    pos = starts[experts] + info[:, 2:4].astype(jnp.int32)
    ntiles = (2 * m) // tm + N_EXPERTS
    cap = ntiles * tm
    tile_start = jnp.arange(ntiles, dtype=jnp.int32) * tm
    tile_expert = jnp.sum(tile_start[:, None] >= ends[None, :], axis=1).astype(jnp.int32)
    tile_active = (tile_start < ends[-1]).astype(jnp.int32)
    last_expert = tile_expert[ends[-1] // tm - 1]
    tile_expert = jnp.where(tile_active == 1, tile_expert, last_expert)

    xs = _dispatch(fin, pos, cap, tg)
    ys = _moe_ffn(xs, tile_expert, tile_active, wg, wu, wd, tm, tf)
    return _combine(x2, info, pos, ys, mod, mod_row, tg)


def _rope_tables(t, rot_dim):
    rows = t // GRID_W
    row = jnp.repeat(jnp.arange(rows, dtype=F32), GRID_W)
    col = jnp.tile(jnp.arange(GRID_W, dtype=F32), rows)
    half = rot_dim // 2
    inv = ROPE_THETA ** (-jnp.arange(0, half, 2, dtype=F32) / half)
    ang_r = row[:, None] * inv[None, :]
    ang_c = col[:, None] * inv[None, :]
    pad = LANES - rot_dim
    cos = jnp.concatenate([jnp.cos(ang_r), jnp.cos(ang_r), jnp.cos(ang_c), jnp.cos(ang_c),
                           jnp.ones((t, pad), F32)], axis=1)
    sin = jnp.concatenate([-jnp.sin(ang_r), jnp.sin(ang_r), -jnp.sin(ang_c), jnp.sin(ang_c),
                           jnp.zeros((t, pad), F32)], axis=1)
    return cos, sin


def _identity_rope(t):
    return jnp.ones((t, LANES), F32), jnp.zeros((t, LANES), F32)


def _pad_heads(w, real, padded):
    k = w.shape[0]
    w = w.reshape(k, MLA_HEADS, real)
    return jnp.pad(w, ((0, 0), (0, 0), (0, padded - real))).reshape(k, MLA_HEADS * padded)


def kernel(x, c, ctx, c_ctx, ada_w, ada_b, mix_norm_g, ffn_norm_g, even_w_in, mla_q_a_norm_g, mla_w_q_b, mla_kv_a_norm_g, mla_w_kv_b, mla_q_norm_g, mla_k_norm_g, conv_dw_w, conv_dw_b, conv_ln_g, conv_ln_b, even_w_out, dense_w_gate, dense_w_up, dense_w_down, odd_w_in, swa_q_norm_g, swa_k_norm_g, swa_sink, odd_w_out, router_w, expert_w_gate, expert_w_up, expert_w_down):
    b, s, d = x.shape
    l = ctx.shape[1]
    assert ada_w.shape[0] == 2, "two layers: an even (MLA | conv, dense) then an odd (SWA | Fourier, MoE) one"

    r = (b + 1 + 7) // 8 * 8
    cvec = jnp.zeros((r, d), F32).at[:b].set(c).at[b].set(c_ctx)
    mod = _modulation(cvec, ada_w, ada_b).reshape(2, r * 6, 1, d)
    mod0, mod1 = mod[0], mod[1]

    tm = _tile(s, 512)
    tml = _tile(b * l, 512)
    lat_row = lambda i, tile: (i * tile) // s
    ctx_row = lambda i, tile: b
    x2 = x.reshape(b * s, d)
    h2 = ctx.reshape(b * l, d)

    w_in = even_w_in[0]
    w_in0 = jnp.concatenate(
        [w_in[:, MLA_IN:], w_in[:, :MLA_IN], jnp.zeros((d, MLA_IN_PAD - MLA_IN), F32)], axis=1).astype(BF16)
    n0 = w_in0.shape[1]
    tn0 = _tile(n0, 1536)
    p_lat = _norm_mod_matmul(x2, mod0, lat_row, 0, mix_norm_g[0], w_in0, tm, tn0)
    p_ctx = _norm_mod_matmul(h2, mod0, ctx_row, 0, mix_norm_g[0], w_in0, tml, tn0)

    mla_scale = MLA_QK ** -0.5
    wq = _pad_heads(mla_w_q_b[0], MLA_QK, MLA_QK_PAD).astype(BF16)
    wkv = mla_w_kv_b[0].astype(BF16)
    qg = jnp.pad(mla_q_norm_g[0] * mla_scale, (0, MLA_QK_PAD - MLA_QK)).reshape(1, -1)
    kg = jnp.pad(mla_k_norm_g[0], (0, MLA_QK_PAD - MLA_QK)).reshape(1, -1)
    qag = mla_q_a_norm_g[0].reshape(1, -1)
    kvag = mla_kv_a_norm_g[0].reshape(1, -1)
    mla_col = 2 * CONV_CH // MLA_IN_PAD
    cos_m, sin_m = _rope_tables(s, MLA_ROPE)
    q_l, k_l, v_l = _mla_prep(p_lat, b, s, mla_col, cos_m, sin_m, qag, kvag, wq, wkv, qg, kg)
    cos_i, sin_i = _identity_rope(l)
    q_c, k_c, v_c = _mla_prep(p_ctx, b, l, mla_col, cos_i, sin_i, qag, kvag, wq, wkv, qg, kg)
    att_l = _mla_attn(q_l, [(k_l, v_l), (k_c, v_c)]).reshape(b * s, -1)
    att_c = _mla_attn(q_c, [(k_c, v_c)]).reshape(b * l, -1)
    conv_l = _conv_module(p_lat, b, s, conv_dw_w[0], conv_dw_b[0], conv_ln_g[0], conv_ln_b[0])
    conv_c = _conv_module(p_ctx, b, l, conv_dw_w[0], conv_dw_b[0], conv_ln_g[0], conv_ln_b[0])

    w_out = even_w_out[0].astype(BF16)
    k_att = MLA_HEADS * MLA_V
    x2 = _out_proj_residual(att_l, conv_l, w_out[:k_att], w_out[k_att:], x2, mod0, lat_row, 2, tm)
    h2 = _out_proj_residual(att_c, conv_c, w_out[:k_att], w_out[k_att:], h2, mod0, ctx_row, 2, tml)

    wg = dense_w_gate[0].astype(BF16)
    wu = dense_w_up[0].astype(BF16)
    wd = dense_w_down[0].astype(BF16)
    tf = _tile(wg.shape[1], 512)
    x2 = _dense_ffn(x2, mod0, lat_row, ffn_norm_g[0], wg, wu, wd, tm, tf)
    h2 = _dense_ffn(h2, mod0, ctx_row, ffn_norm_g[0], wg, wu, wd, tml, tf)

    w_in1 = odd_w_in[0].astype(BF16)
    p = _norm_mod_matmul(x2, mod1, lat_row, 0, mix_norm_g[1], w_in1, tm, _tile(w_in1.shape[1], 1536))
    w_kv_c = w_in1[:, ODD_Q:ODD_Q + 2 * ODD_KV]
    pc = _norm_mod_matmul(h2, mod1, ctx_row, 0, mix_norm_g[1], w_kv_c, tml, w_kv_c.shape[1])

    cos_s, sin_s = _rope_tables(s, HEAD_DIM)
    gains = jnp.concatenate([jnp.tile(swa_q_norm_g[0] * HEAD_DIM ** -0.5, WIN_Q_HEADS),
                             jnp.tile(swa_k_norm_g[0], WIN_KV_HEADS)]).reshape(1, -1)
    qk = _swa_prep(p, b, s, WIN_Q_HEADS + WIN_KV_HEADS, 0, cos_s, sin_s, gains)
    kc = _swa_prep(pc, b, l, WIN_KV_HEADS, 0, cos_i, sin_i, gains[:, ODD_Q:])
    att = _swa_attn(qk, p, kc, pc, swa_sink[0], b, s, l).reshape(b * s, -1)
    fcol = (ODD_Q + 2 * ODD_KV) // (FNET_GROUPS * FNET_CH)
    four = _fourier(p, b, s, fcol).reshape(b * s, -1)

    w_out1 = odd_w_out[0].astype(BF16)
    x2 = _out_proj_residual(att, four, w_out1[:ODD_Q], w_out1[ODD_Q:], x2, mod1, lat_row, 2, tm)

    x2 = _moe(x2, s, mod1, lat_row, ffn_norm_g[1], router_w[0],
              expert_w_gate[0].astype(BF16), expert_w_up[0].astype(BF16), expert_w_down[0].astype(BF16))
    return x2.reshape(b, s, d)
```

```python
import functools

import jax
import jax.numpy as jnp
import numpy as np
from jax import lax
from jax.experimental import pallas as pl
from jax.experimental.pallas import tpu as pltpu

F32 = jnp.float32
BF16 = jnp.bfloat16
U32 = jnp.uint32

EPS = 1e-6
ROPE_THETA = 10000.0
GRID_W = 64
NEG_INF = -1e30
LANES = 128

MLA_HEADS = 8
MLA_Q_RANK = 512
MLA_KV_RANK = 256
MLA_NOPE = 128
MLA_ROPE = 64
MLA_V = 128
MLA_IN = MLA_Q_RANK + MLA_KV_RANK + MLA_ROPE
MLA_QK = MLA_NOPE + MLA_ROPE
MLA_QK_PAD = 2 * LANES
MLA_IN_PAD = 1024
CONV_CH = 1024
CONV_WIDTH = 31
CONV_HALO = 16
HEAD_DIM = 128
WIN_Q_HEADS = 12
WIN_KV_HEADS = 4
WIN_GROUP = WIN_Q_HEADS // WIN_KV_HEADS
WINDOW = 128
BLOCK = 128
FNET_GROUPS = 4
FNET_CH = 128
ODD_Q = WIN_Q_HEADS * HEAD_DIM
ODD_KV = WIN_KV_HEADS * HEAD_DIM
N_EXPERTS = 8

VMEM_LIMIT = 56 * 1024 * 1024


def _cparams(*sem):
    return pltpu.CompilerParams(dimension_semantics=sem, vmem_limit_bytes=VMEM_LIMIT)


def _tile(n, pref):
    if n <= pref:
        return n
    t = pref - pref % 8
    while n % t:
        t -= 8
    return t


def _sigmoid(x):
    return 1.0 / (1.0 + jnp.exp(-x))


def _silu(x):
    return x * _sigmoid(x)


def _rms_mod(x, g, shift, scale):
    ms = jnp.mean(x * x, axis=-1, keepdims=True)
    return (x * lax.rsqrt(ms + EPS) * g) * (1.0 + scale) + shift


def _rope_lanes(t, cos, sin, grp):
    lane = lax.broadcasted_iota(jnp.int32, t.shape, 1)
    first = (lane // grp) % 2 == 0
    swapped = jnp.where(first, pltpu.roll(t, LANES - grp, axis=1), pltpu.roll(t, grp, axis=1))
    return t * cos + swapped * sin


def _pack_pair(a, b):
    ai = lax.bitcast_convert_type(a.astype(BF16).astype(F32), U32)
    bi = lax.bitcast_convert_type(b.astype(BF16).astype(F32), U32)
    return (ai >> 16) | bi


def _unpack_pair(w):
    a = lax.bitcast_convert_type(w << 16, F32)
    b = lax.bitcast_convert_type(w & jnp.uint32(0xFFFF0000), F32)
    return a, b


def _modulation_kernel(c_ref, w_ref, b_ref, o_ref):
    a = _silu(c_ref[...]).astype(BF16)
    acc = jnp.dot(a, w_ref[...].astype(BF16), preferred_element_type=F32)
    o_ref[...] = acc + b_ref[...]


def _modulation(cvec, ada_w, ada_b):
    depth, d, n = ada_w.shape
    r = cvec.shape[0]
    tn = _tile(n, 1024)
    return pl.pallas_call(
        _modulation_kernel,
        grid=(depth, n // tn),
        in_specs=[
            pl.BlockSpec((r, d), lambda l, j: (0, 0)),
            pl.BlockSpec((None, d, tn), lambda l, j: (l, 0, j)),
            pl.BlockSpec((None, 1, tn), lambda l, j: (l, 0, j)),
        ],
        out_specs=pl.BlockSpec((None, r, tn), lambda l, j: (l, 0, j)),
        out_shape=jax.ShapeDtypeStruct((depth, r, n), F32),
        compiler_params=_cparams("parallel", "parallel"),
        name="modulation",
    )(cvec, ada_w, ada_b.reshape(depth, 1, n))


def _nmm_kernel(x_ref, sh_ref, sc_ref, g_ref, w_ref, o_ref, u_ref):
    @pl.when(pl.program_id(1) == 0)
    def _():
        u_ref[...] = _rms_mod(x_ref[...], g_ref[...], sh_ref[0], sc_ref[0]).astype(BF16)

    o_ref[...] = jnp.dot(u_ref[...], w_ref[...], preferred_element_type=F32).astype(o_ref.dtype)


def _norm_mod_matmul(x2, mod, mod_row, which, g, w, tm, tn):
    m, d = x2.shape
    n = w.shape[1]
    return pl.pallas_call(
        _nmm_kernel,
        grid=(m // tm, n // tn),
        in_specs=[
            pl.BlockSpec((tm, d), lambda i, j: (i, 0)),
            pl.BlockSpec((1, 1, d), lambda i, j: (mod_row(i, tm) * 6 + which, 0, 0)),
            pl.BlockSpec((1, 1, d), lambda i, j: (mod_row(i, tm) * 6 + which + 1, 0, 0)),
            pl.BlockSpec((1, d), lambda i, j: (0, 0)),
            pl.BlockSpec((d, tn), lambda i, j: (0, j)),
        ],
        out_specs=pl.BlockSpec((tm, tn), lambda i, j: (i, j)),
        out_shape=jax.ShapeDtypeStruct((m, n), BF16),
        scratch_shapes=[pltpu.VMEM((tm, d), BF16)],
        compiler_params=_cparams("parallel", "arbitrary"),
        name="norm_mod_matmul",
    )(x2, mod, mod, g.reshape(1, d), w)


def _mla_prep_kernel(p_ref, cos_ref, sin_ref, qag_ref, kvag_ref, wq_ref, wkv_ref, qg_ref, kg_ref,
                     q_ref, k_ref, v_ref):
    p = p_ref[...].astype(F32)
    qa = p[:, :MLA_Q_RANK]
    kva = p[:, MLA_Q_RANK:MLA_Q_RANK + MLA_KV_RANK]
    kpe = p[:, MLA_Q_RANK + MLA_KV_RANK:MLA_Q_RANK + MLA_KV_RANK + LANES]
    qn = qa * lax.rsqrt(jnp.mean(qa * qa, axis=-1, keepdims=True) + EPS) * qag_ref[...]
    kvn = kva * lax.rsqrt(jnp.mean(kva * kva, axis=-1, keepdims=True) + EPS) * kvag_ref[...]
    q = jnp.dot(qn.astype(BF16), wq_ref[...], preferred_element_type=F32)
    kv = jnp.dot(kvn.astype(BF16), wkv_ref[...], preferred_element_type=F32)
    cos = cos_ref[...]
    sin = sin_ref[...]
    qg = qg_ref[...]
    kg = kg_ref[...]
    kpe_ss = jnp.sum(kpe * kpe, axis=-1, keepdims=True)
    for h in range(MLA_HEADS):
        lo = h * MLA_QK_PAD
        qh = q[:, lo:lo + MLA_QK_PAD]
        rs = lax.rsqrt(jnp.sum(qh * qh, axis=-1, keepdims=True) * (1.0 / MLA_QK) + EPS)
        q_ref[0, h, :, :LANES] = (qh[:, :LANES] * rs * qg[:, :LANES]).astype(BF16)
        tail = _rope_lanes(qh[:, LANES:] * rs * qg[:, LANES:], cos, sin, MLA_ROPE // 4)
        q_ref[0, h, :, LANES:] = tail.astype(BF16)
        kn = kv[:, lo:lo + MLA_NOPE]
        rs = lax.rsqrt((jnp.sum(kn * kn, axis=-1, keepdims=True) + kpe_ss) * (1.0 / MLA_QK) + EPS)
        k_ref[0, h, :, :LANES] = (kn * rs * kg[:, :LANES]).astype(BF16)
        tail = _rope_lanes(kpe * rs * kg[:, LANES:], cos, sin, MLA_ROPE // 4)
        k_ref[0, h, :, LANES:] = tail.astype(BF16)
        v_ref[0, h] = kv[:, lo + MLA_NOPE:lo + MLA_NOPE + MLA_V].astype(BF16)


def _mla_prep(p, b, t, col_block, cos, sin, qag, kvag, wq, wkv, qg, kg):
    tm = _tile(t, 256)
    nt = t // tm
    const = lambda bb, i: (0, 0)
    hs = lambda w: pl.BlockSpec((1, MLA_HEADS, tm, w), lambda bb, i: (bb, 0, i, 0))
    return pl.pallas_call(
        _mla_prep_kernel,
        grid=(b, nt),
        in_specs=[
            pl.BlockSpec((tm, MLA_IN_PAD), lambda bb, i: (bb * nt + i, col_block)),
            pl.BlockSpec((tm, LANES), lambda bb, i: (i, 0)),
            pl.BlockSpec((tm, LANES), lambda bb, i: (i, 0)),
            pl.BlockSpec((1, MLA_Q_RANK), const),
            pl.BlockSpec((1, MLA_KV_RANK), const),
            pl.BlockSpec(wq.shape, const),
            pl.BlockSpec(wkv.shape, const),
            pl.BlockSpec((1, MLA_QK_PAD), const),
            pl.BlockSpec((1, MLA_QK_PAD), const),
        ],
        out_specs=[hs(MLA_QK_PAD), hs(MLA_QK_PAD), hs(MLA_V)],
        out_shape=[
            jax.ShapeDtypeStruct((b, MLA_HEADS, t, MLA_QK_PAD), BF16),
            jax.ShapeDtypeStruct((b, MLA_HEADS, t, MLA_QK_PAD), BF16),
            jax.ShapeDtypeStruct((b, MLA_HEADS, t, MLA_V), BF16),
        ],
        compiler_params=_cparams("parallel", "parallel"),
        name="mla_prep",
    )(p, cos, sin, qag, kvag, wq, wkv, qg, kg)


def _nt_dot(a, b):
    return lax.dot_general(a, b, (((1,), (1,)), ((), ())), preferred_element_type=F32)


MLA_HEADS_PER_STEP = 4


def _mla_attn_kernel(nseg, q_ref, *refs):
    o_ref = refs[2 * nseg]
    for hp in range(MLA_HEADS_PER_STEP):
        q = q_ref[0, hp]
        s = [_nt_dot(q, refs[2 * i][0, hp]) for i in range(nseg)]
        m = s[0].max(axis=-1, keepdims=True)
        for si in s[1:]:
            m = jnp.maximum(m, si.max(axis=-1, keepdims=True))
        den = 0.0
        acc = 0.0
        for i in range(nseg):
            e = jnp.exp(s[i] - m)
            den = den + jnp.sum(e, axis=-1, keepdims=True)
            acc = acc + jnp.dot(e.astype(BF16), refs[2 * i + 1][0, hp], preferred_element_type=F32)
        o_ref[0, :, hp * MLA_V:(hp + 1) * MLA_V] = (acc / den).astype(o_ref.dtype)


def _mla_attn(q, kvs):
    b, h, t, _ = q.shape
    tq = _tile(t, 256)
    hp = MLA_HEADS_PER_STEP
    in_specs = [pl.BlockSpec((1, hp, tq, MLA_QK_PAD), lambda bb, hh, i: (bb, hh, i, 0))]
    args = [q]
    for k, v in kvs:
        n = k.shape[2]
        in_specs.append(pl.BlockSpec((1, hp, n, MLA_QK_PAD), lambda bb, hh, i: (bb, hh, 0, 0)))
        in_specs.append(pl.BlockSpec((1, hp, n, MLA_V), lambda bb, hh, i: (bb, hh, 0, 0)))
        args += [k, v]
    return pl.pallas_call(
        functools.partial(_mla_attn_kernel, len(kvs)),
        grid=(b, h // hp, t // tq),
        in_specs=in_specs,
        out_specs=pl.BlockSpec((1, tq, hp * MLA_V), lambda bb, hh, i: (bb, i, hh)),
        out_shape=jax.ShapeDtypeStruct((b, t, h * MLA_V), BF16),
        compiler_params=_cparams("parallel", "parallel", "parallel"),
        name="mla_attn",
    )(*args)


CONV_ROWS = 64
CONV_WIN = CONV_ROWS + 2 * CONV_HALO


def _conv_kernel(t, p_ref, w_ref, b_ref, g_ref, beta_ref, o_ref, hp_ref, cv_ref):
    zeros = jnp.zeros((CONV_HALO, CONV_CH), F32)
    hp_ref[0:CONV_HALO, :] = zeros
    hp_ref[CONV_HALO + t:CONV_HALO + t + CONV_HALO, :] = zeros

    def glu(i, c):
        r0 = pl.multiple_of(i * CONV_ROWS, CONV_ROWS)
        a = p_ref[pl.ds(r0, CONV_ROWS), :CONV_CH].astype(F32)
        gate = p_ref[pl.ds(r0, CONV_ROWS), CONV_CH:].astype(F32)
        hp_ref[pl.ds(r0 + CONV_HALO, CONV_ROWS), :] = a * _sigmoid(gate)
        return c

    lax.fori_loop(0, t // CONV_ROWS, glu, 0)

    def tile(i, c):
        r0 = pl.multiple_of(i * CONV_ROWS, CONV_ROWS)
        for cc in range(CONV_CH // LANES):
            cs = slice(cc * LANES, (cc + 1) * LANES)
            win = hp_ref[pl.ds(r0, CONV_WIN), cs]
            acc = jnp.zeros((CONV_ROWS, LANES), F32) + b_ref[:, cs]
            for r in range(8):
                rolled = win if r == 0 else pltpu.roll(win, CONV_WIN - r, axis=0)
                for k in range(CONV_WIDTH):
                    off = k + CONV_HALO - CONV_WIDTH // 2
                    if off % 8 == r:
                        acc = acc + rolled[off - r:off - r + CONV_ROWS] * w_ref[k:k + 1, cs]
            cv_ref[:, cs] = acc
        h = cv_ref[...]
        mu = jnp.mean(h, axis=-1, keepdims=True)
        hc = h - mu
        y = hc * lax.rsqrt(jnp.mean(hc * hc, axis=-1, keepdims=True) + EPS) * g_ref[...] + beta_ref[...]
        o_ref[pl.ds(r0, CONV_ROWS), :] = _silu(y).astype(o_ref.dtype)
        return c

    lax.fori_loop(0, t // CONV_ROWS, tile, 0)


def _conv_module(p, b, t, dw_w, dw_b, ln_g, ln_b):
    const = lambda bb: (0, 0)
    return pl.pallas_call(
        functools.partial(_conv_kernel, t),
        grid=(b,),
        in_specs=[
            pl.BlockSpec((t, 2 * CONV_CH), lambda bb: (bb, 0)),
            pl.BlockSpec((CONV_WIDTH, CONV_CH), const),
            pl.BlockSpec((1, CONV_CH), const),
            pl.BlockSpec((1, CONV_CH), const),
            pl.BlockSpec((1, CONV_CH), const),
        ],
        out_specs=pl.BlockSpec((t, CONV_CH), lambda bb: (bb, 0)),
        out_shape=jax.ShapeDtypeStruct((b * t, CONV_CH), BF16),
        scratch_shapes=[pltpu.VMEM((t + 2 * CONV_HALO, CONV_CH), F32), pltpu.VMEM((CONV_ROWS, CONV_CH), F32)],
        compiler_params=_cparams("parallel"),
        name="conv_module",
    )(p, dw_w, dw_b.reshape(1, -1), ln_g.reshape(1, -1), ln_b.reshape(1, -1))


def _oproj_kernel(a1_ref, a2_ref, w1_ref, w2_ref, x_ref, gate_ref, o_ref):
    acc = jnp.dot(a1_ref[...], w1_ref[...], preferred_element_type=F32)
    acc = acc + jnp.dot(a2_ref[...], w2_ref[...], preferred_element_type=F32)
    o_ref[...] = x_ref[...] + gate_ref[0] * acc


def _out_proj_residual(a1, a2, w1, w2, x2, mod, mod_row, which, tm):
    m, d = x2.shape
    k1, k2 = w1.shape[0], w2.shape[0]
    const = lambda i: (0, 0)
    return pl.pallas_call(
        _oproj_kernel,
        grid=(m // tm,),
        in_specs=[
            pl.BlockSpec((tm, k1), lambda i: (i, 0)),
            pl.BlockSpec((tm, k2), lambda i: (i, 0)),
            pl.BlockSpec((k1, d), const),
            pl.BlockSpec((k2, d), const),
            pl.BlockSpec((tm, d), lambda i: (i, 0)),
            pl.BlockSpec((1, 1, d), lambda i: (mod_row(i, tm) * 6 + which, 0, 0)),
        ],
        out_specs=pl.BlockSpec((tm, d), lambda i: (i, 0)),
        out_shape=jax.ShapeDtypeStruct((m, d), F32),
        compiler_params=_cparams("parallel"),
        name="out_proj_residual",
    )(a1, a2, w1, w2, x2, mod)


def _ffn_kernel(x_ref, sh_ref, sc_ref, gate_ref, g_ref, wg_ref, wu_ref, wd_ref, o_ref, u_ref, acc_ref):
    f = pl.program_id(1)

    @pl.when(f == 0)
    def _():
        u_ref[...] = _rms_mod(x_ref[...], g_ref[...], sh_ref[0], sc_ref[0]).astype(BF16)
        acc_ref[...] = jnp.zeros_like(acc_ref)

    u = u_ref[...]
    hg = jnp.dot(u, wg_ref[...], preferred_element_type=F32)
    hu = jnp.dot(u, wu_ref[...], preferred_element_type=F32)
    hid = (_silu(hg) * hu).astype(BF16)
    acc_ref[...] += jnp.dot(hid, wd_ref[...], preferred_element_type=F32)

    @pl.when(f == pl.num_programs(1) - 1)
    def _():
        o_ref[...] = x_ref[...] + gate_ref[0] * acc_ref[...]


def _dense_ffn(x2, mod, mod_row, g, wg, wu, wd, tm, tf):
    m, d = x2.shape
    ff = wg.shape[1]
    row = lambda w: pl.BlockSpec((1, 1, d), lambda i, f: (mod_row(i, tm) * 6 + w, 0, 0))
    return pl.pallas_call(
        _ffn_kernel,
        grid=(m // tm, ff // tf),
        in_specs=[
            pl.BlockSpec((tm, d), lambda i, f: (i, 0)),
            row(3), row(4), row(5),
            pl.BlockSpec((1, d), lambda i, f: (0, 0)),
            pl.BlockSpec((d, tf), lambda i, f: (0, f)),
            pl.BlockSpec((d, tf), lambda i, f: (0, f)),
            pl.BlockSpec((tf, d), lambda i, f: (f, 0)),
        ],
        out_specs=pl.BlockSpec((tm, d), lambda i, f: (i, 0)),
        out_shape=jax.ShapeDtypeStruct((m, d), F32),
        scratch_shapes=[pltpu.VMEM((tm, d), BF16), pltpu.VMEM((tm, d), F32)],
        compiler_params=_cparams("parallel", "arbitrary"),
        name="dense_ffn",
    )(x2, mod, mod, mod, g.reshape(1, d), wg, wu, wd)


def _swa_prep_kernel(nheads, p_ref, cos_ref, sin_ref, g_ref, o_ref):
    cos = cos_ref[...]
    sin = sin_ref[...]
    for h in range(nheads):
        cs = slice(h * HEAD_DIM, (h + 1) * HEAD_DIM)
        t = p_ref[:, cs].astype(F32)
        t = t * lax.rsqrt(jnp.mean(t * t, axis=-1, keepdims=True) + EPS) * g_ref[:, cs]
        o_ref[:, cs] = _rope_lanes(t, cos, sin, HEAD_DIM // 4).astype(BF16)


def _swa_prep(p, b, t, nheads, col_block, cos, sin, gains):
    tm = _tile(t, 256)
    nt = t // tm
    w = nheads * HEAD_DIM
    return pl.pallas_call(
        functools.partial(_swa_prep_kernel, nheads),
        grid=(b, nt),
        in_specs=[
            pl.BlockSpec((tm, w), lambda bb, i: (bb * nt + i, col_block)),
            pl.BlockSpec((tm, LANES), lambda bb, i: (i, 0)),
            pl.BlockSpec((tm, LANES), lambda bb, i: (i, 0)),
            pl.BlockSpec((1, w), lambda bb, i: (0, 0)),
        ],
        out_specs=pl.BlockSpec((tm, w), lambda bb, i: (bb * nt + i, 0)),
        out_shape=jax.ShapeDtypeStruct((b * t, w), BF16),
        compiler_params=_cparams("parallel", "parallel"),
        name="swa_prep",
    )(p, cos, sin, gains)


def _swa_attn_kernel(s_len, sink_ref, q_ref, k_ref, v_ref, kc_ref, vc_ref, o_ref):
    blk = pl.program_id(1)
    span = BLOCK + 2 * WINDOW
    start = blk * BLOCK
    ws = pl.multiple_of(jnp.clip(start - WINDOW, 0, s_len - span), BLOCK)
    rows = WIN_GROUP * BLOCK
    row = lax.broadcasted_iota(jnp.int32, (rows, span), 0)
    col = lax.broadcasted_iota(jnp.int32, (rows, span), 1)
    in_window = jnp.abs((start + row % BLOCK) - (ws + col)) <= WINDOW
    rcol = lax.broadcasted_iota(jnp.int32, (rows, 1), 0) // BLOCK
    for n in range(WIN_KV_HEADS):
        hs = slice(n * HEAD_DIM, (n + 1) * HEAD_DIM)
        kw = k_ref[0, pl.ds(ws, span), hs]
        vw = v_ref[0, pl.ds(ws, span), hs]
        q0 = n * WIN_GROUP * HEAD_DIM
        q = jnp.concatenate(
            [q_ref[0, :, q0 + g * HEAD_DIM:q0 + (g + 1) * HEAD_DIM] for g in range(WIN_GROUP)], axis=0)
        s_w = jnp.where(in_window, _nt_dot(q, kw), NEG_INF)
        s_c = _nt_dot(q, kc_ref[0, :, hs])
        sink = jnp.zeros((rows, 1), F32)
        for g in range(WIN_GROUP):
            sink = jnp.where(rcol == g, sink_ref[n * WIN_GROUP + g], sink)
        m = jnp.maximum(jnp.maximum(s_w.max(axis=-1, keepdims=True), s_c.max(axis=-1, keepdims=True)), sink)
        e_w = jnp.exp(s_w - m)
        e_c = jnp.exp(s_c - m)
        den = jnp.sum(e_w, axis=-1, keepdims=True) + jnp.sum(e_c, axis=-1, keepdims=True) + jnp.exp(sink - m)
        acc = jnp.dot(e_w.astype(BF16), vw, preferred_element_type=F32)
        acc = acc + jnp.dot(e_c.astype(BF16), vc_ref[0, :, hs], preferred_element_type=F32)
        out = acc / den
        for g in range(WIN_GROUP):
            o_ref[0, :, q0 + g * HEAD_DIM:q0 + (g + 1) * HEAD_DIM] = (
                out[g * BLOCK:(g + 1) * BLOCK].astype(o_ref.dtype))


def _swa_attn(qk, p, kc, pc, sink, b, s_len, ctx_len):
    qk3 = qk.reshape(b, s_len, -1)
    p3 = p.reshape(b, s_len, -1)
    kc3 = kc.reshape(b, ctx_len, -1)
    pc3 = pc.reshape(b, ctx_len, -1)
    return pl.pallas_call(
        functools.partial(_swa_attn_kernel, s_len),
        grid=(b, s_len // BLOCK),
        in_specs=[
            pl.BlockSpec(memory_space=pltpu.SMEM),
            pl.BlockSpec((1, BLOCK, ODD_Q), lambda bb, i: (bb, i, 0)),
            pl.BlockSpec((1, s_len, ODD_KV), lambda bb, i: (bb, 0, ODD_Q // ODD_KV)),
            pl.BlockSpec((1, s_len, ODD_KV), lambda bb, i: (bb, 0, (ODD_Q + ODD_KV) // ODD_KV)),
            pl.BlockSpec((1, ctx_len, ODD_KV), lambda bb, i: (bb, 0, 0)),
            pl.BlockSpec((1, ctx_len, ODD_KV), lambda bb, i: (bb, 0, 1)),
        ],
        out_specs=pl.BlockSpec((1, BLOCK, ODD_Q), lambda bb, i: (bb, i, 0)),
        out_shape=jax.ShapeDtypeStruct((b, s_len, ODD_Q), BF16),
        compiler_params=_cparams("parallel", "parallel"),
        name="swa_attn",
    )(sink, qk3, qk3, p3, kc3, pc3)


def _fourier_kernel(scale, f_ref, cs_ref, ct_ref, st_ref, o_ref, xc_ref, xs_ref):
    @pl.when(pl.program_id(1) == 0)
    def _():
        for g in range(FNET_GROUPS):
            cs = slice(g * FNET_CH, (g + 1) * FNET_CH)
            r = jnp.dot(f_ref[0, :, cs], cs_ref[...], preferred_element_type=F32)
            xc_ref[:, cs] = r[:, :FNET_CH].astype(BF16)
            xs_ref[:, cs] = r[:, FNET_CH:].astype(BF16)

    y = jnp.dot(ct_ref[...], xc_ref[...], preferred_element_type=F32)
    y = y - jnp.dot(st_ref[...], xs_ref[...], preferred_element_type=F32)
    o_ref[0] = (y * scale).astype(o_ref.dtype)


def _dft_tables(n):
    k = jnp.arange(n, dtype=jnp.int32)
    ang = ((k[:, None] * k[None, :]) % n).astype(F32) * (2.0 * np.pi / n)
    return jnp.cos(ang), jnp.sin(ang)


def _fourier(p, b, t, col_block):
    w = FNET_GROUPS * FNET_CH
    p3 = p.reshape(b, t, -1)
    cc, sc = _dft_tables(FNET_CH)
    ct, st = _dft_tables(t)
    cs = jnp.concatenate([cc, sc], axis=1).astype(BF16)
    tk = _tile(t, 512)
    return pl.pallas_call(
        functools.partial(_fourier_kernel, float((t * FNET_CH) ** -0.5)),
        grid=(b, t // tk),
        in_specs=[
            pl.BlockSpec((1, t, w), lambda bb, i: (bb, 0, col_block)),
            pl.BlockSpec((FNET_CH, 2 * FNET_CH), lambda bb, i: (0, 0)),
            pl.BlockSpec((tk, t), lambda bb, i: (i, 0)),
            pl.BlockSpec((tk, t), lambda bb, i: (i, 0)),
        ],
        out_specs=pl.BlockSpec((1, tk, w), lambda bb, i: (bb, i, 0)),
        out_shape=jax.ShapeDtypeStruct((b, t, w), BF16),
        scratch_shapes=[pltpu.VMEM((t, w), BF16), pltpu.VMEM((t, w), BF16)],
        compiler_params=_cparams("parallel", "arbitrary"),
        name="fourier",
    )(p3, cs, ct.astype(BF16), st.astype(BF16))


def _router_kernel(x_ref, sh_ref, sc_ref, g_ref, wh_ref, wl_ref, fin_ref, info_ref, cnt_ref, run_ref):
    @pl.when(pl.program_id(0) == 0)
    def _():
        run_ref[...] = jnp.zeros_like(run_ref)

    u = _rms_mod(x_ref[...], g_ref[...], sh_ref[0], sc_ref[0])
    d = u.shape[1]
    fin_ref[...] = _pack_pair(u[:, :d // 2], u[:, d // 2:])
    u_hi = u.astype(BF16)
    u_lo = (u - u_hi.astype(F32)).astype(BF16)
    logits = (jnp.dot(u_hi, wh_ref[...], preferred_element_type=F32)
              + (jnp.dot(u_lo, wh_ref[...], preferred_element_type=F32)
                 + jnp.dot(u_hi, wl_ref[...], preferred_element_type=F32)))
    tr = logits.shape[0]
    lane = lax.broadcasted_iota(jnp.int32, (tr, LANES), 1).astype(F32)
    logits = jnp.where(lane < N_EXPERTS, logits, -jnp.inf)
    m1 = logits.max(axis=-1, keepdims=True)
    i1 = jnp.where(logits == m1, lane, float(LANES)).min(axis=-1, keepdims=True)
    rest = jnp.where(lane == i1, -jnp.inf, logits)
    m2 = rest.max(axis=-1, keepdims=True)
    i2 = jnp.where(rest == m2, lane, float(LANES)).min(axis=-1, keepdims=True)
    e21 = jnp.exp(m2 - m1)
    g1 = 1.0 / (1.0 + e21)
    g2 = e21 / (1.0 + e21)
    oh1 = lane == i1
    oh2 = lane == i2
    oh = (oh1 | oh2).astype(F32)
    r = lax.broadcasted_iota(jnp.int32, (tr, tr), 0)
    c = lax.broadcasted_iota(jnp.int32, (tr, tr), 1)
    before = (r > c).astype(BF16)
    prefix = jnp.dot(before, oh.astype(BF16), preferred_element_type=F32) + run_ref[...]
    r1 = jnp.sum(jnp.where(oh1, prefix, 0.0), axis=-1, keepdims=True)
    r2 = jnp.sum(jnp.where(oh2, prefix, 0.0), axis=-1, keepdims=True)
    run = run_ref[...] + jnp.sum(oh, axis=0, keepdims=True)
    run_ref[...] = run
    cnt_ref[...] = run
    info = jnp.zeros((tr, LANES), F32)
    for j, val in enumerate((i1, i2, r1, r2, g1, g2)):
        info = jnp.where(lane == j, val, info)
    info_ref[...] = info[:, :8]


def _router(x2, mod, mod_row, g, wr_hi, wr_lo, tr):
    m, d = x2.shape
    row = lambda w: pl.BlockSpec((1, 1, d), lambda i: (mod_row(i, tr) * 6 + w, 0, 0))
    const = lambda i: (0, 0)
    return pl.pallas_call(
        _router_kernel,
        grid=(m // tr,),
        in_specs=[
            pl.BlockSpec((tr, d), lambda i: (i, 0)),
            row(3), row(4),
            pl.BlockSpec((1, d), const),
            pl.BlockSpec((d, LANES), const),
            pl.BlockSpec((d, LANES), const),
        ],
        out_specs=[
            pl.BlockSpec((tr, d // 2), lambda i: (i, 0)),
            pl.BlockSpec((tr, 8), lambda i: (i, 0)),
            pl.BlockSpec((1, LANES), const),
        ],
        out_shape=[
            jax.ShapeDtypeStruct((m, d // 2), U32),
            jax.ShapeDtypeStruct((m, 8), F32),
            jax.ShapeDtypeStruct((1, LANES), F32),
        ],
        scratch_shapes=[pltpu.VMEM((1, LANES), F32)],
        compiler_params=_cparams("arbitrary"),
        name="moe_router",
    )(x2, mod, mod, g.reshape(1, d), wr_hi, wr_lo)


def _row_copy(src, src_row, dst, dst_row, sem):
    return pltpu.make_async_copy(src.at[pl.ds(src_row, 1)], dst.at[pl.ds(dst_row, 1)], sem)


def _dispatch_kernel(tg, pos_ref, fin_ref, init_ref, xs_ref, sem):
    del init_ref

    def issue(t, c):
        _row_copy(fin_ref, t, xs_ref, pos_ref[0, 2 * t], sem).start()
        _row_copy(fin_ref, t, xs_ref, pos_ref[0, 2 * t + 1], sem).start()
        return c

    lax.fori_loop(0, tg, issue, 0, unroll=8)
    for _ in range(2):
        pltpu.make_async_copy(fin_ref, xs_ref.at[pl.ds(0, tg)], sem).wait()


def _dispatch(fin, pos, cap, tg):
    m, w = fin.shape
    return pl.pallas_call(
        functools.partial(_dispatch_kernel, tg),
        grid=(m // tg,),
        in_specs=[
            pl.BlockSpec((None, 1, 2 * tg), lambda i: (i, 0, 0), memory_space=pltpu.SMEM),
            pl.BlockSpec((tg, w), lambda i: (i, 0)),
            pl.BlockSpec(memory_space=pl.ANY),
        ],
        out_specs=pl.BlockSpec(memory_space=pl.ANY),
        out_shape=jax.ShapeDtypeStruct((cap, w), U32),
        scratch_shapes=[pltpu.SemaphoreType.DMA(())],
        input_output_aliases={2: 0},
        compiler_params=_cparams("arbitrary"),
        name="moe_dispatch",
    )(pos.reshape(m // tg, 1, 2 * tg), fin, jnp.zeros((cap, w), U32))


def _moe_ffn_kernel(sub, te_ref, valid_ref, xs_ref, wg_ref, wu_ref, wd_ref, ys_ref, u_ref, acc_ref):
    i = pl.program_id(0)
    f = pl.program_id(1)
    tm, d = u_ref.shape
    valid = valid_ref[i]

    @pl.when(f == 0)
    def _():
        a, b = _unpack_pair(xs_ref[...])
        u_ref[:, :d // 2] = a.astype(BF16)
        u_ref[:, d // 2:] = b.astype(BF16)
        acc_ref[...] = jnp.zeros_like(acc_ref)

    for sb in range(tm // sub):
        @pl.when(valid > sb * sub)
        def _(sb=sb):
            rows = slice(sb * sub, (sb + 1) * sub)
            u = u_ref[rows]
            hg = jnp.dot(u, wg_ref[...].astype(BF16), preferred_element_type=F32)
            hu = jnp.dot(u, wu_ref[...].astype(BF16), preferred_element_type=F32)
            hid = (_silu(hg) * hu).astype(BF16)
            acc_ref[rows] += jnp.dot(hid, wd_ref[...].astype(BF16), preferred_element_type=F32)

    @pl.when(f == pl.num_programs(1) - 1)
    def _():
        ys_ref[...] = _pack_pair(acc_ref[:, :d // 2], acc_ref[:, d // 2:])


def _moe_ffn(xs, tile_expert, tile_valid, wg, wu, wd, tm, tf, sub):
    cap, w = xs.shape
    d = 2 * w
    ff = wg.shape[3]
    nf = ff // tf

    def f_eff(i, f, valid):
        return jnp.where(valid[i] > 0, f, nf - 1)

    grid_spec = pltpu.PrefetchScalarGridSpec(
        num_scalar_prefetch=2,
        grid=(cap // tm, nf),
        in_specs=[
            pl.BlockSpec((tm, w), lambda i, f, te, valid: (i, 0)),
            pl.BlockSpec((None, None, d, tf), lambda i, f, te, valid: (0, te[i], 0, f_eff(i, f, valid))),
            pl.BlockSpec((None, None, d, tf), lambda i, f, te, valid: (0, te[i], 0, f_eff(i, f, valid))),
            pl.BlockSpec((None, None, tf, d), lambda i, f, te, valid: (0, te[i], f_eff(i, f, valid), 0)),
        ],
        out_specs=pl.BlockSpec((tm, w), lambda i, f, te, valid: (i, 0)),
        scratch_shapes=[pltpu.VMEM((tm, d), BF16), pltpu.VMEM((tm, d), F32)],
    )
    return pl.pallas_call(
        functools.partial(_moe_ffn_kernel, sub),
        grid_spec=grid_spec,
        out_shape=jax.ShapeDtypeStruct((cap, w), U32),
        compiler_params=_cparams("parallel", "arbitrary"),
        name="moe_ffn",
    )(tile_expert, tile_valid, xs, wg, wu, wd)


def _combine_kernel(tc, pos_ref, posn_ref, x_ref, info_ref, gate_ref, ys_ref, o_ref, buf_ref, sem):
    i = pl.program_id(0)
    slot = i % 2

    def issue(p_ref, s):
        def body(t, c):
            _row_copy(ys_ref, p_ref[0, 2 * t], buf_ref.at[s, 0], t, sem.at[s]).start()
            _row_copy(ys_ref, p_ref[0, 2 * t + 1], buf_ref.at[s, 1], t, sem.at[s]).start()
            return c

        lax.fori_loop(0, tc, body, 0, unroll=8)

    @pl.when(i == 0)
    def _():
        issue(pos_ref, 0)

    @pl.when(i + 1 < pl.num_programs(0))
    def _():
        issue(posn_ref, 1 - slot)

    for k in range(2):
        pltpu.make_async_copy(ys_ref.at[pl.ds(0, tc)], buf_ref.at[slot, k], sem.at[slot]).wait()
    w = buf_ref.shape[3]
    g1 = info_ref[:, 4:5]
    g2 = info_ref[:, 5:6]
    a1, b1 = _unpack_pair(buf_ref[slot, 0])
    a2, b2 = _unpack_pair(buf_ref[slot, 1])
    gate = gate_ref[0]
    o_ref[:, :w] = x_ref[:, :w] + gate[:, :w] * (g1 * a1 + g2 * a2)
    o_ref[:, w:] = x_ref[:, w:] + gate[:, w:] * (g1 * b1 + g2 * b2)


def _combine(x2, info, pos, ys, mod, mod_row, tc):
    m, d = x2.shape
    w = ys.shape[1]
    n = m // tc
    pos3 = pos.reshape(n, 1, 2 * tc)
    return pl.pallas_call(
        functools.partial(_combine_kernel, tc),
        grid=(n,),
        in_specs=[
            pl.BlockSpec((None, 1, 2 * tc), lambda i: (i, 0, 0), memory_space=pltpu.SMEM),
            pl.BlockSpec((None, 1, 2 * tc), lambda i: (jnp.minimum(i + 1, n - 1), 0, 0), memory_space=pltpu.SMEM),
            pl.BlockSpec((tc, d), lambda i: (i, 0)),
            pl.BlockSpec((tc, 8), lambda i: (i, 0)),
            pl.BlockSpec((1, 1, d), lambda i: (mod_row(i, tc) * 6 + 5, 0, 0)),
            pl.BlockSpec(memory_space=pl.ANY),
        ],
        out_specs=pl.BlockSpec((tc, d), lambda i: (i, 0)),
        out_shape=jax.ShapeDtypeStruct((m, d), F32),
        scratch_shapes=[pltpu.VMEM((2, 2, tc, w), U32), pltpu.SemaphoreType.DMA((2,))],
        compiler_params=_cparams("arbitrary"),
        name="moe_combine",
    )(pos3, pos3, x2, info, mod, ys)


MOE_SUB = 512


def _moe(x2, s, mod, mod_row, g, router_w, wg, wu, wd):
    m, d = x2.shape
    tm = _tile(m, 1024)
    sub = _tile(tm, MOE_SUB)
    tf = _tile(wg.shape[3], 256)
    tr = _tile(s, 512)
    tg = _tile(s, 256)
    wr = jnp.zeros((d, LANES), F32).at[:, :N_EXPERTS].set(router_w)
    wr_hi = wr.astype(BF16)
    wr_lo = (wr - wr_hi.astype(F32)).astype(BF16)
    fin, info, cnt = _router(x2, mod, mod_row, g, wr_hi, wr_lo, tr)

    counts = cnt[0, :N_EXPERTS].astype(jnp.int32)
    padded = (counts + tm - 1) // tm * tm
    ends = jnp.cumsum(padded)
    starts = ends - padded
    experts = info[:, 0:2].astype(jnp.int32)
    pos = starts[experts] + info[:, 2:4].astype(jnp.int32)
    ntiles = (2 * m) // tm + N_EXPERTS
    cap = ntiles * tm
    tile_start = jnp.arange(ntiles, dtype=jnp.int32) * tm
    tile_expert = jnp.sum(tile_start[:, None] >= ends[None, :], axis=1).astype(jnp.int32)
    active = tile_start < ends[-1]
    last_expert = tile_expert[ends[-1] // tm - 1]
    tile_expert = jnp.where(active, tile_expert, last_expert)
    group_end = (starts + counts)[tile_expert]
    tile_valid = jnp.where(active, jnp.clip(group_end - tile_start, 0, tm), 0).astype(jnp.int32)

    xs = _dispatch(fin, pos, cap, tg)
    ys = _moe_ffn(xs, tile_expert, tile_valid, wg, wu, wd, tm, tf, sub)
    return _combine(x2, info, pos, ys, mod, mod_row, tg)


def _rope_tables(t, rot_dim):
    rows = t // GRID_W
    row = jnp.repeat(jnp.arange(rows, dtype=F32), GRID_W)
    col = jnp.tile(jnp.arange(GRID_W, dtype=F32), rows)
    half = rot_dim // 2
    inv = ROPE_THETA ** (-jnp.arange(0, half, 2, dtype=F32) / half)
    ang_r = row[:, None] * inv[None, :]
    ang_c = col[:, None] * inv[None, :]
    pad = LANES - rot_dim
    cos = jnp.concatenate([jnp.cos(ang_r), jnp.cos(ang_r), jnp.cos(ang_c), jnp.cos(ang_c),
                           jnp.ones((t, pad), F32)], axis=1)
    sin = jnp.concatenate([-jnp.sin(ang_r), jnp.sin(ang_r), -jnp.sin(ang_c), jnp.sin(ang_c),
                           jnp.zeros((t, pad), F32)], axis=1)
    return cos, sin


def _identity_rope(t):
    return jnp.ones((t, LANES), F32), jnp.zeros((t, LANES), F32)


def _pad_heads(w, real, padded):
    k = w.shape[0]
    w = w.reshape(k, MLA_HEADS, real)
    return jnp.pad(w, ((0, 0), (0, 0), (0, padded - real))).reshape(k, MLA_HEADS * padded)


def kernel(x, c, ctx, c_ctx, ada_w, ada_b, mix_norm_g, ffn_norm_g, even_w_in, mla_q_a_norm_g, mla_w_q_b, mla_kv_a_norm_g, mla_w_kv_b, mla_q_norm_g, mla_k_norm_g, conv_dw_w, conv_dw_b, conv_ln_g, conv_ln_b, even_w_out, dense_w_gate, dense_w_up, dense_w_down, odd_w_in, swa_q_norm_g, swa_k_norm_g, swa_sink, odd_w_out, router_w, expert_w_gate, expert_w_up, expert_w_down):
    b, s, d = x.shape
    l = ctx.shape[1]
    assert ada_w.shape[0] == 2, "two layers: an even (MLA | conv, dense) then an odd (SWA | Fourier, MoE) one"

    r = (b + 1 + 7) // 8 * 8
    cvec = jnp.zeros((r, d), F32).at[:b].set(c).at[b].set(c_ctx)
    mod = _modulation(cvec, ada_w, ada_b).reshape(2, r * 6, 1, d)
    mod0, mod1 = mod[0], mod[1]

    tm = _tile(s, 512)
    tml = _tile(b * l, 512)
    lat_row = lambda i, tile: (i * tile) // s
    ctx_row = lambda i, tile: b
    x2 = x.reshape(b * s, d)
    h2 = ctx.reshape(b * l, d)

    w_in = even_w_in[0]
    w_in0 = jnp.concatenate(
        [w_in[:, MLA_IN:], w_in[:, :MLA_IN], jnp.zeros((d, MLA_IN_PAD - MLA_IN), F32)], axis=1).astype(BF16)
    n0 = w_in0.shape[1]
    tn0 = _tile(n0, 1536)
    p_lat = _norm_mod_matmul(x2, mod0, lat_row, 0, mix_norm_g[0], w_in0, tm, tn0)
    p_ctx = _norm_mod_matmul(h2, mod0, ctx_row, 0, mix_norm_g[0], w_in0, tml, tn0)

    mla_scale = MLA_QK ** -0.5
    wq = _pad_heads(mla_w_q_b[0], MLA_QK, MLA_QK_PAD).astype(BF16)
    wkv = mla_w_kv_b[0].astype(BF16)
    qg = jnp.pad(mla_q_norm_g[0] * mla_scale, (0, MLA_QK_PAD - MLA_QK)).reshape(1, -1)
    kg = jnp.pad(mla_k_norm_g[0], (0, MLA_QK_PAD - MLA_QK)).reshape(1, -1)
    qag = mla_q_a_norm_g[0].reshape(1, -1)
    kvag = mla_kv_a_norm_g[0].reshape(1, -1)
    mla_col = 2 * CONV_CH // MLA_IN_PAD
    cos_m, sin_m = _rope_tables(s, MLA_ROPE)
    q_l, k_l, v_l = _mla_prep(p_lat, b, s, mla_col, cos_m, sin_m, qag, kvag, wq, wkv, qg, kg)
    cos_i, sin_i = _identity_rope(l)
    q_c, k_c, v_c = _mla_prep(p_ctx, b, l, mla_col, cos_i, sin_i, qag, kvag, wq, wkv, qg, kg)
    att_l = _mla_attn(q_l, [(k_l, v_l), (k_c, v_c)]).reshape(b * s, -1)
    att_c = _mla_attn(q_c, [(k_c, v_c)]).reshape(b * l, -1)
    conv_l = _conv_module(p_lat, b, s, conv_dw_w[0], conv_dw_b[0], conv_ln_g[0], conv_ln_b[0])
    conv_c = _conv_module(p_ctx, b, l, conv_dw_w[0], conv_dw_b[0], conv_ln_g[0], conv_ln_b[0])

    w_out = even_w_out[0].astype(BF16)
    k_att = MLA_HEADS * MLA_V
    x2 = _out_proj_residual(att_l, conv_l, w_out[:k_att], w_out[k_att:], x2, mod0, lat_row, 2, tm)
    h2 = _out_proj_residual(att_c, conv_c, w_out[:k_att], w_out[k_att:], h2, mod0, ctx_row, 2, tml)

    wg = dense_w_gate[0].astype(BF16)
    wu = dense_w_up[0].astype(BF16)
    wd = dense_w_down[0].astype(BF16)
    tf = _tile(wg.shape[1], 512)
    x2 = _dense_ffn(x2, mod0, lat_row, ffn_norm_g[0], wg, wu, wd, tm, tf)
    h2 = _dense_ffn(h2, mod0, ctx_row, ffn_norm_g[0], wg, wu, wd, tml, tf)

    w_in1 = odd_w_in[0].astype(BF16)
    p = _norm_mod_matmul(x2, mod1, lat_row, 0, mix_norm_g[1], w_in1, tm, _tile(w_in1.shape[1], 1536))
    w_kv_c = w_in1[:, ODD_Q:ODD_Q + 2 * ODD_KV]
    pc = _norm_mod_matmul(h2, mod1, ctx_row, 0, mix_norm_g[1], w_kv_c, tml, w_kv_c.shape[1])

    cos_s, sin_s = _rope_tables(s, HEAD_DIM)
    gains = jnp.concatenate([jnp.tile(swa_q_norm_g[0] * HEAD_DIM ** -0.5, WIN_Q_HEADS),
                             jnp.tile(swa_k_norm_g[0], WIN_KV_HEADS)]).reshape(1, -1)
    qk = _swa_prep(p, b, s, WIN_Q_HEADS + WIN_KV_HEADS, 0, cos_s, sin_s, gains)
    kc = _swa_prep(pc, b, l, WIN_KV_HEADS, 0, cos_i, sin_i, gains[:, ODD_Q:])
    att = _swa_attn(qk, p, kc, pc, swa_sink[0], b, s, l).reshape(b * s, -1)
    fcol = (ODD_Q + 2 * ODD_KV) // (FNET_GROUPS * FNET_CH)
    four = _fourier(p, b, s, fcol).reshape(b * s, -1)

    w_out1 = odd_w_out[0].astype(BF16)
    x2 = _out_proj_residual(att, four, w_out1[:ODD_Q], w_out1[ODD_Q:], x2, mod1, lat_row, 2, tm)

    x2 = _moe(x2, s, mod1, lat_row, ffn_norm_g[1], router_w[0],
              expert_w_gate, expert_w_up, expert_w_down)
    return x2.reshape(b, s, d)
```

```python
import functools

import jax
import jax.numpy as jnp
import numpy as np
from jax import lax
from jax.experimental import pallas as pl
from jax.experimental.pallas import tpu as pltpu

F32 = jnp.float32
BF16 = jnp.bfloat16
U32 = jnp.uint32

EPS = 1e-6
ROPE_THETA = 10000.0
GRID_W = 64
NEG_INF = -1e30
LANES = 128

MLA_HEADS = 8
MLA_Q_RANK = 512
MLA_KV_RANK = 256
MLA_NOPE = 128
MLA_ROPE = 64
MLA_V = 128
MLA_IN = MLA_Q_RANK + MLA_KV_RANK + MLA_ROPE
MLA_QK = MLA_NOPE + MLA_ROPE
MLA_QK_PAD = 2 * LANES
MLA_IN_PAD = 1024
CONV_CH = 1024
CONV_WIDTH = 31
CONV_HALO = 16
HEAD_DIM = 128
WIN_Q_HEADS = 12
WIN_KV_HEADS = 4
WIN_GROUP = WIN_Q_HEADS // WIN_KV_HEADS
WINDOW = 128
BLOCK = 128
FNET_GROUPS = 4
FNET_CH = 128
ODD_Q = WIN_Q_HEADS * HEAD_DIM
ODD_KV = WIN_KV_HEADS * HEAD_DIM
N_EXPERTS = 8

VMEM_LIMIT = 56 * 1024 * 1024


def _cparams(*sem):
    return pltpu.CompilerParams(dimension_semantics=sem, vmem_limit_bytes=VMEM_LIMIT)


def _tile(n, pref):
    if n <= pref:
        return n
    t = pref - pref % 8
    while n % t:
        t -= 8
    return t


def _sigmoid(x):
    return 1.0 / (1.0 + jnp.exp(-x))


def _silu(x):
    return x * _sigmoid(x)


def _rms_mod(x, g, shift, scale):
    ms = jnp.mean(x * x, axis=-1, keepdims=True)
    return (x * lax.rsqrt(ms + EPS) * g) * (1.0 + scale) + shift


NORM_ROWS = 16


def _rms_mod_rows(x_ref, rows, gain, shift):
    x = x_ref[rows, :]
    ms = jnp.mean(x * x, axis=-1, keepdims=True)
    return (x * lax.rsqrt(ms + EPS) * gain + shift).astype(BF16)


def _rms_mod_loop(x_ref, g_ref, sh_ref, sc_ref, dst_ref):
    gain = g_ref[...] * (1.0 + sc_ref[0])
    shift = sh_ref[0]

    def body(c, carry):
        rows = pl.ds(pl.multiple_of(c * NORM_ROWS, NORM_ROWS), NORM_ROWS)
        dst_ref[rows, :] = _rms_mod_rows(x_ref, rows, gain, shift)
        return carry

    lax.fori_loop(0, x_ref.shape[0] // NORM_ROWS, body, 0)


def _pack_pair(a, b):
    ai = lax.bitcast_convert_type(a.astype(BF16).astype(F32), U32)
    bi = lax.bitcast_convert_type(b.astype(BF16).astype(F32), U32)
    return (ai >> 16) | bi


def _unpack_pair(w):
    a = lax.bitcast_convert_type(w << 16, F32)
    b = lax.bitcast_convert_type(w & jnp.uint32(0xFFFF0000), F32)
    return a, b


def _modulation_kernel(c_ref, w_ref, b_ref, o_ref):
    a = _silu(c_ref[...]).astype(BF16)
    acc = jnp.dot(a, w_ref[...].astype(BF16), preferred_element_type=F32)
    o_ref[...] = acc + b_ref[...]


def _modulation(cvec, ada_w, ada_b):
    depth, d, n = ada_w.shape
    r = cvec.shape[0]
    tn = _tile(n, 1024)
    return pl.pallas_call(
        _modulation_kernel,
        grid=(depth, n // tn),
        in_specs=[
            pl.BlockSpec((r, d), lambda l, j: (0, 0)),
            pl.BlockSpec((None, d, tn), lambda l, j: (l, 0, j)),
            pl.BlockSpec((None, 1, tn), lambda l, j: (l, 0, j)),
        ],
        out_specs=pl.BlockSpec((None, r, tn), lambda l, j: (l, 0, j)),
        out_shape=jax.ShapeDtypeStruct((depth, r, n), F32),
        compiler_params=_cparams("parallel", "parallel"),
        name="modulation",
    )(cvec, ada_w, ada_b.reshape(depth, 1, n))


def _nmm_kernel(x_ref, sh_ref, sc_ref, g_ref, w_ref, o_ref, u0_ref, u1_ref):
    i = pl.program_id(0)

    @pl.when((i == 0) & (pl.program_id(1) == 0))
    def _():
        _rms_mod_loop(x_ref, g_ref, sh_ref, sc_ref, u0_ref)

    def step(cur_ref, nxt_ref):
        gain = g_ref[...] * (1.0 + sc_ref[0])
        shift = sh_ref[0]
        for c in range(x_ref.shape[0] // NORM_ROWS):
            rows = slice(c * NORM_ROWS, (c + 1) * NORM_ROWS)
            nxt_ref[rows, :] = _rms_mod_rows(x_ref, rows, gain, shift)
        o_ref[...] = jnp.dot(cur_ref[...], w_ref[...], preferred_element_type=F32).astype(o_ref.dtype)

    @pl.when(i % 2 == 0)
    def _():
        step(u0_ref, u1_ref)

    @pl.when(i % 2 == 1)
    def _():
        step(u1_ref, u0_ref)


def _norm_mod_matmul(x2, mod, mod_row, which, g, w, tm):
    m, d = x2.shape
    n = w.shape[1]
    tn = n // 2
    nt = m // tm
    ahead = lambda i, j: jnp.minimum(i + jnp.minimum(j, 1), nt - 1)
    return pl.pallas_call(
        _nmm_kernel,
        grid=(nt, 2),
        in_specs=[
            pl.BlockSpec((tm, d), lambda i, j: (ahead(i, j), 0)),
            pl.BlockSpec((1, 1, d), lambda i, j: (mod_row(ahead(i, j), tm) * 6 + which, 0, 0)),
            pl.BlockSpec((1, 1, d), lambda i, j: (mod_row(ahead(i, j), tm) * 6 + which + 1, 0, 0)),
            pl.BlockSpec((1, d), lambda i, j: (0, 0)),
            pl.BlockSpec((d, tn), lambda i, j: (0, j)),
        ],
        out_specs=pl.BlockSpec((tm, tn), lambda i, j: (i, j)),
        out_shape=jax.ShapeDtypeStruct((m, n), BF16),
        scratch_shapes=[pltpu.VMEM((tm, d), BF16), pltpu.VMEM((tm, d), BF16)],
        compiler_params=_cparams("arbitrary", "arbitrary"),
        name="norm_mod_matmul",
    )(x2, mod, mod, g.reshape(1, d), w)


def _split_dot(x, m01):
    hi = x.astype(BF16)
    lo = (x - hi.astype(F32)).astype(BF16)
    return jnp.dot(hi, m01, preferred_element_type=F32) + jnp.dot(lo, m01, preferred_element_type=F32)


def _rope_mxu(t, cos, sin, perm):
    return t * cos + _split_dot(t, perm) * sin


def _mla_prep_kernel(p_ref, cos_ref, sin_ref, qag_ref, kvag_ref, wq_ref, wkv_ref, qg_ref, kg_ref,
                     perm_ref, q_ref, k_ref, v_ref):
    p = p_ref[...].astype(F32)
    qa = p[:, :MLA_Q_RANK]
    kva = p[:, MLA_Q_RANK:MLA_Q_RANK + MLA_KV_RANK]
    kpe = p[:, MLA_Q_RANK + MLA_KV_RANK:MLA_Q_RANK + MLA_KV_RANK + LANES]
    qn = qa * lax.rsqrt(jnp.mean(qa * qa, axis=-1, keepdims=True) + EPS) * qag_ref[...]
    kvn = kva * lax.rsqrt(jnp.mean(kva * kva, axis=-1, keepdims=True) + EPS) * kvag_ref[...]
    q = jnp.dot(qn.astype(BF16), wq_ref[...], preferred_element_type=F32)
    kv = jnp.dot(kvn.astype(BF16), wkv_ref[...], preferred_element_type=F32)
    cos = cos_ref[...]
    sin = sin_ref[...]
    qg = qg_ref[...]
    kg = kg_ref[...]
    perm = perm_ref[...]
    kpe_ss = jnp.sum(kpe * kpe, axis=-1, keepdims=True)
    for h in range(MLA_HEADS):
        lo = h * MLA_QK_PAD
        qh = q[:, lo:lo + MLA_QK_PAD]
        rs = lax.rsqrt(jnp.sum(qh * qh, axis=-1, keepdims=True) * (1.0 / MLA_QK) + EPS)
        q_ref[0, h, :, :LANES] = (qh[:, :LANES] * rs * qg[:, :LANES]).astype(BF16)
        tail = _rope_mxu(qh[:, LANES:] * rs * qg[:, LANES:], cos, sin, perm)
        q_ref[0, h, :, LANES:] = tail.astype(BF16)
        kn = kv[:, lo:lo + MLA_NOPE]
        rs = lax.rsqrt((jnp.sum(kn * kn, axis=-1, keepdims=True) + kpe_ss) * (1.0 / MLA_QK) + EPS)
        k_ref[0, h, :, :LANES] = (kn * rs * kg[:, :LANES]).astype(BF16)
        tail = _rope_mxu(kpe * rs * kg[:, LANES:], cos, sin, perm)
        k_ref[0, h, :, LANES:] = tail.astype(BF16)
        v_ref[0, h] = kv[:, lo + MLA_NOPE:lo + MLA_NOPE + MLA_V].astype(BF16)


def _rope_perm(grp):
    lane = np.arange(LANES)
    src = np.where((lane // grp) % 2 == 0, lane + grp, lane - grp)
    perm = np.zeros((LANES, LANES), np.float32)
    perm[src, lane] = 1.0
    return jnp.asarray(perm, BF16)


def _mla_prep(p, b, t, col_block, cos, sin, qag, kvag, wq, wkv, qg, kg):
    tm = _tile(t, 256)
    nt = t // tm
    const = lambda bb, i: (0, 0)
    perm = _rope_perm(MLA_ROPE // 4)
    hs = lambda w: pl.BlockSpec((1, MLA_HEADS, tm, w), lambda bb, i: (bb, 0, i, 0))
    return pl.pallas_call(
        _mla_prep_kernel,
        grid=(b, nt),
        in_specs=[
            pl.BlockSpec((tm, MLA_IN_PAD), lambda bb, i: (bb * nt + i, col_block)),
            pl.BlockSpec((tm, LANES), lambda bb, i: (i, 0)),
            pl.BlockSpec((tm, LANES), lambda bb, i: (i, 0)),
            pl.BlockSpec((1, MLA_Q_RANK), const),
            pl.BlockSpec((1, MLA_KV_RANK), const),
            pl.BlockSpec(wq.shape, const),
            pl.BlockSpec(wkv.shape, const),
            pl.BlockSpec((1, MLA_QK_PAD), const),
            pl.BlockSpec((1, MLA_QK_PAD), const),
            pl.BlockSpec((LANES, LANES), const),
        ],
        out_specs=[hs(MLA_QK_PAD), hs(MLA_QK_PAD), hs(MLA_V)],
        out_shape=[
            jax.ShapeDtypeStruct((b, MLA_HEADS, t, MLA_QK_PAD), BF16),
            jax.ShapeDtypeStruct((b, MLA_HEADS, t, MLA_QK_PAD), BF16),
            jax.ShapeDtypeStruct((b, MLA_HEADS, t, MLA_V), BF16),
        ],
        compiler_params=_cparams("parallel", "parallel"),
        name="mla_prep",
    )(p, cos, sin, qag, kvag, wq, wkv, qg, kg, perm)


def _nt_dot(a, b):
    return lax.dot_general(a, b, (((1,), (1,)), ((), ())), preferred_element_type=F32)


MLA_HEADS_PER_STEP = 4


def _mla_attn_kernel(nseg, q_ref, *refs):
    o_ref = refs[2 * nseg]
    for hp in range(MLA_HEADS_PER_STEP):
        q = q_ref[0, hp]
        s = [_nt_dot(q, refs[2 * i][0, hp]) for i in range(nseg)]
        m = s[0].max(axis=-1, keepdims=True)
        for si in s[1:]:
            m = jnp.maximum(m, si.max(axis=-1, keepdims=True))
        den = 0.0
        acc = 0.0
        for i in range(nseg):
            e = jnp.exp(s[i] - m)
            den = den + jnp.sum(e, axis=-1, keepdims=True)
            acc = acc + jnp.dot(e.astype(BF16), refs[2 * i + 1][0, hp], preferred_element_type=F32)
        o_ref[0, :, hp * MLA_V:(hp + 1) * MLA_V] = (acc / den).astype(o_ref.dtype)


def _mla_attn(q, kvs):
    b, h, t, _ = q.shape
    tq = _tile(t, 256)
    hp = MLA_HEADS_PER_STEP
    in_specs = [pl.BlockSpec((1, hp, tq, MLA_QK_PAD), lambda bb, hh, i: (bb, hh, i, 0))]
    args = [q]
    for k, v in kvs:
        n = k.shape[2]
        in_specs.append(pl.BlockSpec((1, hp, n, MLA_QK_PAD), lambda bb, hh, i: (bb, hh, 0, 0)))
        in_specs.append(pl.BlockSpec((1, hp, n, MLA_V), lambda bb, hh, i: (bb, hh, 0, 0)))
        args += [k, v]
    return pl.pallas_call(
        functools.partial(_mla_attn_kernel, len(kvs)),
        grid=(b, h // hp, t // tq),
        in_specs=in_specs,
        out_specs=pl.BlockSpec((1, tq, hp * MLA_V), lambda bb, hh, i: (bb, i, hh)),
        out_shape=jax.ShapeDtypeStruct((b, t, h * MLA_V), BF16),
        compiler_params=_cparams("parallel", "parallel", "parallel"),
        name="mla_attn",
    )(*args)


CONV_ROWS = 64
CONV_WIN = CONV_ROWS + 2 * CONV_HALO


def _conv_kernel(t, p_ref, w_ref, b_ref, g_ref, beta_ref, o_ref, hp_ref, cv_ref):
    zeros = jnp.zeros((CONV_HALO, CONV_CH), F32)
    hp_ref[0:CONV_HALO, :] = zeros
    hp_ref[CONV_HALO + t:CONV_HALO + t + CONV_HALO, :] = zeros

    def glu(i, c):
        r0 = pl.multiple_of(i * CONV_ROWS, CONV_ROWS)
        a = p_ref[pl.ds(r0, CONV_ROWS), :CONV_CH].astype(F32)
        gate = p_ref[pl.ds(r0, CONV_ROWS), CONV_CH:].astype(F32)
        hp_ref[pl.ds(r0 + CONV_HALO, CONV_ROWS), :] = a * _sigmoid(gate)
        return c

    lax.fori_loop(0, t // CONV_ROWS, glu, 0)

    def tile(i, c):
        r0 = pl.multiple_of(i * CONV_ROWS, CONV_ROWS)
        for cc in range(CONV_CH // LANES):
            cs = slice(cc * LANES, (cc + 1) * LANES)
            win = hp_ref[pl.ds(r0, CONV_WIN), cs]
            acc = jnp.zeros((CONV_ROWS, LANES), F32) + b_ref[:, cs]
            for r in range(8):
                rolled = win if r == 0 else pltpu.roll(win, CONV_WIN - r, axis=0)
                for k in range(CONV_WIDTH):
                    off = k + CONV_HALO - CONV_WIDTH // 2
                    if off % 8 == r:
                        acc = acc + rolled[off - r:off - r + CONV_ROWS] * w_ref[k:k + 1, cs]
            cv_ref[:, cs] = acc
        h = cv_ref[...]
        mu = jnp.mean(h, axis=-1, keepdims=True)
        hc = h - mu
        y = hc * lax.rsqrt(jnp.mean(hc * hc, axis=-1, keepdims=True) + EPS) * g_ref[...] + beta_ref[...]
        o_ref[pl.ds(r0, CONV_ROWS), :] = _silu(y).astype(o_ref.dtype)
        return c

    lax.fori_loop(0, t // CONV_ROWS, tile, 0)


def _conv_module(p, b, t, dw_w, dw_b, ln_g, ln_b):
    const = lambda bb: (0, 0)
    return pl.pallas_call(
        functools.partial(_conv_kernel, t),
        grid=(b,),
        in_specs=[
            pl.BlockSpec((t, 2 * CONV_CH), lambda bb: (bb, 0)),
            pl.BlockSpec((CONV_WIDTH, CONV_CH), const),
            pl.BlockSpec((1, CONV_CH), const),
            pl.BlockSpec((1, CONV_CH), const),
            pl.BlockSpec((1, CONV_CH), const),
        ],
        out_specs=pl.BlockSpec((t, CONV_CH), lambda bb: (bb, 0)),
        out_shape=jax.ShapeDtypeStruct((b * t, CONV_CH), BF16),
        scratch_shapes=[pltpu.VMEM((t + 2 * CONV_HALO, CONV_CH), F32), pltpu.VMEM((CONV_ROWS, CONV_CH), F32)],
        compiler_params=_cparams("parallel"),
        name="conv_module",
    )(p, dw_w, dw_b.reshape(1, -1), ln_g.reshape(1, -1), ln_b.reshape(1, -1))


def _oproj_kernel(a1_ref, a2_ref, w1_ref, w2_ref, x_ref, gate_ref, o_ref):
    acc = jnp.dot(a1_ref[...], w1_ref[...], preferred_element_type=F32)
    acc = acc + jnp.dot(a2_ref[...], w2_ref[...], preferred_element_type=F32)
    o_ref[...] = x_ref[...] + gate_ref[0] * acc


def _out_proj_residual(a1, a2, w1, w2, x2, mod, mod_row, which, tm):
    m, d = x2.shape
    k1, k2 = w1.shape[0], w2.shape[0]
    const = lambda i: (0, 0)
    return pl.pallas_call(
        _oproj_kernel,
        grid=(m // tm,),
        in_specs=[
            pl.BlockSpec((tm, k1), lambda i: (i, 0)),
            pl.BlockSpec((tm, k2), lambda i: (i, 0)),
            pl.BlockSpec((k1, d), const),
            pl.BlockSpec((k2, d), const),
            pl.BlockSpec((tm, d), lambda i: (i, 0)),
            pl.BlockSpec((1, 1, d), lambda i: (mod_row(i, tm) * 6 + which, 0, 0)),
        ],
        out_specs=pl.BlockSpec((tm, d), lambda i: (i, 0)),
        out_shape=jax.ShapeDtypeStruct((m, d), F32),
        compiler_params=_cparams("parallel"),
        name="out_proj_residual",
    )(a1, a2, w1, w2, x2, mod)


def _ffn_kernel(x_ref, sh_ref, sc_ref, gate_ref, g_ref, wg_ref, wu_ref, wd_ref, o_ref, u_ref):
    f = pl.program_id(1)

    @pl.when(f == 0)
    def _():
        _rms_mod_loop(x_ref, g_ref, sh_ref, sc_ref, u_ref)
        o_ref[...] = jnp.zeros_like(o_ref)

    tm = u_ref.shape[0]
    halves = [slice(0, tm // 2), slice(tm // 2, tm)]
    hid = []
    for rows in halves:
        u = u_ref[rows]
        hg = jnp.dot(u, wg_ref[...], preferred_element_type=F32)
        hu = jnp.dot(u, wu_ref[...], preferred_element_type=F32)
        hid.append((_silu(hg) * hu).astype(BF16))
    for rows, h in zip(halves, hid):
        o_ref[rows] += jnp.dot(h, wd_ref[...], preferred_element_type=F32)

    @pl.when(f == pl.num_programs(1) - 1)
    def _():
        o_ref[...] = x_ref[...] + gate_ref[0] * o_ref[...]


def _dense_ffn(x2, mod, mod_row, g, wg, wu, wd, tm, tf):
    m, d = x2.shape
    ff = wg.shape[1]
    row = lambda w: pl.BlockSpec((1, 1, d), lambda i, f: (mod_row(i, tm) * 6 + w, 0, 0))
    return pl.pallas_call(
        _ffn_kernel,
        grid=(m // tm, ff // tf),
        in_specs=[
            pl.BlockSpec((tm, d), lambda i, f: (i, 0)),
            row(3), row(4), row(5),
            pl.BlockSpec((1, d), lambda i, f: (0, 0)),
            pl.BlockSpec((d, tf), lambda i, f: (0, f)),
            pl.BlockSpec((d, tf), lambda i, f: (0, f)),
            pl.BlockSpec((tf, d), lambda i, f: (f, 0)),
        ],
        out_specs=pl.BlockSpec((tm, d), lambda i, f: (i, 0)),
        out_shape=jax.ShapeDtypeStruct((m, d), F32),
        scratch_shapes=[pltpu.VMEM((tm, d), BF16)],
        compiler_params=_cparams("parallel", "arbitrary"),
        name="dense_ffn",
    )(x2, mod, mod, mod, g.reshape(1, d), wg, wu, wd)


def _swa_prep_kernel(nheads, p_ref, cos_ref, sin_ref, g_ref, perm_ref, ones_ref, o_ref):
    cos = cos_ref[...]
    sin = sin_ref[...]
    perm = perm_ref[...]
    ones = ones_ref[...]
    for h in range(nheads):
        cs = slice(h * HEAD_DIM, (h + 1) * HEAD_DIM)
        t = p_ref[:, cs].astype(F32)
        ms = _split_dot(t * t, ones) * (1.0 / HEAD_DIM)
        t = t * lax.rsqrt(ms + EPS) * g_ref[:, cs]
        o_ref[:, cs] = _rope_mxu(t, cos, sin, perm).astype(BF16)


def _swa_prep(p, b, t, nheads, col_block, cos, sin, gains):
    tm = _tile(t, 256)
    nt = t // tm
    w = nheads * HEAD_DIM
    return pl.pallas_call(
        functools.partial(_swa_prep_kernel, nheads),
        grid=(b, nt),
        in_specs=[
            pl.BlockSpec((tm, w), lambda bb, i: (bb * nt + i, col_block)),
            pl.BlockSpec((tm, LANES), lambda bb, i: (i, 0)),
            pl.BlockSpec((tm, LANES), lambda bb, i: (i, 0)),
            pl.BlockSpec((1, w), lambda bb, i: (0, 0)),
            pl.BlockSpec((LANES, LANES), lambda bb, i: (0, 0)),
            pl.BlockSpec((LANES, LANES), lambda bb, i: (0, 0)),
        ],
        out_specs=pl.BlockSpec((tm, w), lambda bb, i: (bb * nt + i, 0)),
        out_shape=jax.ShapeDtypeStruct((b * t, w), BF16),
        compiler_params=_cparams("parallel", "parallel"),
        name="swa_prep",
    )(p, cos, sin, gains, _rope_perm(HEAD_DIM // 4), jnp.ones((LANES, LANES), BF16))


def _swa_attn_kernel(s_len, sink_ref, q_ref, k_ref, v_ref, kc_ref, vc_ref, o_ref):
    blk = pl.program_id(1)
    span = BLOCK + 2 * WINDOW
    start = blk * BLOCK
    ws = pl.multiple_of(jnp.clip(start - WINDOW, 0, s_len - span), BLOCK)
    rows = WIN_GROUP * BLOCK
    row = lax.broadcasted_iota(jnp.int32, (rows, span), 0)
    col = lax.broadcasted_iota(jnp.int32, (rows, span), 1)
    in_window = jnp.abs((start + row % BLOCK) - (ws + col)) <= WINDOW
    rcol = lax.broadcasted_iota(jnp.int32, (rows, 1), 0) // BLOCK
    for n in range(WIN_KV_HEADS):
        hs = slice(n * HEAD_DIM, (n + 1) * HEAD_DIM)
        kw = k_ref[0, pl.ds(ws, span), hs]
        vw = v_ref[0, pl.ds(ws, span), hs]
        q0 = n * WIN_GROUP * HEAD_DIM
        q = jnp.concatenate(
            [q_ref[0, :, q0 + g * HEAD_DIM:q0 + (g + 1) * HEAD_DIM] for g in range(WIN_GROUP)], axis=0)
        s_w = jnp.where(in_window, _nt_dot(q, kw), NEG_INF)
        s_c = _nt_dot(q, kc_ref[0, :, hs])
        sink = jnp.zeros((rows, 1), F32)
        for g in range(WIN_GROUP):
            sink = jnp.where(rcol == g, sink_ref[n * WIN_GROUP + g], sink)
        m = jnp.maximum(jnp.maximum(s_w.max(axis=-1, keepdims=True), s_c.max(axis=-1, keepdims=True)), sink)
        e_w = jnp.exp(s_w - m)
        e_c = jnp.exp(s_c - m)
        den = jnp.sum(e_w, axis=-1, keepdims=True) + jnp.sum(e_c, axis=-1, keepdims=True) + jnp.exp(sink - m)
        acc = jnp.dot(e_w.astype(BF16), vw, preferred_element_type=F32)
        acc = acc + jnp.dot(e_c.astype(BF16), vc_ref[0, :, hs], preferred_element_type=F32)
        out = acc / den
        for g in range(WIN_GROUP):
            o_ref[0, :, q0 + g * HEAD_DIM:q0 + (g + 1) * HEAD_DIM] = (
                out[g * BLOCK:(g + 1) * BLOCK].astype(o_ref.dtype))


def _swa_attn(qk, p, kc, pc, sink, b, s_len, ctx_len):
    qk3 = qk.reshape(b, s_len, -1)
    p3 = p.reshape(b, s_len, -1)
    kc3 = kc.reshape(b, ctx_len, -1)
    pc3 = pc.reshape(b, ctx_len, -1)
    return pl.pallas_call(
        functools.partial(_swa_attn_kernel, s_len),
        grid=(b, s_len // BLOCK),
        in_specs=[
            pl.BlockSpec(memory_space=pltpu.SMEM),
            pl.BlockSpec((1, BLOCK, ODD_Q), lambda bb, i: (bb, i, 0)),
            pl.BlockSpec((1, s_len, ODD_KV), lambda bb, i: (bb, 0, ODD_Q // ODD_KV)),
            pl.BlockSpec((1, s_len, ODD_KV), lambda bb, i: (bb, 0, (ODD_Q + ODD_KV) // ODD_KV)),
            pl.BlockSpec((1, ctx_len, ODD_KV), lambda bb, i: (bb, 0, 0)),
            pl.BlockSpec((1, ctx_len, ODD_KV), lambda bb, i: (bb, 0, 1)),
        ],
        out_specs=pl.BlockSpec((1, BLOCK, ODD_Q), lambda bb, i: (bb, i, 0)),
        out_shape=jax.ShapeDtypeStruct((b, s_len, ODD_Q), BF16),
        compiler_params=_cparams("parallel", "parallel"),
        name="swa_attn",
    )(sink, qk3, qk3, p3, kc3, pc3)


def _fourier_kernel(scale, f_ref, cs_ref, ct_ref, st_ref, o_ref, xc_ref, xs_ref):
    @pl.when(pl.program_id(1) == 0)
    def _():
        for g in range(FNET_GROUPS):
            cs = slice(g * FNET_CH, (g + 1) * FNET_CH)
            r = jnp.dot(f_ref[0, :, cs], cs_ref[...], preferred_element_type=F32)
            xc_ref[:, cs] = r[:, :FNET_CH].astype(BF16)
            xs_ref[:, cs] = r[:, FNET_CH:].astype(BF16)

    y = jnp.dot(ct_ref[...], xc_ref[...], preferred_element_type=F32)
    y = y - jnp.dot(st_ref[...], xs_ref[...], preferred_element_type=F32)
    o_ref[0] = (y * scale).astype(o_ref.dtype)


def _dft_tables(n):
    k = jnp.arange(n, dtype=jnp.int32)
    ang = ((k[:, None] * k[None, :]) % n).astype(F32) * (2.0 * np.pi / n)
    return jnp.cos(ang), jnp.sin(ang)


def _fourier(p, b, t, col_block):
    w = FNET_GROUPS * FNET_CH
    p3 = p.reshape(b, t, -1)
    cc, sc = _dft_tables(FNET_CH)
    ct, st = _dft_tables(t)
    cs = jnp.concatenate([cc, sc], axis=1).astype(BF16)
    tk = _tile(t, 512)
    return pl.pallas_call(
        functools.partial(_fourier_kernel, float((t * FNET_CH) ** -0.5)),
        grid=(b, t // tk),
        in_specs=[
            pl.BlockSpec((1, t, w), lambda bb, i: (bb, 0, col_block)),
            pl.BlockSpec((FNET_CH, 2 * FNET_CH), lambda bb, i: (0, 0)),
            pl.BlockSpec((tk, t), lambda bb, i: (i, 0)),
            pl.BlockSpec((tk, t), lambda bb, i: (i, 0)),
        ],
        out_specs=pl.BlockSpec((1, tk, w), lambda bb, i: (bb, i, 0)),
        out_shape=jax.ShapeDtypeStruct((b, t, w), BF16),
        scratch_shapes=[pltpu.VMEM((t, w), BF16), pltpu.VMEM((t, w), BF16)],
        compiler_params=_cparams("parallel", "arbitrary"),
        name="fourier",
    )(p3, cs, ct.astype(BF16), st.astype(BF16))


def _router_kernel(x_ref, sh_ref, sc_ref, g_ref, wh_ref, wl_ref, fin_ref, info_ref, cnt_ref, run_ref):
    @pl.when(pl.program_id(0) == 0)
    def _():
        run_ref[...] = jnp.zeros_like(run_ref)

    u = _rms_mod(x_ref[...], g_ref[...], sh_ref[0], sc_ref[0])
    d = u.shape[1]
    fin_ref[...] = _pack_pair(u[:, :d // 2], u[:, d // 2:])
    u_hi = u.astype(BF16)
    u_lo = (u - u_hi.astype(F32)).astype(BF16)
    logits = (jnp.dot(u_hi, wh_ref[...], preferred_element_type=F32)
              + (jnp.dot(u_lo, wh_ref[...], preferred_element_type=F32)
                 + jnp.dot(u_hi, wl_ref[...], preferred_element_type=F32)))
    tr = logits.shape[0]
    lane = lax.broadcasted_iota(jnp.int32, (tr, LANES), 1).astype(F32)
    logits = jnp.where(lane < N_EXPERTS, logits, -jnp.inf)
    m1 = logits.max(axis=-1, keepdims=True)
    i1 = jnp.where(logits == m1, lane, float(LANES)).min(axis=-1, keepdims=True)
    rest = jnp.where(lane == i1, -jnp.inf, logits)
    m2 = rest.max(axis=-1, keepdims=True)
    i2 = jnp.where(rest == m2, lane, float(LANES)).min(axis=-1, keepdims=True)
    e21 = jnp.exp(m2 - m1)
    g1 = 1.0 / (1.0 + e21)
    g2 = e21 / (1.0 + e21)
    oh1 = lane == i1
    oh2 = lane == i2
    oh = (oh1 | oh2).astype(F32)
    r = lax.broadcasted_iota(jnp.int32, (tr, tr), 0)
    c = lax.broadcasted_iota(jnp.int32, (tr, tr), 1)
    before = (r > c).astype(BF16)
    prefix = jnp.dot(before, oh.astype(BF16), preferred_element_type=F32) + run_ref[...]
    r1 = jnp.sum(jnp.where(oh1, prefix, 0.0), axis=-1, keepdims=True)
    r2 = jnp.sum(jnp.where(oh2, prefix, 0.0), axis=-1, keepdims=True)
    run = run_ref[...] + jnp.sum(oh, axis=0, keepdims=True)
    run_ref[...] = run
    cnt_ref[...] = run
    info = jnp.zeros((tr, LANES), F32)
    for j, val in enumerate((i1, i2, r1, r2, g1, g2)):
        info = jnp.where(lane == j, val, info)
    info_ref[...] = info[:, :8]


def _router(x2, mod, mod_row, g, wr_hi, wr_lo, tr):
    m, d = x2.shape
    row = lambda w: pl.BlockSpec((1, 1, d), lambda i: (mod_row(i, tr) * 6 + w, 0, 0))
    const = lambda i: (0, 0)
    return pl.pallas_call(
        _router_kernel,
        grid=(m // tr,),
        in_specs=[
            pl.BlockSpec((tr, d), lambda i: (i, 0)),
            row(3), row(4),
            pl.BlockSpec((1, d), const),
            pl.BlockSpec((d, LANES), const),
            pl.BlockSpec((d, LANES), const),
        ],
        out_specs=[
            pl.BlockSpec((tr, d // 2), lambda i: (i, 0)),
            pl.BlockSpec((tr, 8), lambda i: (i, 0)),
            pl.BlockSpec((1, LANES), const),
        ],
        out_shape=[
            jax.ShapeDtypeStruct((m, d // 2), U32),
            jax.ShapeDtypeStruct((m, 8), F32),
            jax.ShapeDtypeStruct((1, LANES), F32),
        ],
        scratch_shapes=[pltpu.VMEM((1, LANES), F32)],
        compiler_params=_cparams("arbitrary"),
        name="moe_router",
    )(x2, mod, mod, g.reshape(1, d), wr_hi, wr_lo)


def _row_copy(src, src_row, dst, dst_row, sem):
    return pltpu.make_async_copy(src.at[pl.ds(src_row, 1)], dst.at[pl.ds(dst_row, 1)], sem)


def _dispatch_kernel(tg, pos_ref, fin_ref, init_ref, xs_ref, sem):
    del init_ref

    def issue(t, c):
        _row_copy(fin_ref, t, xs_ref, pos_ref[0, 2 * t], sem).start()
        _row_copy(fin_ref, t, xs_ref, pos_ref[0, 2 * t + 1], sem).start()
        return c

    lax.fori_loop(0, tg, issue, 0, unroll=8)
    for _ in range(2):
        pltpu.make_async_copy(fin_ref, xs_ref.at[pl.ds(0, tg)], sem).wait()


def _dispatch(fin, pos, cap, tg):
    m, w = fin.shape
    return pl.pallas_call(
        functools.partial(_dispatch_kernel, tg),
        grid=(m // tg,),
        in_specs=[
            pl.BlockSpec((None, 1, 2 * tg), lambda i: (i, 0, 0), memory_space=pltpu.SMEM),
            pl.BlockSpec((tg, w), lambda i: (i, 0)),
            pl.BlockSpec(memory_space=pl.ANY),
        ],
        out_specs=pl.BlockSpec(memory_space=pl.ANY),
        out_shape=jax.ShapeDtypeStruct((cap, w), U32),
        scratch_shapes=[pltpu.SemaphoreType.DMA(())],
        input_output_aliases={2: 0},
        compiler_params=_cparams("arbitrary"),
        name="moe_dispatch",
    )(pos.reshape(m // tg, 1, 2 * tg), fin, jnp.zeros((cap, w), U32))


def _moe_ffn_kernel(sub, te_ref, valid_ref, xs_ref, wg_ref, wu_ref, wd_ref, ys_ref, u_ref, acc_ref):
    i = pl.program_id(0)
    f = pl.program_id(1)
    tm, d = u_ref.shape
    valid = valid_ref[i]

    @pl.when(f == 0)
    def _():
        a, b = _unpack_pair(xs_ref[...])
        u_ref[:, :d // 2] = a.astype(BF16)
        u_ref[:, d // 2:] = b.astype(BF16)
        acc_ref[...] = jnp.zeros_like(acc_ref)

    def swiglu(nsub):
        wg = wg_ref[...].astype(BF16)
        wu = wu_ref[...].astype(BF16)
        wd = wd_ref[...].astype(BF16)
        hid = []
        for sb in range(nsub):
            u = u_ref[sb * sub:(sb + 1) * sub]
            hg = jnp.dot(u, wg, preferred_element_type=F32)
            hu = jnp.dot(u, wu, preferred_element_type=F32)
            hid.append((_silu(hg) * hu).astype(BF16))
        for sb in range(nsub):
            acc_ref[sb * sub:(sb + 1) * sub] += jnp.dot(hid[sb], wd, preferred_element_type=F32)

    nsub = tm // sub
    for n_valid in range(1, nsub + 1):
        hi = n_valid * sub if n_valid < nsub else tm
        @pl.when((valid > (n_valid - 1) * sub) & (valid <= hi))
        def _(n_valid=n_valid):
            swiglu(n_valid)

    @pl.when(f == pl.num_programs(1) - 1)
    def _():
        ys_ref[...] = _pack_pair(acc_ref[:, :d // 2], acc_ref[:, d // 2:])


def _moe_ffn(xs, tile_expert, tile_valid, wg, wu, wd, tm, tf, sub):
    cap, w = xs.shape
    d = 2 * w
    ff = wg.shape[3]
    nf = ff // tf

    def f_eff(i, f, valid):
        return jnp.where(valid[i] > 0, f, nf - 1)

    grid_spec = pltpu.PrefetchScalarGridSpec(
        num_scalar_prefetch=2,
        grid=(cap // tm, nf),
        in_specs=[
            pl.BlockSpec((tm, w), lambda i, f, te, valid: (i, 0)),
            pl.BlockSpec((None, None, d, tf), lambda i, f, te, valid: (0, te[i], 0, f_eff(i, f, valid))),
            pl.BlockSpec((None, None, d, tf), lambda i, f, te, valid: (0, te[i], 0, f_eff(i, f, valid))),
            pl.BlockSpec((None, None, tf, d), lambda i, f, te, valid: (0, te[i], f_eff(i, f, valid), 0)),
        ],
        out_specs=pl.BlockSpec((tm, w), lambda i, f, te, valid: (i, 0)),
        scratch_shapes=[pltpu.VMEM((tm, d), BF16), pltpu.VMEM((tm, d), F32)],
    )
    return pl.pallas_call(
        functools.partial(_moe_ffn_kernel, sub),
        grid_spec=grid_spec,
        out_shape=jax.ShapeDtypeStruct((cap, w), U32),
        compiler_params=_cparams("parallel", "arbitrary"),
        name="moe_ffn",
    )(tile_expert, tile_valid, xs, wg, wu, wd)


def _combine_kernel(tc, pos_ref, posn_ref, x_ref, info_ref, gate_ref, ys_ref, o_ref, buf_ref, sem):
    i = pl.program_id(0)
    slot = i % 2

    def issue(p_ref, s):
        def body(t, c):
            _row_copy(ys_ref, p_ref[0, 2 * t], buf_ref.at[s, 0], t, sem.at[s]).start()
            _row_copy(ys_ref, p_ref[0, 2 * t + 1], buf_ref.at[s, 1], t, sem.at[s]).start()
            return c

        lax.fori_loop(0, tc, body, 0, unroll=8)

    @pl.when(i == 0)
    def _():
        issue(pos_ref, 0)

    @pl.when(i + 1 < pl.num_programs(0))
    def _():
        issue(posn_ref, 1 - slot)

    for k in range(2):
        pltpu.make_async_copy(ys_ref.at[pl.ds(0, tc)], buf_ref.at[slot, k], sem.at[slot]).wait()
    w = buf_ref.shape[3]
    g1 = info_ref[:, 4:5]
    g2 = info_ref[:, 5:6]
    a1, b1 = _unpack_pair(buf_ref[slot, 0])
    a2, b2 = _unpack_pair(buf_ref[slot, 1])
    gate = gate_ref[0]
    o_ref[:, :w] = x_ref[:, :w] + gate[:, :w] * (g1 * a1 + g2 * a2)
    o_ref[:, w:] = x_ref[:, w:] + gate[:, w:] * (g1 * b1 + g2 * b2)


def _combine(x2, info, pos, ys, mod, mod_row, tc):
    m, d = x2.shape
    w = ys.shape[1]
    n = m // tc
    pos3 = pos.reshape(n, 1, 2 * tc)
    return pl.pallas_call(
        functools.partial(_combine_kernel, tc),
        grid=(n,),
        in_specs=[
            pl.BlockSpec((None, 1, 2 * tc), lambda i: (i, 0, 0), memory_space=pltpu.SMEM),
            pl.BlockSpec((None, 1, 2 * tc), lambda i: (jnp.minimum(i + 1, n - 1), 0, 0), memory_space=pltpu.SMEM),
            pl.BlockSpec((tc, d), lambda i: (i, 0)),
            pl.BlockSpec((tc, 8), lambda i: (i, 0)),
            pl.BlockSpec((1, 1, d), lambda i: (mod_row(i, tc) * 6 + 5, 0, 0)),
            pl.BlockSpec(memory_space=pl.ANY),
        ],
        out_specs=pl.BlockSpec((tc, d), lambda i: (i, 0)),
        out_shape=jax.ShapeDtypeStruct((m, d), F32),
        scratch_shapes=[pltpu.VMEM((2, 2, tc, w), U32), pltpu.SemaphoreType.DMA((2,))],
        compiler_params=_cparams("arbitrary"),
        name="moe_combine",
    )(pos3, pos3, x2, info, mod, ys)


MOE_SUB = 512


def _moe(x2, s, mod, mod_row, g, router_w, wg, wu, wd):
    m, d = x2.shape
    tm = _tile(m, 1024)
    sub = _tile(tm, MOE_SUB)
    tf = _tile(wg.shape[3], 256)
    tr = _tile(s, 512)
    tg = _tile(s, 256)
    wr = jnp.zeros((d, LANES), F32).at[:, :N_EXPERTS].set(router_w)
    wr_hi = wr.astype(BF16)
    wr_lo = (wr - wr_hi.astype(F32)).astype(BF16)
    fin, info, cnt = _router(x2, mod, mod_row, g, wr_hi, wr_lo, tr)

    counts = cnt[0, :N_EXPERTS].astype(jnp.int32)
    padded = (counts + tm - 1) // tm * tm
    ends = jnp.cumsum(padded)
    starts = ends - padded
    experts = info[:, 0:2].astype(jnp.int32)
    pos = starts[experts] + info[:, 2:4].astype(jnp.int32)
    ntiles = (2 * m) // tm + N_EXPERTS
    cap = ntiles * tm
    tile_start = jnp.arange(ntiles, dtype=jnp.int32) * tm
    tile_expert = jnp.sum(tile_start[:, None] >= ends[None, :], axis=1).astype(jnp.int32)
    active = tile_start < ends[-1]
    last_expert = tile_expert[ends[-1] // tm - 1]
    tile_expert = jnp.where(active, tile_expert, last_expert)
    group_end = (starts + counts)[tile_expert]
    tile_valid = jnp.where(active, jnp.clip(group_end - tile_start, 0, tm), 0).astype(jnp.int32)

    xs = _dispatch(fin, pos, cap, tg)
    ys = _moe_ffn(xs, tile_expert, tile_valid, wg, wu, wd, tm, tf, sub)
    return _combine(x2, info, pos, ys, mod, mod_row, tg)


def _rope_tables(t, rot_dim):
    rows = t // GRID_W
    row = jnp.repeat(jnp.arange(rows, dtype=F32), GRID_W)
    col = jnp.tile(jnp.arange(GRID_W, dtype=F32), rows)
    half = rot_dim // 2
    inv = ROPE_THETA ** (-jnp.arange(0, half, 2, dtype=F32) / half)
    ang_r = row[:, None] * inv[None, :]
    ang_c = col[:, None] * inv[None, :]
    pad = LANES - rot_dim
    cos = jnp.concatenate([jnp.cos(ang_r), jnp.cos(ang_r), jnp.cos(ang_c), jnp.cos(ang_c),
                           jnp.ones((t, pad), F32)], axis=1)
    sin = jnp.concatenate([-jnp.sin(ang_r), jnp.sin(ang_r), -jnp.sin(ang_c), jnp.sin(ang_c),
                           jnp.zeros((t, pad), F32)], axis=1)
    return cos, sin


def _identity_rope(t):
    return jnp.ones((t, LANES), F32), jnp.zeros((t, LANES), F32)


def _pad_heads(w, real, padded):
    k = w.shape[0]
    w = w.reshape(k, MLA_HEADS, real)
    return jnp.pad(w, ((0, 0), (0, 0), (0, padded - real))).reshape(k, MLA_HEADS * padded)


def kernel(x, c, ctx, c_ctx, ada_w, ada_b, mix_norm_g, ffn_norm_g, even_w_in, mla_q_a_norm_g, mla_w_q_b, mla_kv_a_norm_g, mla_w_kv_b, mla_q_norm_g, mla_k_norm_g, conv_dw_w, conv_dw_b, conv_ln_g, conv_ln_b, even_w_out, dense_w_gate, dense_w_up, dense_w_down, odd_w_in, swa_q_norm_g, swa_k_norm_g, swa_sink, odd_w_out, router_w, expert_w_gate, expert_w_up, expert_w_down):
    b, s, d = x.shape
    l = ctx.shape[1]
    assert ada_w.shape[0] == 2, "two layers: an even (MLA | conv, dense) then an odd (SWA | Fourier, MoE) one"

    r = (b + 1 + 7) // 8 * 8
    cvec = jnp.zeros((r, d), F32).at[:b].set(c).at[b].set(c_ctx)
    mod = _modulation(cvec, ada_w, ada_b).reshape(2, r * 6, 1, d)
    mod0, mod1 = mod[0], mod[1]

    tm = _tile(s, 512)
    tml = _tile(b * l, 512)
    lat_row = lambda i, tile: (i * tile) // s
    ctx_row = lambda i, tile: b
    x2 = x.reshape(b * s, d)
    h2 = ctx.reshape(b * l, d)

    w_in = even_w_in[0]
    w_in0 = jnp.concatenate(
        [w_in[:, MLA_IN:], w_in[:, :MLA_IN], jnp.zeros((d, MLA_IN_PAD - MLA_IN), F32)], axis=1).astype(BF16)
    p_lat = _norm_mod_matmul(x2, mod0, lat_row, 0, mix_norm_g[0], w_in0, tm)
    p_ctx = _norm_mod_matmul(h2, mod0, ctx_row, 0, mix_norm_g[0], w_in0, tml)

    mla_scale = MLA_QK ** -0.5
    wq = _pad_heads(mla_w_q_b[0], MLA_QK, MLA_QK_PAD).astype(BF16)
    wkv = mla_w_kv_b[0].astype(BF16)
    qg = jnp.pad(mla_q_norm_g[0] * mla_scale, (0, MLA_QK_PAD - MLA_QK)).reshape(1, -1)
    kg = jnp.pad(mla_k_norm_g[0], (0, MLA_QK_PAD - MLA_QK)).reshape(1, -1)
    qag = mla_q_a_norm_g[0].reshape(1, -1)
    kvag = mla_kv_a_norm_g[0].reshape(1, -1)
    mla_col = 2 * CONV_CH // MLA_IN_PAD
    cos_m, sin_m = _rope_tables(s, MLA_ROPE)
    q_l, k_l, v_l = _mla_prep(p_lat, b, s, mla_col, cos_m, sin_m, qag, kvag, wq, wkv, qg, kg)
    cos_i, sin_i = _identity_rope(l)
    q_c, k_c, v_c = _mla_prep(p_ctx, b, l, mla_col, cos_i, sin_i, qag, kvag, wq, wkv, qg, kg)
    att_l = _mla_attn(q_l, [(k_l, v_l), (k_c, v_c)]).reshape(b * s, -1)
    att_c = _mla_attn(q_c, [(k_c, v_c)]).reshape(b * l, -1)
    conv_l = _conv_module(p_lat, b, s, conv_dw_w[0], conv_dw_b[0], conv_ln_g[0], conv_ln_b[0])
    conv_c = _conv_module(p_ctx, b, l, conv_dw_w[0], conv_dw_b[0], conv_ln_g[0], conv_ln_b[0])

    w_out = even_w_out[0].astype(BF16)
    k_att = MLA_HEADS * MLA_V
    x2 = _out_proj_residual(att_l, conv_l, w_out[:k_att], w_out[k_att:], x2, mod0, lat_row, 2, tm)
    h2 = _out_proj_residual(att_c, conv_c, w_out[:k_att], w_out[k_att:], h2, mod0, ctx_row, 2, tml)

    wg = dense_w_gate[0].astype(BF16)
    wu = dense_w_up[0].astype(BF16)
    wd = dense_w_down[0].astype(BF16)
    tf = _tile(wg.shape[1], 256)
    x2 = _dense_ffn(x2, mod0, lat_row, ffn_norm_g[0], wg, wu, wd, _tile(s, 1024), tf)
    h2 = _dense_ffn(h2, mod0, ctx_row, ffn_norm_g[0], wg, wu, wd, _tile(b * l, 1024), tf)

    w_in1 = odd_w_in[0].astype(BF16)
    p = _norm_mod_matmul(x2, mod1, lat_row, 0, mix_norm_g[1], w_in1, tm)
    w_kv_c = w_in1[:, ODD_Q:ODD_Q + 2 * ODD_KV]
    pc = _norm_mod_matmul(h2, mod1, ctx_row, 0, mix_norm_g[1], w_kv_c, tml)

    cos_s, sin_s = _rope_tables(s, HEAD_DIM)
    gains = jnp.concatenate([jnp.tile(swa_q_norm_g[0] * HEAD_DIM ** -0.5, WIN_Q_HEADS),
                             jnp.tile(swa_k_norm_g[0], WIN_KV_HEADS)]).reshape(1, -1)
    qk = _swa_prep(p, b, s, WIN_Q_HEADS + WIN_KV_HEADS, 0, cos_s, sin_s, gains)
    kc = _swa_prep(pc, b, l, WIN_KV_HEADS, 0, cos_i, sin_i, gains[:, ODD_Q:])
    att = _swa_attn(qk, p, kc, pc, swa_sink[0], b, s, l).reshape(b * s, -1)
    fcol = (ODD_Q + 2 * ODD_KV) // (FNET_GROUPS * FNET_CH)
    four = _fourier(p, b, s, fcol).reshape(b * s, -1)

    w_out1 = odd_w_out[0].astype(BF16)
    x2 = _out_proj_residual(att, four, w_out1[:ODD_Q], w_out1[ODD_Q:], x2, mod1, lat_row, 2, tm)

    x2 = _moe(x2, s, mod1, lat_row, ffn_norm_g[1], router_w[0],
              expert_w_gate, expert_w_up, expert_w_down)
    return x2.reshape(b, s, d)
```

```python
import functools

import jax
import jax.numpy as jnp
import numpy as np
from jax import lax
from jax.experimental import pallas as pl
from jax.experimental.pallas import tpu as pltpu

F32 = jnp.float32
BF16 = jnp.bfloat16
U32 = jnp.uint32

EPS = 1e-6
ROPE_THETA = 10000.0
GRID_W = 64
NEG_INF = -1e30
LANES = 128

MLA_HEADS = 8
MLA_Q_RANK = 512
MLA_KV_RANK = 256
MLA_NOPE = 128
MLA_ROPE = 64
MLA_V = 128
MLA_IN = MLA_Q_RANK + MLA_KV_RANK + MLA_ROPE
MLA_QK = MLA_NOPE + MLA_ROPE
MLA_QK_PAD = 2 * LANES
MLA_IN_PAD = 1024
CONV_CH = 1024
CONV_WIDTH = 31
CONV_HALO = 16
HEAD_DIM = 128
WIN_Q_HEADS = 12
WIN_KV_HEADS = 4
WIN_GROUP = WIN_Q_HEADS // WIN_KV_HEADS
WINDOW = 128
BLOCK = 128
FNET_GROUPS = 4
FNET_CH = 128
ODD_Q = WIN_Q_HEADS * HEAD_DIM
ODD_KV = WIN_KV_HEADS * HEAD_DIM
N_EXPERTS = 8

VMEM_LIMIT = 60 * 1024 * 1024


def _cparams(*sem):
    return pltpu.CompilerParams(dimension_semantics=sem, vmem_limit_bytes=VMEM_LIMIT)


def _tile(n, pref):
    if n <= pref:
        return n
    t = pref - pref % 8
    while n % t:
        t -= 8
    return t


def _sigmoid(x):
    return 1.0 / (1.0 + jnp.exp(-x))


def _silu(x):
    return x * _sigmoid(x)


def _rms_mod(x, g, shift, scale):
    ms = jnp.mean(x * x, axis=-1, keepdims=True)
    return (x * lax.rsqrt(ms + EPS) * g) * (1.0 + scale) + shift


NORM_ROWS = 16


def _rms_mod_rows(x_ref, rows, gain, shift):
    x = x_ref[rows, :]
    ms = jnp.mean(x * x, axis=-1, keepdims=True)
    return (x * lax.rsqrt(ms + EPS) * gain + shift).astype(BF16)


def _rms_mod_loop(x_ref, g_ref, sh_ref, sc_ref, dst_ref):
    gain = g_ref[...] * (1.0 + sc_ref[0])
    shift = sh_ref[0]

    def body(c, carry):
        rows = pl.ds(pl.multiple_of(c * NORM_ROWS, NORM_ROWS), NORM_ROWS)
        dst_ref[rows, :] = _rms_mod_rows(x_ref, rows, gain, shift)
        return carry

    lax.fori_loop(0, x_ref.shape[0] // NORM_ROWS, body, 0, unroll=8)


def _pack_pair(a, b):
    ai = lax.bitcast_convert_type(a.astype(BF16).astype(F32), U32)
    bi = lax.bitcast_convert_type(b.astype(BF16).astype(F32), U32)
    return (ai >> 16) | bi


def _unpack_pair(w):
    a = lax.bitcast_convert_type(w << 16, F32)
    b = lax.bitcast_convert_type(w & jnp.uint32(0xFFFF0000), F32)
    return a, b


def _modulation_kernel(c_ref, w_ref, b_ref, o_ref):
    a = _silu(c_ref[...]).astype(BF16)
    acc = jnp.dot(a, w_ref[...].astype(BF16), preferred_element_type=F32)
    o_ref[...] = acc + b_ref[...]


def _modulation(cvec, ada_w, ada_b):
    depth, d, n = ada_w.shape
    r = cvec.shape[0]
    tn = _tile(n, 1024)
    return pl.pallas_call(
        _modulation_kernel,
        grid=(depth, n // tn),
        in_specs=[
            pl.BlockSpec((r, d), lambda l, j: (0, 0)),
            pl.BlockSpec((None, d, tn), lambda l, j: (l, 0, j)),
            pl.BlockSpec((None, 1, tn), lambda l, j: (l, 0, j)),
        ],
        out_specs=pl.BlockSpec((None, r, tn), lambda l, j: (l, 0, j)),
        out_shape=jax.ShapeDtypeStruct((depth, r, n), F32),
        compiler_params=_cparams("parallel", "parallel"),
        name="modulation",
    )(cvec, ada_w, ada_b.reshape(depth, 1, n))


def _nmm_kernel(x_ref, sh_ref, sc_ref, g_ref, w_ref, o_ref, u0_ref, u1_ref):
    i = pl.program_id(0)

    @pl.when((i == 0) & (pl.program_id(1) == 0))
    def _():
        _rms_mod_loop(x_ref, g_ref, sh_ref, sc_ref, u0_ref)

    def step(cur_ref, nxt_ref):
        gain = g_ref[...] * (1.0 + sc_ref[0])
        shift = sh_ref[0]
        for c in range(x_ref.shape[0] // NORM_ROWS):
            rows = slice(c * NORM_ROWS, (c + 1) * NORM_ROWS)
            nxt_ref[rows, :] = _rms_mod_rows(x_ref, rows, gain, shift)
        o_ref[...] = jnp.dot(cur_ref[...], w_ref[...], preferred_element_type=F32).astype(o_ref.dtype)

    @pl.when(i % 2 == 0)
    def _():
        step(u0_ref, u1_ref)

    @pl.when(i % 2 == 1)
    def _():
        step(u1_ref, u0_ref)


def _norm_mod_matmul(x2, mod, mod_row, which, g, w, tm):
    m, d = x2.shape
    n = w.shape[1]
    tn = n // 2
    nt = m // tm
    ahead = lambda i, j: jnp.minimum(i + jnp.minimum(j, 1), nt - 1)
    return pl.pallas_call(
        _nmm_kernel,
        grid=(nt, 2),
        in_specs=[
            pl.BlockSpec((tm, d), lambda i, j: (ahead(i, j), 0)),
            pl.BlockSpec((1, 1, d), lambda i, j: (mod_row(ahead(i, j), tm) * 6 + which, 0, 0)),
            pl.BlockSpec((1, 1, d), lambda i, j: (mod_row(ahead(i, j), tm) * 6 + which + 1, 0, 0)),
            pl.BlockSpec((1, d), lambda i, j: (0, 0)),
            pl.BlockSpec((d, tn), lambda i, j: (0, j)),
        ],
        out_specs=pl.BlockSpec((tm, tn), lambda i, j: (i, j)),
        out_shape=jax.ShapeDtypeStruct((m, n), BF16),
        scratch_shapes=[pltpu.VMEM((tm, d), BF16), pltpu.VMEM((tm, d), BF16)],
        compiler_params=_cparams("arbitrary", "arbitrary"),
        name="norm_mod_matmul",
    )(x2, mod, mod, g.reshape(1, d), w)


def _split_dot(x, m01):
    hi = x.astype(BF16)
    lo = (x - hi.astype(F32)).astype(BF16)
    return jnp.dot(hi, m01, preferred_element_type=F32) + jnp.dot(lo, m01, preferred_element_type=F32)


def _rope_mxu(t, cos, sin, perm):
    return t * cos + _split_dot(t, perm) * sin


def _mla_prep_kernel(p_ref, cos_ref, sin_ref, qag_ref, kvag_ref, wq_ref, wkv_ref, qg_ref, kg_ref,
                     perm_ref, q_ref, k_ref, v_ref):
    p = p_ref[...].astype(F32)
    qa = p[:, :MLA_Q_RANK]
    kva = p[:, MLA_Q_RANK:MLA_Q_RANK + MLA_KV_RANK]
    kpe = p[:, MLA_Q_RANK + MLA_KV_RANK:MLA_Q_RANK + MLA_KV_RANK + LANES]
    qn = qa * lax.rsqrt(jnp.mean(qa * qa, axis=-1, keepdims=True) + EPS) * qag_ref[...]
    kvn = kva * lax.rsqrt(jnp.mean(kva * kva, axis=-1, keepdims=True) + EPS) * kvag_ref[...]
    q = jnp.dot(qn.astype(BF16), wq_ref[...], preferred_element_type=F32)
    kv = jnp.dot(kvn.astype(BF16), wkv_ref[...], preferred_element_type=F32)
    cos = cos_ref[...]
    sin = sin_ref[...]
    qg = qg_ref[...]
    kg = kg_ref[...]
    perm = perm_ref[...]
    kpe_ss = jnp.sum(kpe * kpe, axis=-1, keepdims=True)
    for h in range(MLA_HEADS):
        lo = h * MLA_QK_PAD
        qh = q[:, lo:lo + MLA_QK_PAD]
        rs = lax.rsqrt(jnp.sum(qh * qh, axis=-1, keepdims=True) * (1.0 / MLA_QK) + EPS)
        q_ref[0, h, :, :LANES] = (qh[:, :LANES] * rs * qg[:, :LANES]).astype(BF16)
        tail = _rope_mxu(qh[:, LANES:] * rs * qg[:, LANES:], cos, sin, perm)
        q_ref[0, h, :, LANES:] = tail.astype(BF16)
        kn = kv[:, lo:lo + MLA_NOPE]
        rs = lax.rsqrt((jnp.sum(kn * kn, axis=-1, keepdims=True) + kpe_ss) * (1.0 / MLA_QK) + EPS)
        k_ref[0, h, :, :LANES] = (kn * rs * kg[:, :LANES]).astype(BF16)
        tail = _rope_mxu(kpe * rs * kg[:, LANES:], cos, sin, perm)
        k_ref[0, h, :, LANES:] = tail.astype(BF16)
        v_ref[0, h] = kv[:, lo + MLA_NOPE:lo + MLA_NOPE + MLA_V].astype(BF16)


def _rope_perm(grp):
    lane = np.arange(LANES)
    src = np.where((lane // grp) % 2 == 0, lane + grp, lane - grp)
    perm = np.zeros((LANES, LANES), np.float32)
    perm[src, lane] = 1.0
    return jnp.asarray(perm, BF16)


def _mla_prep(p, b, t, col_block, cos, sin, qag, kvag, wq, wkv, qg, kg):
    tm = _tile(t, 256)
    nt = t // tm
    const = lambda bb, i: (0, 0)
    perm = _rope_perm(MLA_ROPE // 4)
    hs = lambda w: pl.BlockSpec((1, MLA_HEADS, tm, w), lambda bb, i: (bb, 0, i, 0))
    return pl.pallas_call(
        _mla_prep_kernel,
        grid=(b, nt),
        in_specs=[
            pl.BlockSpec((tm, MLA_IN_PAD), lambda bb, i: (bb * nt + i, col_block)),
            pl.BlockSpec((tm, LANES), lambda bb, i: (i, 0)),
            pl.BlockSpec((tm, LANES), lambda bb, i: (i, 0)),
            pl.BlockSpec((1, MLA_Q_RANK), const),
            pl.BlockSpec((1, MLA_KV_RANK), const),
            pl.BlockSpec(wq.shape, const),
            pl.BlockSpec(wkv.shape, const),
            pl.BlockSpec((1, MLA_QK_PAD), const),
            pl.BlockSpec((1, MLA_QK_PAD), const),
            pl.BlockSpec((LANES, LANES), const),
        ],
        out_specs=[hs(MLA_QK_PAD), hs(MLA_QK_PAD), hs(MLA_V)],
        out_shape=[
            jax.ShapeDtypeStruct((b, MLA_HEADS, t, MLA_QK_PAD), BF16),
            jax.ShapeDtypeStruct((b, MLA_HEADS, t, MLA_QK_PAD), BF16),
            jax.ShapeDtypeStruct((b, MLA_HEADS, t, MLA_V), BF16),
        ],
        compiler_params=_cparams("parallel", "parallel"),
        name="mla_prep",
    )(p, cos, sin, qag, kvag, wq, wkv, qg, kg, perm)


def _nt_dot(a, b):
    return lax.dot_general(a, b, (((1,), (1,)), ((), ())), preferred_element_type=F32)


MLA_HEADS_PER_STEP = 4


def _mla_attn_kernel(nseg, q_ref, *refs):
    o_ref = refs[2 * nseg]
    for hp in range(MLA_HEADS_PER_STEP):
        q = q_ref[0, hp]
        s = [_nt_dot(q, refs[2 * i][0, hp]) for i in range(nseg)]
        m = s[0].max(axis=-1, keepdims=True)
        for si in s[1:]:
            m = jnp.maximum(m, si.max(axis=-1, keepdims=True))
        den = 0.0
        acc = 0.0
        for i in range(nseg):
            e = jnp.exp(s[i] - m)
            den = den + jnp.sum(e, axis=-1, keepdims=True)
            acc = acc + jnp.dot(e.astype(BF16), refs[2 * i + 1][0, hp], preferred_element_type=F32)
        o_ref[0, :, hp * MLA_V:(hp + 1) * MLA_V] = (acc / den).astype(o_ref.dtype)


def _mla_attn(q, kvs):
    b, h, t, _ = q.shape
    tq = _tile(t, 256)
    hp = MLA_HEADS_PER_STEP
    in_specs = [pl.BlockSpec((1, hp, tq, MLA_QK_PAD), lambda bb, hh, i: (bb, hh, i, 0))]
    args = [q]
    for k, v in kvs:
        n = k.shape[2]
        in_specs.append(pl.BlockSpec((1, hp, n, MLA_QK_PAD), lambda bb, hh, i: (bb, hh, 0, 0)))
        in_specs.append(pl.BlockSpec((1, hp, n, MLA_V), lambda bb, hh, i: (bb, hh, 0, 0)))
        args += [k, v]
    return pl.pallas_call(
        functools.partial(_mla_attn_kernel, len(kvs)),
        grid=(b, h // hp, t // tq),
        in_specs=in_specs,
        out_specs=pl.BlockSpec((1, tq, hp * MLA_V), lambda bb, hh, i: (bb, i, hh)),
        out_shape=jax.ShapeDtypeStruct((b, t, h * MLA_V), BF16),
        compiler_params=_cparams("parallel", "parallel", "parallel"),
        name="mla_attn",
    )(*args)


CONV_ROWS = 64
CONV_WIN = CONV_ROWS + 2 * CONV_HALO


def _conv_kernel(t, p_ref, w_ref, b_ref, g_ref, beta_ref, o_ref, hp_ref, cv_ref):
    zeros = jnp.zeros((CONV_HALO, CONV_CH), F32)
    hp_ref[0:CONV_HALO, :] = zeros
    hp_ref[CONV_HALO + t:CONV_HALO + t + CONV_HALO, :] = zeros

    def glu(i, c):
        r0 = pl.multiple_of(i * CONV_ROWS, CONV_ROWS)
        a = p_ref[pl.ds(r0, CONV_ROWS), :CONV_CH].astype(F32)
        gate = p_ref[pl.ds(r0, CONV_ROWS), CONV_CH:].astype(F32)
        hp_ref[pl.ds(r0 + CONV_HALO, CONV_ROWS), :] = a * _sigmoid(gate)
        return c

    lax.fori_loop(0, t // CONV_ROWS, glu, 0)

    def tile(i, c):
        r0 = pl.multiple_of(i * CONV_ROWS, CONV_ROWS)
        for cc in range(CONV_CH // LANES):
            cs = slice(cc * LANES, (cc + 1) * LANES)
            win = hp_ref[pl.ds(r0, CONV_WIN), cs]
            acc = jnp.zeros((CONV_ROWS, LANES), F32) + b_ref[:, cs]
            for r in range(8):
                rolled = win if r == 0 else pltpu.roll(win, CONV_WIN - r, axis=0)
                for k in range(CONV_WIDTH):
                    off = k + CONV_HALO - CONV_WIDTH // 2
                    if off % 8 == r:
                        acc = acc + rolled[off - r:off - r + CONV_ROWS] * w_ref[k:k + 1, cs]
            cv_ref[:, cs] = acc
        h = cv_ref[...]
        mu = jnp.mean(h, axis=-1, keepdims=True)
        hc = h - mu
        y = hc * lax.rsqrt(jnp.mean(hc * hc, axis=-1, keepdims=True) + EPS) * g_ref[...] + beta_ref[...]
        o_ref[pl.ds(r0, CONV_ROWS), :] = _silu(y).astype(o_ref.dtype)
        return c

    lax.fori_loop(0, t // CONV_ROWS, tile, 0)


def _conv_module(p, b, t, dw_w, dw_b, ln_g, ln_b):
    const = lambda bb: (0, 0)
    return pl.pallas_call(
        functools.partial(_conv_kernel, t),
        grid=(b,),
        in_specs=[
            pl.BlockSpec((t, 2 * CONV_CH), lambda bb: (bb, 0)),
            pl.BlockSpec((CONV_WIDTH, CONV_CH), const),
            pl.BlockSpec((1, CONV_CH), const),
            pl.BlockSpec((1, CONV_CH), const),
            pl.BlockSpec((1, CONV_CH), const),
        ],
        out_specs=pl.BlockSpec((t, CONV_CH), lambda bb: (bb, 0)),
        out_shape=jax.ShapeDtypeStruct((b * t, CONV_CH), BF16),
        scratch_shapes=[pltpu.VMEM((t + 2 * CONV_HALO, CONV_CH), F32), pltpu.VMEM((CONV_ROWS, CONV_CH), F32)],
        compiler_params=_cparams("parallel"),
        name="conv_module",
    )(p, dw_w, dw_b.reshape(1, -1), ln_g.reshape(1, -1), ln_b.reshape(1, -1))


def _oproj_kernel(a1_ref, a2_ref, w1_ref, w2_ref, x_ref, gate_ref, o_ref):
    acc = jnp.dot(a1_ref[...], w1_ref[...], preferred_element_type=F32)
    acc = acc + jnp.dot(a2_ref[...], w2_ref[...], preferred_element_type=F32)
    o_ref[...] = x_ref[...] + gate_ref[0] * acc


def _out_proj_residual(a1, a2, w1, w2, x2, mod, mod_row, which, tm):
    m, d = x2.shape
    k1, k2 = w1.shape[0], w2.shape[0]
    const = lambda i: (0, 0)
    return pl.pallas_call(
        _oproj_kernel,
        grid=(m // tm,),
        in_specs=[
            pl.BlockSpec((tm, k1), lambda i: (i, 0)),
            pl.BlockSpec((tm, k2), lambda i: (i, 0)),
            pl.BlockSpec((k1, d), const),
            pl.BlockSpec((k2, d), const),
            pl.BlockSpec((tm, d), lambda i: (i, 0)),
            pl.BlockSpec((1, 1, d), lambda i: (mod_row(i, tm) * 6 + which, 0, 0)),
        ],
        out_specs=pl.BlockSpec((tm, d), lambda i: (i, 0)),
        out_shape=jax.ShapeDtypeStruct((m, d), F32),
        compiler_params=_cparams("parallel"),
        name="out_proj_residual",
    )(a1, a2, w1, w2, x2, mod)


def _ffn_kernel(x_ref, sh_ref, sc_ref, gate_ref, g_ref, wg_ref, wu_ref, wd_ref, o_ref, u_ref, acc_ref):
    f = pl.program_id(1)

    @pl.when(f == 0)
    def _():
        _rms_mod_loop(x_ref, g_ref, sh_ref, sc_ref, u_ref)
        acc_ref[...] = jnp.zeros_like(acc_ref)

    tm = u_ref.shape[0]
    halves = [slice(0, tm // 2), slice(tm // 2, tm)]
    hid = []
    for rows in halves:
        u = u_ref[rows]
        hg = jnp.dot(u, wg_ref[...], preferred_element_type=F32)
        hu = jnp.dot(u, wu_ref[...], preferred_element_type=F32)
        hid.append((_silu(hg) * hu).astype(BF16))
    for rows, h in zip(halves, hid):
        acc_ref[rows] += jnp.dot(h, wd_ref[...], preferred_element_type=F32)

    @pl.when(f == pl.num_programs(1) - 1)
    def _():
        o_ref[...] = x_ref[...] + gate_ref[0] * acc_ref[...]


def _dense_ffn(x2, mod, mod_row, g, wg, wu, wd, tm, tf):
    m, d = x2.shape
    ff = wg.shape[1]
    row = lambda w: pl.BlockSpec((1, 1, d), lambda i, f: (mod_row(i, tm) * 6 + w, 0, 0))
    return pl.pallas_call(
        _ffn_kernel,
        grid=(m // tm, ff // tf),
        in_specs=[
            pl.BlockSpec((tm, d), lambda i, f: (i, 0)),
            row(3), row(4), row(5),
            pl.BlockSpec((1, d), lambda i, f: (0, 0)),
            pl.BlockSpec((d, tf), lambda i, f: (0, f)),
            pl.BlockSpec((d, tf), lambda i, f: (0, f)),
            pl.BlockSpec((tf, d), lambda i, f: (f, 0)),
        ],
        out_specs=pl.BlockSpec((tm, d), lambda i, f: (i, 0)),
        out_shape=jax.ShapeDtypeStruct((m, d), F32),
        scratch_shapes=[pltpu.VMEM((tm, d), BF16), pltpu.VMEM((tm, d), F32)],
        compiler_params=_cparams("parallel", "arbitrary"),
        name="dense_ffn",
    )(x2, mod, mod, mod, g.reshape(1, d), wg, wu, wd)


def _swa_prep_kernel(nheads, p_ref, cos_ref, sin_ref, g_ref, perm_ref, ones_ref, o_ref):
    cos = cos_ref[...]
    sin = sin_ref[...]
    perm = perm_ref[...]
    ones = ones_ref[...]
    for h in range(nheads):
        cs = slice(h * HEAD_DIM, (h + 1) * HEAD_DIM)
        t = p_ref[:, cs].astype(F32)
        ms = _split_dot(t * t, ones) * (1.0 / HEAD_DIM)
        t = t * lax.rsqrt(ms + EPS) * g_ref[:, cs]
        o_ref[:, cs] = _rope_mxu(t, cos, sin, perm).astype(BF16)


def _swa_prep(p, b, t, nheads, col_block, cos, sin, gains):
    tm = _tile(t, 256)
    nt = t // tm
    w = nheads * HEAD_DIM
    return pl.pallas_call(
        functools.partial(_swa_prep_kernel, nheads),
        grid=(b, nt),
        in_specs=[
            pl.BlockSpec((tm, w), lambda bb, i: (bb * nt + i, col_block)),
            pl.BlockSpec((tm, LANES), lambda bb, i: (i, 0)),
            pl.BlockSpec((tm, LANES), lambda bb, i: (i, 0)),
            pl.BlockSpec((1, w), lambda bb, i: (0, 0)),
            pl.BlockSpec((LANES, LANES), lambda bb, i: (0, 0)),
            pl.BlockSpec((LANES, LANES), lambda bb, i: (0, 0)),
        ],
        out_specs=pl.BlockSpec((tm, w), lambda bb, i: (bb * nt + i, 0)),
        out_shape=jax.ShapeDtypeStruct((b * t, w), BF16),
        compiler_params=_cparams("parallel", "parallel"),
        name="swa_prep",
    )(p, cos, sin, gains, _rope_perm(HEAD_DIM // 4), jnp.ones((LANES, LANES), BF16))


def _swa_attn_kernel(s_len, sink_ref, q_ref, k_ref, v_ref, kc_ref, vc_ref, o_ref):
    blk = pl.program_id(1)
    span = BLOCK + 2 * WINDOW
    start = blk * BLOCK
    ws = pl.multiple_of(jnp.clip(start - WINDOW, 0, s_len - span), BLOCK)
    rows = WIN_GROUP * BLOCK
    row = lax.broadcasted_iota(jnp.int32, (rows, span), 0)
    col = lax.broadcasted_iota(jnp.int32, (rows, span), 1)
    in_window = jnp.abs((start + row % BLOCK) - (ws + col)) <= WINDOW
    rcol = lax.broadcasted_iota(jnp.int32, (rows, 1), 0) // BLOCK
    for n in range(WIN_KV_HEADS):
        hs = slice(n * HEAD_DIM, (n + 1) * HEAD_DIM)
        kw = k_ref[0, pl.ds(ws, span), hs]
        vw = v_ref[0, pl.ds(ws, span), hs]
        q0 = n * WIN_GROUP * HEAD_DIM
        q = jnp.concatenate(
            [q_ref[0, :, q0 + g * HEAD_DIM:q0 + (g + 1) * HEAD_DIM] for g in range(WIN_GROUP)], axis=0)
        s_w = jnp.where(in_window, _nt_dot(q, kw), NEG_INF)
        s_c = _nt_dot(q, kc_ref[0, :, hs])
        sink = jnp.zeros((rows, 1), F32)
        for g in range(WIN_GROUP):
            sink = jnp.where(rcol == g, sink_ref[n * WIN_GROUP + g], sink)
        m = jnp.maximum(jnp.maximum(s_w.max(axis=-1, keepdims=True), s_c.max(axis=-1, keepdims=True)), sink)
        e_w = jnp.exp(s_w - m)
        e_c = jnp.exp(s_c - m)
        den = jnp.sum(e_w, axis=-1, keepdims=True) + jnp.sum(e_c, axis=-1, keepdims=True) + jnp.exp(sink - m)
        acc = jnp.dot(e_w.astype(BF16), vw, preferred_element_type=F32)
        acc = acc + jnp.dot(e_c.astype(BF16), vc_ref[0, :, hs], preferred_element_type=F32)
        out = acc / den
        for g in range(WIN_GROUP):
            o_ref[0, :, q0 + g * HEAD_DIM:q0 + (g + 1) * HEAD_DIM] = (
                out[g * BLOCK:(g + 1) * BLOCK].astype(o_ref.dtype))


def _swa_attn(qk, p, kc, pc, sink, b, s_len, ctx_len):
    qk3 = qk.reshape(b, s_len, -1)
    p3 = p.reshape(b, s_len, -1)
    kc3 = kc.reshape(b, ctx_len, -1)
    pc3 = pc.reshape(b, ctx_len, -1)
    return pl.pallas_call(
        functools.partial(_swa_attn_kernel, s_len),
        grid=(b, s_len // BLOCK),
        in_specs=[
            pl.BlockSpec(memory_space=pltpu.SMEM),
            pl.BlockSpec((1, BLOCK, ODD_Q), lambda bb, i: (bb, i, 0)),
            pl.BlockSpec((1, s_len, ODD_KV), lambda bb, i: (bb, 0, ODD_Q // ODD_KV)),
            pl.BlockSpec((1, s_len, ODD_KV), lambda bb, i: (bb, 0, (ODD_Q + ODD_KV) // ODD_KV)),
            pl.BlockSpec((1, ctx_len, ODD_KV), lambda bb, i: (bb, 0, 0)),
            pl.BlockSpec((1, ctx_len, ODD_KV), lambda bb, i: (bb, 0, 1)),
        ],
        out_specs=pl.BlockSpec((1, BLOCK, ODD_Q), lambda bb, i: (bb, i, 0)),
        out_shape=jax.ShapeDtypeStruct((b, s_len, ODD_Q), BF16),
        compiler_params=_cparams("parallel", "parallel"),
        name="swa_attn",
    )(sink, qk3, qk3, p3, kc3, pc3)


def _fourier_kernel(scale, f_ref, cs_ref, ct_ref, st_ref, o_ref, xc_ref, xs_ref):
    @pl.when(pl.program_id(1) == 0)
    def _():
        for g in range(FNET_GROUPS):
            cs = slice(g * FNET_CH, (g + 1) * FNET_CH)
            r = jnp.dot(f_ref[0, :, cs], cs_ref[...], preferred_element_type=F32)
            xc_ref[:, cs] = r[:, :FNET_CH].astype(BF16)
            xs_ref[:, cs] = r[:, FNET_CH:].astype(BF16)

    y = jnp.dot(ct_ref[...], xc_ref[...], preferred_element_type=F32)
    y = y - jnp.dot(st_ref[...], xs_ref[...], preferred_element_type=F32)
    o_ref[0] = (y * scale).astype(o_ref.dtype)


def _dft_tables(n):
    k = jnp.arange(n, dtype=jnp.int32)
    ang = ((k[:, None] * k[None, :]) % n).astype(F32) * (2.0 * np.pi / n)
    return jnp.cos(ang), jnp.sin(ang)


def _fourier(p, b, t, col_block):
    w = FNET_GROUPS * FNET_CH
    p3 = p.reshape(b, t, -1)
    cc, sc = _dft_tables(FNET_CH)
    ct, st = _dft_tables(t)
    cs = jnp.concatenate([cc, sc], axis=1).astype(BF16)
    tk = _tile(t, 512)
    return pl.pallas_call(
        functools.partial(_fourier_kernel, float((t * FNET_CH) ** -0.5)),
        grid=(b, t // tk),
        in_specs=[
            pl.BlockSpec((1, t, w), lambda bb, i: (bb, 0, col_block)),
            pl.BlockSpec((FNET_CH, 2 * FNET_CH), lambda bb, i: (0, 0)),
            pl.BlockSpec((tk, t), lambda bb, i: (i, 0)),
            pl.BlockSpec((tk, t), lambda bb, i: (i, 0)),
        ],
        out_specs=pl.BlockSpec((1, tk, w), lambda bb, i: (bb, i, 0)),
        out_shape=jax.ShapeDtypeStruct((b, t, w), BF16),
        scratch_shapes=[pltpu.VMEM((t, w), BF16), pltpu.VMEM((t, w), BF16)],
        compiler_params=_cparams("parallel", "arbitrary"),
        name="fourier",
    )(p3, cs, ct.astype(BF16), st.astype(BF16))


def _router_kernel(x_ref, sh_ref, sc_ref, g_ref, wh_ref, wl_ref, fin_ref, info_ref, cnt_ref, run_ref):
    @pl.when(pl.program_id(0) == 0)
    def _():
        run_ref[...] = jnp.zeros_like(run_ref)

    u = _rms_mod(x_ref[...], g_ref[...], sh_ref[0], sc_ref[0])
    d = u.shape[1]
    fin_ref[...] = _pack_pair(u[:, :d // 2], u[:, d // 2:])
    u_hi = u.astype(BF16)
    u_lo = (u - u_hi.astype(F32)).astype(BF16)
    logits = (jnp.dot(u_hi, wh_ref[...], preferred_element_type=F32)
              + (jnp.dot(u_lo, wh_ref[...], preferred_element_type=F32)
                 + jnp.dot(u_hi, wl_ref[...], preferred_element_type=F32)))
    tr = logits.shape[0]
    lane = lax.broadcasted_iota(jnp.int32, (tr, LANES), 1).astype(F32)
    logits = jnp.where(lane < N_EXPERTS, logits, -jnp.inf)
    m1 = logits.max(axis=-1, keepdims=True)
    i1 = jnp.where(logits == m1, lane, float(LANES)).min(axis=-1, keepdims=True)
    rest = jnp.where(lane == i1, -jnp.inf, logits)
    m2 = rest.max(axis=-1, keepdims=True)
    i2 = jnp.where(rest == m2, lane, float(LANES)).min(axis=-1, keepdims=True)
    e21 = jnp.exp(m2 - m1)
    g1 = 1.0 / (1.0 + e21)
    g2 = e21 / (1.0 + e21)
    oh1 = lane == i1
    oh2 = lane == i2
    oh = (oh1 | oh2).astype(F32)
    r = lax.broadcasted_iota(jnp.int32, (tr, tr), 0)
    c = lax.broadcasted_iota(jnp.int32, (tr, tr), 1)
    before = (r > c).astype(BF16)
    prefix = jnp.dot(before, oh.astype(BF16), preferred_element_type=F32) + run_ref[...]
    r1 = jnp.sum(jnp.where(oh1, prefix, 0.0), axis=-1, keepdims=True)
    r2 = jnp.sum(jnp.where(oh2, prefix, 0.0), axis=-1, keepdims=True)
    run = run_ref[...] + jnp.sum(oh, axis=0, keepdims=True)
    run_ref[...] = run
    cnt_ref[...] = run
    info = jnp.zeros((tr, LANES), F32)
    for j, val in enumerate((i1, i2, r1, r2, g1, g2)):
        info = jnp.where(lane == j, val, info)
    info_ref[...] = info[:, :8]


def _router(x2, mod, mod_row, g, wr_hi, wr_lo, tr):
    m, d = x2.shape
    row = lambda w: pl.BlockSpec((1, 1, d), lambda i: (mod_row(i, tr) * 6 + w, 0, 0))
    const = lambda i: (0, 0)
    return pl.pallas_call(
        _router_kernel,
        grid=(m // tr,),
        in_specs=[
            pl.BlockSpec((tr, d), lambda i: (i, 0)),
            row(3), row(4),
            pl.BlockSpec((1, d), const),
            pl.BlockSpec((d, LANES), const),
            pl.BlockSpec((d, LANES), const),
        ],
        out_specs=[
            pl.BlockSpec((tr, d // 2), lambda i: (i, 0)),
            pl.BlockSpec((tr, 8), lambda i: (i, 0)),
            pl.BlockSpec((1, LANES), const),
        ],
        out_shape=[
            jax.ShapeDtypeStruct((m, d // 2), U32),
            jax.ShapeDtypeStruct((m, 8), F32),
            jax.ShapeDtypeStruct((1, LANES), F32),
        ],
        scratch_shapes=[pltpu.VMEM((1, LANES), F32)],
        compiler_params=_cparams("arbitrary"),
        name="moe_router",
    )(x2, mod, mod, g.reshape(1, d), wr_hi, wr_lo)


def _row_copy(src, src_row, dst, dst_row, sem):
    return pltpu.make_async_copy(src.at[pl.ds(src_row, 1)], dst.at[pl.ds(dst_row, 1)], sem)


def _dispatch_kernel(tg, pos_ref, fin_ref, init_ref, xs_ref, sem):
    del init_ref

    def issue(t, c):
        _row_copy(fin_ref, t, xs_ref, pos_ref[0, 2 * t], sem).start()
        _row_copy(fin_ref, t, xs_ref, pos_ref[0, 2 * t + 1], sem).start()
        return c

    lax.fori_loop(0, tg, issue, 0, unroll=8)
    for _ in range(2):
        pltpu.make_async_copy(fin_ref, xs_ref.at[pl.ds(0, tg)], sem).wait()


def _dispatch(fin, pos, cap, tg):
    m, w = fin.shape
    return pl.pallas_call(
        functools.partial(_dispatch_kernel, tg),
        grid=(m // tg,),
        in_specs=[
            pl.BlockSpec((None, 1, 2 * tg), lambda i: (i, 0, 0), memory_space=pltpu.SMEM),
            pl.BlockSpec((tg, w), lambda i: (i, 0)),
            pl.BlockSpec(memory_space=pl.ANY),
        ],
        out_specs=pl.BlockSpec(memory_space=pl.ANY),
        out_shape=jax.ShapeDtypeStruct((cap, w), U32),
        scratch_shapes=[pltpu.SemaphoreType.DMA(())],
        input_output_aliases={2: 0},
        compiler_params=_cparams("arbitrary"),
        name="moe_dispatch",
    )(pos.reshape(m // tg, 1, 2 * tg), fin, jnp.zeros((cap, w), U32))


def _moe_ffn_kernel(sub, te_ref, valid_ref, xs_ref, wg_ref, wu_ref, wd_ref, ys_ref, u_ref, acc_ref):
    i = pl.program_id(0)
    f = pl.program_id(1)
    tm, d = u_ref.shape
    valid = valid_ref[i]

    @pl.when(f == 0)
    def _():
        a, b = _unpack_pair(xs_ref[...])
        u_ref[:, :d // 2] = a.astype(BF16)
        u_ref[:, d // 2:] = b.astype(BF16)
        acc_ref[...] = jnp.zeros_like(acc_ref)

    def swiglu(nsub):
        wg = wg_ref[...].astype(BF16)
        wu = wu_ref[...].astype(BF16)
        wd = wd_ref[...].astype(BF16)
        hid = []
        for sb in range(nsub):
            u = u_ref[sb * sub:(sb + 1) * sub]
            hg = jnp.dot(u, wg, preferred_element_type=F32)
            hu = jnp.dot(u, wu, preferred_element_type=F32)
            hid.append((_silu(hg) * hu).astype(BF16))
        for sb in range(nsub):
            acc_ref[sb * sub:(sb + 1) * sub] += jnp.dot(hid[sb], wd, preferred_element_type=F32)

    nsub = tm // sub
    for n_valid in range(1, nsub + 1):
        hi = n_valid * sub if n_valid < nsub else tm
        @pl.when((valid > (n_valid - 1) * sub) & (valid <= hi))
        def _(n_valid=n_valid):
            swiglu(n_valid)

    @pl.when(f == pl.num_programs(1) - 1)
    def _():
        ys_ref[...] = _pack_pair(acc_ref[:, :d // 2], acc_ref[:, d // 2:])


def _moe_ffn(xs, tile_expert, tile_valid, wg, wu, wd, tm, tf, sub):
    cap, w = xs.shape
    d = 2 * w
    ff = wg.shape[3]
    nf = ff // tf

    def f_eff(i, f, valid):
        return jnp.where(valid[i] > 0, f, nf - 1)

    grid_spec = pltpu.PrefetchScalarGridSpec(
        num_scalar_prefetch=2,
        grid=(cap // tm, nf),
        in_specs=[
            pl.BlockSpec((tm, w), lambda i, f, te, valid: (i, 0)),
            pl.BlockSpec((None, None, d, tf), lambda i, f, te, valid: (0, te[i], 0, f_eff(i, f, valid))),
            pl.BlockSpec((None, None, d, tf), lambda i, f, te, valid: (0, te[i], 0, f_eff(i, f, valid))),
            pl.BlockSpec((None, None, tf, d), lambda i, f, te, valid: (0, te[i], f_eff(i, f, valid), 0)),
        ],
        out_specs=pl.BlockSpec((tm, w), lambda i, f, te, valid: (i, 0)),
        scratch_shapes=[pltpu.VMEM((tm, d), BF16), pltpu.VMEM((tm, d), F32)],
    )
    return pl.pallas_call(
        functools.partial(_moe_ffn_kernel, sub),
        grid_spec=grid_spec,
        out_shape=jax.ShapeDtypeStruct((cap, w), U32),
        compiler_params=_cparams("parallel", "arbitrary"),
        name="moe_ffn",
    )(tile_expert, tile_valid, xs, wg, wu, wd)


def _combine_kernel(tc, pos_ref, posn_ref, x_ref, info_ref, gate_ref, ys_ref, o_ref, buf_ref, sem):
    i = pl.program_id(0)
    slot = i % 2

    def issue(p_ref, s):
        def body(t, c):
            _row_copy(ys_ref, p_ref[0, 2 * t], buf_ref.at[s, 0], t, sem.at[s]).start()
            _row_copy(ys_ref, p_ref[0, 2 * t + 1], buf_ref.at[s, 1], t, sem.at[s]).start()
            return c

        lax.fori_loop(0, tc, body, 0, unroll=8)

    @pl.when(i == 0)
    def _():
        issue(pos_ref, 0)

    @pl.when(i + 1 < pl.num_programs(0))
    def _():
        issue(posn_ref, 1 - slot)

    for k in range(2):
        pltpu.make_async_copy(ys_ref.at[pl.ds(0, tc)], buf_ref.at[slot, k], sem.at[slot]).wait()
    w = buf_ref.shape[3]
    g1 = info_ref[:, 4:5]
    g2 = info_ref[:, 5:6]
    a1, b1 = _unpack_pair(buf_ref[slot, 0])
    a2, b2 = _unpack_pair(buf_ref[slot, 1])
    gate = gate_ref[0]
    o_ref[:, :w] = x_ref[:, :w] + gate[:, :w] * (g1 * a1 + g2 * a2)
    o_ref[:, w:] = x_ref[:, w:] + gate[:, w:] * (g1 * b1 + g2 * b2)


def _combine(x2, info, pos, ys, mod, mod_row, tc):
    m, d = x2.shape
    w = ys.shape[1]
    n = m // tc
    pos3 = pos.reshape(n, 1, 2 * tc)
    return pl.pallas_call(
        functools.partial(_combine_kernel, tc),
        grid=(n,),
        in_specs=[
            pl.BlockSpec((None, 1, 2 * tc), lambda i: (i, 0, 0), memory_space=pltpu.SMEM),
            pl.BlockSpec((None, 1, 2 * tc), lambda i: (jnp.minimum(i + 1, n - 1), 0, 0), memory_space=pltpu.SMEM),
            pl.BlockSpec((tc, d), lambda i: (i, 0)),
            pl.BlockSpec((tc, 8), lambda i: (i, 0)),
            pl.BlockSpec((1, 1, d), lambda i: (mod_row(i, tc) * 6 + 5, 0, 0)),
            pl.BlockSpec(memory_space=pl.ANY),
        ],
        out_specs=pl.BlockSpec((tc, d), lambda i: (i, 0)),
        out_shape=jax.ShapeDtypeStruct((m, d), F32),
        scratch_shapes=[pltpu.VMEM((2, 2, tc, w), U32), pltpu.SemaphoreType.DMA((2,))],
        compiler_params=_cparams("arbitrary"),
        name="moe_combine",
    )(pos3, pos3, x2, info, mod, ys)


MOE_SUB = 512


def _moe(x2, s, mod, mod_row, g, router_w, wg, wu, wd):
    m, d = x2.shape
    tm = _tile(m, 1024)
    sub = _tile(tm, MOE_SUB)
    tf = _tile(wg.shape[3], 512)
    tr = _tile(s, 512)
    tg = _tile(s, 256)
    wr = jnp.zeros((d, LANES), F32).at[:, :N_EXPERTS].set(router_w)
    wr_hi = wr.astype(BF16)
    wr_lo = (wr - wr_hi.astype(F32)).astype(BF16)
    fin, info, cnt = _router(x2, mod, mod_row, g, wr_hi, wr_lo, tr)

    counts = cnt[0, :N_EXPERTS].astype(jnp.int32)
    padded = (counts + tm - 1) // tm * tm
    ends = jnp.cumsum(padded)
    starts = ends - padded
    experts = info[:, 0:2].astype(jnp.int32)
    pos = starts[experts] + info[:, 2:4].astype(jnp.int32)
    ntiles = (2 * m) // tm + N_EXPERTS
    cap = ntiles * tm
    tile_start = jnp.arange(ntiles, dtype=jnp.int32) * tm
    tile_expert = jnp.sum(tile_start[:, None] >= ends[None, :], axis=1).astype(jnp.int32)
    active = tile_start < ends[-1]
    last_expert = tile_expert[ends[-1] // tm - 1]
    tile_expert = jnp.where(active, tile_expert, last_expert)
    group_end = (starts + counts)[tile_expert]
    tile_valid = jnp.where(active, jnp.clip(group_end - tile_start, 0, tm), 0).astype(jnp.int32)

    xs = _dispatch(fin, pos, cap, tg)
    ys = _moe_ffn(xs, tile_expert, tile_valid, wg, wu, wd, tm, tf, sub)
    return _combine(x2, info, pos, ys, mod, mod_row, tg)


def _rope_tables(t, rot_dim):
    rows = t // GRID_W
    row = jnp.repeat(jnp.arange(rows, dtype=F32), GRID_W)
    col = jnp.tile(jnp.arange(GRID_W, dtype=F32), rows)
    half = rot_dim // 2
    inv = ROPE_THETA ** (-jnp.arange(0, half, 2, dtype=F32) / half)
    ang_r = row[:, None] * inv[None, :]
    ang_c = col[:, None] * inv[None, :]
    pad = LANES - rot_dim
    cos = jnp.concatenate([jnp.cos(ang_r), jnp.cos(ang_r), jnp.cos(ang_c), jnp.cos(ang_c),
                           jnp.ones((t, pad), F32)], axis=1)
    sin = jnp.concatenate([-jnp.sin(ang_r), jnp.sin(ang_r), -jnp.sin(ang_c), jnp.sin(ang_c),
                           jnp.zeros((t, pad), F32)], axis=1)
    return cos, sin


def _identity_rope(t):
    return jnp.ones((t, LANES), F32), jnp.zeros((t, LANES), F32)


def _pad_heads(w, real, padded):
    k = w.shape[0]
    w = w.reshape(k, MLA_HEADS, real)
    return jnp.pad(w, ((0, 0), (0, 0), (0, padded - real))).reshape(k, MLA_HEADS * padded)


def kernel(x, c, ctx, c_ctx, ada_w, ada_b, mix_norm_g, ffn_norm_g, even_w_in, mla_q_a_norm_g, mla_w_q_b, mla_kv_a_norm_g, mla_w_kv_b, mla_q_norm_g, mla_k_norm_g, conv_dw_w, conv_dw_b, conv_ln_g, conv_ln_b, even_w_out, dense_w_gate, dense_w_up, dense_w_down, odd_w_in, swa_q_norm_g, swa_k_norm_g, swa_sink, odd_w_out, router_w, expert_w_gate, expert_w_up, expert_w_down):
    b, s, d = x.shape
    l = ctx.shape[1]
    assert ada_w.shape[0] == 2, "two layers: an even (MLA | conv, dense) then an odd (SWA | Fourier, MoE) one"

    r = (b + 1 + 7) // 8 * 8
    cvec = jnp.zeros((r, d), F32).at[:b].set(c).at[b].set(c_ctx)
    mod = _modulation(cvec, ada_w, ada_b).reshape(2, r * 6, 1, d)
    mod0, mod1 = mod[0], mod[1]

    tm = _tile(s, 512)
    tml = _tile(b * l, 512)
    lat_row = lambda i, tile: (i * tile) // s
    ctx_row = lambda i, tile: b
    x2 = x.reshape(b * s, d)
    h2 = ctx.reshape(b * l, d)

    w_in = even_w_in[0]
    w_in0 = jnp.concatenate(
        [w_in[:, MLA_IN:], w_in[:, :MLA_IN], jnp.zeros((d, MLA_IN_PAD - MLA_IN), F32)], axis=1).astype(BF16)
    p_lat = _norm_mod_matmul(x2, mod0, lat_row, 0, mix_norm_g[0], w_in0, tm)
    p_ctx = _norm_mod_matmul(h2, mod0, ctx_row, 0, mix_norm_g[0], w_in0, tml)

    mla_scale = MLA_QK ** -0.5
    wq = _pad_heads(mla_w_q_b[0], MLA_QK, MLA_QK_PAD).astype(BF16)
    wkv = mla_w_kv_b[0].astype(BF16)
    qg = jnp.pad(mla_q_norm_g[0] * mla_scale, (0, MLA_QK_PAD - MLA_QK)).reshape(1, -1)
    kg = jnp.pad(mla_k_norm_g[0], (0, MLA_QK_PAD - MLA_QK)).reshape(1, -1)
    qag = mla_q_a_norm_g[0].reshape(1, -1)
    kvag = mla_kv_a_norm_g[0].reshape(1, -1)
    mla_col = 2 * CONV_CH // MLA_IN_PAD
    cos_m, sin_m = _rope_tables(s, MLA_ROPE)
    q_l, k_l, v_l = _mla_prep(p_lat, b, s, mla_col, cos_m, sin_m, qag, kvag, wq, wkv, qg, kg)
    cos_i, sin_i = _identity_rope(l)
    q_c, k_c, v_c = _mla_prep(p_ctx, b, l, mla_col, cos_i, sin_i, qag, kvag, wq, wkv, qg, kg)
    att_l = _mla_attn(q_l, [(k_l, v_l), (k_c, v_c)]).reshape(b * s, -1)
    att_c = _mla_attn(q_c, [(k_c, v_c)]).reshape(b * l, -1)
    conv_l = _conv_module(p_lat, b, s, conv_dw_w[0], conv_dw_b[0], conv_ln_g[0], conv_ln_b[0])
    conv_c = _conv_module(p_ctx, b, l, conv_dw_w[0], conv_dw_b[0], conv_ln_g[0], conv_ln_b[0])

    w_out = even_w_out[0].astype(BF16)
    k_att = MLA_HEADS * MLA_V
    x2 = _out_proj_residual(att_l, conv_l, w_out[:k_att], w_out[k_att:], x2, mod0, lat_row, 2, tm)
    h2 = _out_proj_residual(att_c, conv_c, w_out[:k_att], w_out[k_att:], h2, mod0, ctx_row, 2, tml)

    wg = dense_w_gate[0].astype(BF16)
    wu = dense_w_up[0].astype(BF16)
    wd = dense_w_down[0].astype(BF16)
    tf = _tile(wg.shape[1], 512)
    x2 = _dense_ffn(x2, mod0, lat_row, ffn_norm_g[0], wg, wu, wd, tm, tf)
    h2 = _dense_ffn(h2, mod0, ctx_row, ffn_norm_g[0], wg, wu, wd, tml, tf)

    w_in1 = odd_w_in[0].astype(BF16)
    p = _norm_mod_matmul(x2, mod1, lat_row, 0, mix_norm_g[1], w_in1, tm)
    w_kv_c = w_in1[:, ODD_Q:ODD_Q + 2 * ODD_KV]
    pc = _norm_mod_matmul(h2, mod1, ctx_row, 0, mix_norm_g[1], w_kv_c, tml)

    cos_s, sin_s = _rope_tables(s, HEAD_DIM)
    gains = jnp.concatenate([jnp.tile(swa_q_norm_g[0] * HEAD_DIM ** -0.5, WIN_Q_HEADS),
                             jnp.tile(swa_k_norm_g[0], WIN_KV_HEADS)]).reshape(1, -1)
    qk = _swa_prep(p, b, s, WIN_Q_HEADS + WIN_KV_HEADS, 0, cos_s, sin_s, gains)
    kc = _swa_prep(pc, b, l, WIN_KV_HEADS, 0, cos_i, sin_i, gains[:, ODD_Q:])
    att = _swa_attn(qk, p, kc, pc, swa_sink[0], b, s, l).reshape(b * s, -1)
    fcol = (ODD_Q + 2 * ODD_KV) // (FNET_GROUPS * FNET_CH)
    four = _fourier(p, b, s, fcol).reshape(b * s, -1)

    w_out1 = odd_w_out[0].astype(BF16)
    x2 = _out_proj_residual(att, four, w_out1[:ODD_Q], w_out1[ODD_Q:], x2, mod1, lat_row, 2, tm)

    x2 = _moe(x2, s, mod1, lat_row, ffn_norm_g[1], router_w[0],
              expert_w_gate, expert_w_up, expert_w_down)
    return x2.reshape(b, s, d)
```

```python
import functools

import jax
import jax.numpy as jnp
import numpy as np
from jax import lax
from jax.experimental import pallas as pl
from jax.experimental.pallas import tpu as pltpu

F32 = jnp.float32
BF16 = jnp.bfloat16
U32 = jnp.uint32

EPS = 1e-6
ROPE_THETA = 10000.0
GRID_W = 64
NEG_INF = -1e30
LANES = 128

MLA_HEADS = 8
MLA_Q_RANK = 512
MLA_KV_RANK = 256
MLA_NOPE = 128
MLA_ROPE = 64
MLA_V = 128
MLA_IN = MLA_Q_RANK + MLA_KV_RANK + MLA_ROPE
MLA_QK = MLA_NOPE + MLA_ROPE
MLA_QK_PAD = 2 * LANES
MLA_IN_PAD = 1024
CONV_CH = 1024
CONV_WIDTH = 31
CONV_HALO = 16
HEAD_DIM = 128
WIN_Q_HEADS = 12
WIN_KV_HEADS = 4
WIN_GROUP = WIN_Q_HEADS // WIN_KV_HEADS
WINDOW = 128
BLOCK = 128
FNET_GROUPS = 4
FNET_CH = 128
ODD_Q = WIN_Q_HEADS * HEAD_DIM
ODD_KV = WIN_KV_HEADS * HEAD_DIM
N_EXPERTS = 8

VMEM_LIMIT = 60 * 1024 * 1024


def _cparams(*sem):
    return pltpu.CompilerParams(dimension_semantics=sem, vmem_limit_bytes=VMEM_LIMIT)


def _tile(n, pref):
    if n <= pref:
        return n
    t = pref - pref % 8
    while n % t:
        t -= 8
    return t


def _sigmoid(x):
    return 1.0 / (1.0 + jnp.exp(-x))


def _silu(x):
    return x * _sigmoid(x)


def _rms_mod(x, g, shift, scale):
    ms = jnp.mean(x * x, axis=-1, keepdims=True)
    return (x * lax.rsqrt(ms + EPS) * g) * (1.0 + scale) + shift


NORM_ROWS = 16


def _rms_mod_rows(x_ref, rows, gain, shift):
    x = x_ref[rows, :]
    ms = jnp.mean(x * x, axis=-1, keepdims=True)
    return (x * lax.rsqrt(ms + EPS) * gain + shift).astype(BF16)


def _rms_mod_loop(x_ref, g_ref, sh_ref, sc_ref, dst_ref):
    gain = g_ref[...] * (1.0 + sc_ref[0])
    shift = sh_ref[0]

    def body(c, carry):
        rows = pl.ds(pl.multiple_of(c * NORM_ROWS, NORM_ROWS), NORM_ROWS)
        dst_ref[rows, :] = _rms_mod_rows(x_ref, rows, gain, shift)
        return carry

    lax.fori_loop(0, x_ref.shape[0] // NORM_ROWS, body, 0, unroll=8)


def _pack_pair(a, b):
    ai = lax.bitcast_convert_type(a.astype(BF16).astype(F32), U32)
    bi = lax.bitcast_convert_type(b.astype(BF16).astype(F32), U32)
    return (ai >> 16) | bi


def _unpack_pair(w):
    a = lax.bitcast_convert_type(w << 16, F32)
    b = lax.bitcast_convert_type(w & jnp.uint32(0xFFFF0000), F32)
    return a, b


def _modulation_kernel(c_ref, w_ref, b_ref, o_ref):
    a = _silu(c_ref[...]).astype(BF16)
    acc = jnp.dot(a, w_ref[...].astype(BF16), preferred_element_type=F32)
    o_ref[...] = acc + b_ref[...]


def _modulation(cvec, ada_w, ada_b):
    depth, d, n = ada_w.shape
    r = cvec.shape[0]
    tn = _tile(n, 1024)
    return pl.pallas_call(
        _modulation_kernel,
        grid=(depth, n // tn),
        in_specs=[
            pl.BlockSpec((r, d), lambda l, j: (0, 0)),
            pl.BlockSpec((None, d, tn), lambda l, j: (l, 0, j)),
            pl.BlockSpec((None, 1, tn), lambda l, j: (l, 0, j)),
        ],
        out_specs=pl.BlockSpec((None, r, tn), lambda l, j: (l, 0, j)),
        out_shape=jax.ShapeDtypeStruct((depth, r, n), F32),
        compiler_params=_cparams("parallel", "parallel"),
        name="modulation",
    )(cvec, ada_w, ada_b.reshape(depth, 1, n))


def _nmm_kernel(x_ref, sh_ref, sc_ref, g_ref, w_ref, o_ref, u0_ref, u1_ref):
    i = pl.program_id(0)

    @pl.when((i == 0) & (pl.program_id(1) == 0))
    def _():
        _rms_mod_loop(x_ref, g_ref, sh_ref, sc_ref, u0_ref)

    def step(cur_ref, nxt_ref):
        gain = g_ref[...] * (1.0 + sc_ref[0])
        shift = sh_ref[0]
        for c in range(x_ref.shape[0] // NORM_ROWS):
            rows = slice(c * NORM_ROWS, (c + 1) * NORM_ROWS)
            nxt_ref[rows, :] = _rms_mod_rows(x_ref, rows, gain, shift)
        o_ref[...] = jnp.dot(cur_ref[...], w_ref[...], preferred_element_type=F32).astype(o_ref.dtype)

    @pl.when(i % 2 == 0)
    def _():
        step(u0_ref, u1_ref)

    @pl.when(i % 2 == 1)
    def _():
        step(u1_ref, u0_ref)


def _norm_mod_matmul(x2, mod, mod_row, which, g, w, tm):
    m, d = x2.shape
    n = w.shape[1]
    tn = n // 2
    nt = m // tm
    ahead = lambda i, j: jnp.minimum(i + jnp.minimum(j, 1), nt - 1)
    return pl.pallas_call(
        _nmm_kernel,
        grid=(nt, 2),
        in_specs=[
            pl.BlockSpec((tm, d), lambda i, j: (ahead(i, j), 0)),
            pl.BlockSpec((1, 1, d), lambda i, j: (mod_row(ahead(i, j), tm) * 6 + which, 0, 0)),
            pl.BlockSpec((1, 1, d), lambda i, j: (mod_row(ahead(i, j), tm) * 6 + which + 1, 0, 0)),
            pl.BlockSpec((1, d), lambda i, j: (0, 0)),
            pl.BlockSpec((d, tn), lambda i, j: (0, j)),
        ],
        out_specs=pl.BlockSpec((tm, tn), lambda i, j: (i, j)),
        out_shape=jax.ShapeDtypeStruct((m, n), BF16),
        scratch_shapes=[pltpu.VMEM((tm, d), BF16), pltpu.VMEM((tm, d), BF16)],
        compiler_params=_cparams("arbitrary", "arbitrary"),
        name="norm_mod_matmul",
    )(x2, mod, mod, g.reshape(1, d), w)


def _split_dot(x, m01):
    hi = x.astype(BF16)
    lo = (x - hi.astype(F32)).astype(BF16)
    return jnp.dot(hi, m01, preferred_element_type=F32) + jnp.dot(lo, m01, preferred_element_type=F32)


def _rope_mxu(t, cos, sin, perm):
    return t * cos + _split_dot(t, perm) * sin


def _mla_prep_kernel(p_ref, cos_ref, sin_ref, qag_ref, kvag_ref, wq_ref, wkv_ref, qg_ref, kg_ref,
                     perm_ref, q_ref, k_ref, v_ref):
    p = p_ref[...].astype(F32)
    qa = p[:, :MLA_Q_RANK]
    kva = p[:, MLA_Q_RANK:MLA_Q_RANK + MLA_KV_RANK]
    kpe = p[:, MLA_Q_RANK + MLA_KV_RANK:MLA_Q_RANK + MLA_KV_RANK + LANES]
    qn = qa * lax.rsqrt(jnp.mean(qa * qa, axis=-1, keepdims=True) + EPS) * qag_ref[...]
    kvn = kva * lax.rsqrt(jnp.mean(kva * kva, axis=-1, keepdims=True) + EPS) * kvag_ref[...]
    q = jnp.dot(qn.astype(BF16), wq_ref[...], preferred_element_type=F32)
    kv = jnp.dot(kvn.astype(BF16), wkv_ref[...], preferred_element_type=F32)
    cos = cos_ref[...]
    sin = sin_ref[...]
    qg = qg_ref[...]
    kg = kg_ref[...]
    perm = perm_ref[...]
    kpe_ss = jnp.sum(kpe * kpe, axis=-1, keepdims=True)
    for h in range(MLA_HEADS):
        lo = h * MLA_QK_PAD
        qh = q[:, lo:lo + MLA_QK_PAD]
        rs = lax.rsqrt(jnp.sum(qh * qh, axis=-1, keepdims=True) * (1.0 / MLA_QK) + EPS)
        q_ref[0, h, :, :LANES] = (qh[:, :LANES] * rs * qg[:, :LANES]).astype(BF16)
        tail = _rope_mxu(qh[:, LANES:] * rs * qg[:, LANES:], cos, sin, perm)
        q_ref[0, h, :, LANES:] = tail.astype(BF16)
        kn = kv[:, lo:lo + MLA_NOPE]
        rs = lax.rsqrt((jnp.sum(kn * kn, axis=-1, keepdims=True) + kpe_ss) * (1.0 / MLA_QK) + EPS)
        k_ref[0, h, :, :LANES] = (kn * rs * kg[:, :LANES]).astype(BF16)
        tail = _rope_mxu(kpe * rs * kg[:, LANES:], cos, sin, perm)
        k_ref[0, h, :, LANES:] = tail.astype(BF16)
        v_ref[0, h] = kv[:, lo + MLA_NOPE:lo + MLA_NOPE + MLA_V].astype(BF16)


def _rope_perm(grp):
    lane = np.arange(LANES)
    src = np.where((lane // grp) % 2 == 0, lane + grp, lane - grp)
    perm = np.zeros((LANES, LANES), np.float32)
    perm[src, lane] = 1.0
    return jnp.asarray(perm, BF16)


def _mla_prep(p, b, t, col_block, cos, sin, qag, kvag, wq, wkv, qg, kg):
    tm = _tile(t, 256)
    nt = t // tm
    const = lambda bb, i: (0, 0)
    perm = _rope_perm(MLA_ROPE // 4)
    hs = lambda w: pl.BlockSpec((1, MLA_HEADS, tm, w), lambda bb, i: (bb, 0, i, 0))
    return pl.pallas_call(
        _mla_prep_kernel,
        grid=(b, nt),
        in_specs=[
            pl.BlockSpec((tm, MLA_IN_PAD), lambda bb, i: (bb * nt + i, col_block)),
            pl.BlockSpec((tm, LANES), lambda bb, i: (i, 0)),
            pl.BlockSpec((tm, LANES), lambda bb, i: (i, 0)),
            pl.BlockSpec((1, MLA_Q_RANK), const),
            pl.BlockSpec((1, MLA_KV_RANK), const),
            pl.BlockSpec(wq.shape, const),
            pl.BlockSpec(wkv.shape, const),
            pl.BlockSpec((1, MLA_QK_PAD), const),
            pl.BlockSpec((1, MLA_QK_PAD), const),
            pl.BlockSpec((LANES, LANES), const),
        ],
        out_specs=[hs(MLA_QK_PAD), hs(MLA_QK_PAD), hs(MLA_V)],
        out_shape=[
            jax.ShapeDtypeStruct((b, MLA_HEADS, t, MLA_QK_PAD), BF16),
            jax.ShapeDtypeStruct((b, MLA_HEADS, t, MLA_QK_PAD), BF16),
            jax.ShapeDtypeStruct((b, MLA_HEADS, t, MLA_V), BF16),
        ],
        compiler_params=_cparams("parallel", "parallel"),
        name="mla_prep",
    )(p, cos, sin, qag, kvag, wq, wkv, qg, kg, perm)


def _nt_dot(a, b):
    return lax.dot_general(a, b, (((1,), (1,)), ((), ())), preferred_element_type=F32)


MLA_HEADS_PER_STEP = 4


def _mla_attn_kernel(nseg, q_ref, *refs):
    o_ref = refs[2 * nseg]
    for hp in range(MLA_HEADS_PER_STEP):
        q = q_ref[0, hp]
        s = [_nt_dot(q, refs[2 * i][0, hp]) for i in range(nseg)]
        m = s[0].max(axis=-1, keepdims=True)
        for si in s[1:]:
            m = jnp.maximum(m, si.max(axis=-1, keepdims=True))
        den = 0.0
        acc = 0.0
        for i in range(nseg):
            e = jnp.exp(s[i] - m)
            den = den + jnp.sum(e, axis=-1, keepdims=True)
            acc = acc + jnp.dot(e.astype(BF16), refs[2 * i + 1][0, hp], preferred_element_type=F32)
        o_ref[0, :, hp * MLA_V:(hp + 1) * MLA_V] = (acc / den).astype(o_ref.dtype)


def _mla_attn(q, kvs):
    b, h, t, _ = q.shape
    tq = _tile(t, 256)
    hp = MLA_HEADS_PER_STEP
    in_specs = [pl.BlockSpec((1, hp, tq, MLA_QK_PAD), lambda bb, hh, i: (bb, hh, i, 0))]
    args = [q]
    for k, v in kvs:
        n = k.shape[2]
        in_specs.append(pl.BlockSpec((1, hp, n, MLA_QK_PAD), lambda bb, hh, i: (bb, hh, 0, 0)))
        in_specs.append(pl.BlockSpec((1, hp, n, MLA_V), lambda bb, hh, i: (bb, hh, 0, 0)))
        args += [k, v]
    return pl.pallas_call(
        functools.partial(_mla_attn_kernel, len(kvs)),
        grid=(b, h // hp, t // tq),
        in_specs=in_specs,
        out_specs=pl.BlockSpec((1, tq, hp * MLA_V), lambda bb, hh, i: (bb, i, hh)),
        out_shape=jax.ShapeDtypeStruct((b, t, h * MLA_V), BF16),
        compiler_params=_cparams("parallel", "parallel", "parallel"),
        name="mla_attn",
    )(*args)


CONV_ROWS = 64
CONV_WIN = CONV_ROWS + 2 * CONV_HALO


def _conv_kernel(t, p_ref, w_ref, b_ref, g_ref, beta_ref, o_ref, hp_ref, cv_ref):
    zeros = jnp.zeros((CONV_HALO, CONV_CH), F32)
    hp_ref[0:CONV_HALO, :] = zeros
    hp_ref[CONV_HALO + t:CONV_HALO + t + CONV_HALO, :] = zeros

    def glu(i, c):
        r0 = pl.multiple_of(i * CONV_ROWS, CONV_ROWS)
        a = p_ref[pl.ds(r0, CONV_ROWS), :CONV_CH].astype(F32)
        gate = p_ref[pl.ds(r0, CONV_ROWS), CONV_CH:].astype(F32)
        hp_ref[pl.ds(r0 + CONV_HALO, CONV_ROWS), :] = a * _sigmoid(gate)
        return c

    lax.fori_loop(0, t // CONV_ROWS, glu, 0)

    def tile(i, c):
        r0 = pl.multiple_of(i * CONV_ROWS, CONV_ROWS)
        for cc in range(CONV_CH // LANES):
            cs = slice(cc * LANES, (cc + 1) * LANES)
            win = hp_ref[pl.ds(r0, CONV_WIN), cs]
            acc = jnp.zeros((CONV_ROWS, LANES), F32) + b_ref[:, cs]
            for r in range(8):
                rolled = win if r == 0 else pltpu.roll(win, CONV_WIN - r, axis=0)
                for k in range(CONV_WIDTH):
                    off = k + CONV_HALO - CONV_WIDTH // 2
                    if off % 8 == r:
                        acc = acc + rolled[off - r:off - r + CONV_ROWS] * w_ref[k:k + 1, cs]
            cv_ref[:, cs] = acc
        h = cv_ref[...]
        mu = jnp.mean(h, axis=-1, keepdims=True)
        hc = h - mu
        y = hc * lax.rsqrt(jnp.mean(hc * hc, axis=-1, keepdims=True) + EPS) * g_ref[...] + beta_ref[...]
        o_ref[pl.ds(r0, CONV_ROWS), :] = _silu(y).astype(o_ref.dtype)
        return c

    lax.fori_loop(0, t // CONV_ROWS, tile, 0)


def _conv_module(p, b, t, dw_w, dw_b, ln_g, ln_b):
    const = lambda bb: (0, 0)
    return pl.pallas_call(
        functools.partial(_conv_kernel, t),
        grid=(b,),
        in_specs=[
            pl.BlockSpec((t, 2 * CONV_CH), lambda bb: (bb, 0)),
            pl.BlockSpec((CONV_WIDTH, CONV_CH), const),
            pl.BlockSpec((1, CONV_CH), const),
            pl.BlockSpec((1, CONV_CH), const),
            pl.BlockSpec((1, CONV_CH), const),
        ],
        out_specs=pl.BlockSpec((t, CONV_CH), lambda bb: (bb, 0)),
        out_shape=jax.ShapeDtypeStruct((b * t, CONV_CH), BF16),
        scratch_shapes=[pltpu.VMEM((t + 2 * CONV_HALO, CONV_CH), F32), pltpu.VMEM((CONV_ROWS, CONV_CH), F32)],
        compiler_params=_cparams("parallel"),
        name="conv_module",
    )(p, dw_w, dw_b.reshape(1, -1), ln_g.reshape(1, -1), ln_b.reshape(1, -1))


def _oproj_kernel(a1_ref, a2_ref, w1_ref, w2_ref, x_ref, gate_ref, o_ref):
    acc = jnp.dot(a1_ref[...], w1_ref[...], preferred_element_type=F32)
    acc = acc + jnp.dot(a2_ref[...], w2_ref[...], preferred_element_type=F32)
    o_ref[...] = x_ref[...] + gate_ref[0] * acc


def _out_proj_residual(a1, a2, w1, w2, x2, mod, mod_row, which, tm):
    m, d = x2.shape
    k1, k2 = w1.shape[0], w2.shape[0]
    const = lambda i: (0, 0)
    return pl.pallas_call(
        _oproj_kernel,
        grid=(m // tm,),
        in_specs=[
            pl.BlockSpec((tm, k1), lambda i: (i, 0)),
            pl.BlockSpec((tm, k2), lambda i: (i, 0)),
            pl.BlockSpec((k1, d), const),
            pl.BlockSpec((k2, d), const),
            pl.BlockSpec((tm, d), lambda i: (i, 0)),
            pl.BlockSpec((1, 1, d), lambda i: (mod_row(i, tm) * 6 + which, 0, 0)),
        ],
        out_specs=pl.BlockSpec((tm, d), lambda i: (i, 0)),
        out_shape=jax.ShapeDtypeStruct((m, d), F32),
        compiler_params=_cparams("parallel"),
        name="out_proj_residual",
    )(a1, a2, w1, w2, x2, mod)


FFN_SUB = 256


def _ffn_kernel(x_ref, sh_ref, sc_ref, gate_ref, g_ref, wg_ref, wu_ref, wd_ref, o_ref, u_ref):
    f = pl.program_id(1)
    acc_ref = o_ref

    @pl.when(f == 0)
    def _():
        _rms_mod_loop(x_ref, g_ref, sh_ref, sc_ref, u_ref)
        acc_ref[...] = jnp.zeros_like(acc_ref)

    tm = u_ref.shape[0]
    sub = _tile(tm, FFN_SUB)
    halves = [slice(r, r + sub) for r in range(0, tm, sub)]
    hid = []
    for rows in halves:
        u = u_ref[rows]
        hg = jnp.dot(u, wg_ref[...], preferred_element_type=F32)
        hu = jnp.dot(u, wu_ref[...], preferred_element_type=F32)
        hid.append((_silu(hg) * hu).astype(BF16))
    for rows, h in zip(halves, hid):
        acc_ref[rows] += jnp.dot(h, wd_ref[...], preferred_element_type=F32)

    @pl.when(f == pl.num_programs(1) - 1)
    def _():
        o_ref[...] = x_ref[...] + gate_ref[0] * acc_ref[...]


def _dense_ffn(x2, mod, mod_row, g, wg, wu, wd, tm, tf):
    m, d = x2.shape
    ff = wg.shape[1]
    row = lambda w: pl.BlockSpec((1, 1, d), lambda i, f: (mod_row(i, tm) * 6 + w, 0, 0))
    return pl.pallas_call(
        _ffn_kernel,
        grid=(m // tm, ff // tf),
        in_specs=[
            pl.BlockSpec((tm, d), lambda i, f: (i, 0)),
            row(3), row(4), row(5),
            pl.BlockSpec((1, d), lambda i, f: (0, 0)),
            pl.BlockSpec((d, tf), lambda i, f: (0, f)),
            pl.BlockSpec((d, tf), lambda i, f: (0, f)),
            pl.BlockSpec((tf, d), lambda i, f: (f, 0)),
        ],
        out_specs=pl.BlockSpec((tm, d), lambda i, f: (i, 0)),
        out_shape=jax.ShapeDtypeStruct((m, d), F32),
        scratch_shapes=[pltpu.VMEM((tm, d), BF16)],
        compiler_params=_cparams("parallel", "arbitrary"),
        name="dense_ffn",
    )(x2, mod, mod, mod, g.reshape(1, d), wg, wu, wd)


def _swa_prep_kernel(nheads, p_ref, cos_ref, sin_ref, g_ref, perm_ref, ones_ref, o_ref):
    cos = cos_ref[...]
    sin = sin_ref[...]
    perm = perm_ref[...]
    ones = ones_ref[...]
    for h in range(nheads):
        cs = slice(h * HEAD_DIM, (h + 1) * HEAD_DIM)
        t = p_ref[:, cs].astype(F32)
        ms = _split_dot(t * t, ones) * (1.0 / HEAD_DIM)
        t = t * lax.rsqrt(ms + EPS) * g_ref[:, cs]
        o_ref[:, cs] = _rope_mxu(t, cos, sin, perm).astype(BF16)


def _swa_prep(p, b, t, nheads, col_block, cos, sin, gains):
    tm = _tile(t, 256)
    nt = t // tm
    w = nheads * HEAD_DIM
    return pl.pallas_call(
        functools.partial(_swa_prep_kernel, nheads),
        grid=(b, nt),
        in_specs=[
            pl.BlockSpec((tm, w), lambda bb, i: (bb * nt + i, col_block)),
            pl.BlockSpec((tm, LANES), lambda bb, i: (i, 0)),
            pl.BlockSpec((tm, LANES), lambda bb, i: (i, 0)),
            pl.BlockSpec((1, w), lambda bb, i: (0, 0)),
            pl.BlockSpec((LANES, LANES), lambda bb, i: (0, 0)),
            pl.BlockSpec((LANES, LANES), lambda bb, i: (0, 0)),
        ],
        out_specs=pl.BlockSpec((tm, w), lambda bb, i: (bb * nt + i, 0)),
        out_shape=jax.ShapeDtypeStruct((b * t, w), BF16),
        compiler_params=_cparams("parallel", "parallel"),
        name="swa_prep",
    )(p, cos, sin, gains, _rope_perm(HEAD_DIM // 4), jnp.ones((LANES, LANES), BF16))


def _swa_attn_kernel(s_len, sink_ref, q_ref, k_ref, v_ref, kc_ref, vc_ref, o_ref):
    blk = pl.program_id(1)
    span = BLOCK + 2 * WINDOW
    start = blk * BLOCK
    ws = pl.multiple_of(jnp.clip(start - WINDOW, 0, s_len - span), BLOCK)
    rows = WIN_GROUP * BLOCK
    row = lax.broadcasted_iota(jnp.int32, (rows, span), 0)
    col = lax.broadcasted_iota(jnp.int32, (rows, span), 1)
    in_window = jnp.abs((start + row % BLOCK) - (ws + col)) <= WINDOW
    rcol = lax.broadcasted_iota(jnp.int32, (rows, 1), 0) // BLOCK
    for n in range(WIN_KV_HEADS):
        hs = slice(n * HEAD_DIM, (n + 1) * HEAD_DIM)
        kw = k_ref[0, pl.ds(ws, span), hs]
        vw = v_ref[0, pl.ds(ws, span), hs]
        q0 = n * WIN_GROUP * HEAD_DIM
        q = jnp.concatenate(
            [q_ref[0, :, q0 + g * HEAD_DIM:q0 + (g + 1) * HEAD_DIM] for g in range(WIN_GROUP)], axis=0)
        s_w = jnp.where(in_window, _nt_dot(q, kw), NEG_INF)
        s_c = _nt_dot(q, kc_ref[0, :, hs])
        sink = jnp.zeros((rows, 1), F32)
        for g in range(WIN_GROUP):
            sink = jnp.where(rcol == g, sink_ref[n * WIN_GROUP + g], sink)
        m = jnp.maximum(jnp.maximum(s_w.max(axis=-1, keepdims=True), s_c.max(axis=-1, keepdims=True)), sink)
        e_w = jnp.exp(s_w - m)
        e_c = jnp.exp(s_c - m)
        den = jnp.sum(e_w, axis=-1, keepdims=True) + jnp.sum(e_c, axis=-1, keepdims=True) + jnp.exp(sink - m)
        acc = jnp.dot(e_w.astype(BF16), vw, preferred_element_type=F32)
        acc = acc + jnp.dot(e_c.astype(BF16), vc_ref[0, :, hs], preferred_element_type=F32)
        out = acc / den
        for g in range(WIN_GROUP):
            o_ref[0, :, q0 + g * HEAD_DIM:q0 + (g + 1) * HEAD_DIM] = (
                out[g * BLOCK:(g + 1) * BLOCK].astype(o_ref.dtype))


def _swa_attn(qk, p, kc, pc, sink, b, s_len, ctx_len):
    qk3 = qk.reshape(b, s_len, -1)
    p3 = p.reshape(b, s_len, -1)
    kc3 = kc.reshape(b, ctx_len, -1)
    pc3 = pc.reshape(b, ctx_len, -1)
    return pl.pallas_call(
        functools.partial(_swa_attn_kernel, s_len),
        grid=(b, s_len // BLOCK),
        in_specs=[
            pl.BlockSpec(memory_space=pltpu.SMEM),
            pl.BlockSpec((1, BLOCK, ODD_Q), lambda bb, i: (bb, i, 0)),
            pl.BlockSpec((1, s_len, ODD_KV), lambda bb, i: (bb, 0, ODD_Q // ODD_KV)),
            pl.BlockSpec((1, s_len, ODD_KV), lambda bb, i: (bb, 0, (ODD_Q + ODD_KV) // ODD_KV)),
            pl.BlockSpec((1, ctx_len, ODD_KV), lambda bb, i: (bb, 0, 0)),
            pl.BlockSpec((1, ctx_len, ODD_KV), lambda bb, i: (bb, 0, 1)),
        ],
        out_specs=pl.BlockSpec((1, BLOCK, ODD_Q), lambda bb, i: (bb, i, 0)),
        out_shape=jax.ShapeDtypeStruct((b, s_len, ODD_Q), BF16),
        compiler_params=_cparams("parallel", "parallel"),
        name="swa_attn",
    )(sink, qk3, qk3, p3, kc3, pc3)


def _fourier_kernel(scale, f_ref, cs_ref, ct_ref, st_ref, o_ref, xc_ref, xs_ref):
    @pl.when(pl.program_id(1) == 0)
    def _():
        for g in range(FNET_GROUPS):
            cs = slice(g * FNET_CH, (g + 1) * FNET_CH)
            r = jnp.dot(f_ref[0, :, cs], cs_ref[...], preferred_element_type=F32)
            xc_ref[:, cs] = r[:, :FNET_CH].astype(BF16)
            xs_ref[:, cs] = r[:, FNET_CH:].astype(BF16)

    y = jnp.dot(ct_ref[...], xc_ref[...], preferred_element_type=F32)
    y = y - jnp.dot(st_ref[...], xs_ref[...], preferred_element_type=F32)
    o_ref[0] = (y * scale).astype(o_ref.dtype)


def _dft_tables(n):
    k = jnp.arange(n, dtype=jnp.int32)
    ang = ((k[:, None] * k[None, :]) % n).astype(F32) * (2.0 * np.pi / n)
    return jnp.cos(ang), jnp.sin(ang)


def _fourier(p, b, t, col_block):
    w = FNET_GROUPS * FNET_CH
    p3 = p.reshape(b, t, -1)
    cc, sc = _dft_tables(FNET_CH)
    ct, st = _dft_tables(t)
    cs = jnp.concatenate([cc, sc], axis=1).astype(BF16)
    tk = _tile(t, 512)
    return pl.pallas_call(
        functools.partial(_fourier_kernel, float((t * FNET_CH) ** -0.5)),
        grid=(b, t // tk),
        in_specs=[
            pl.BlockSpec((1, t, w), lambda bb, i: (bb, 0, col_block)),
            pl.BlockSpec((FNET_CH, 2 * FNET_CH), lambda bb, i: (0, 0)),
            pl.BlockSpec((tk, t), lambda bb, i: (i, 0)),
            pl.BlockSpec((tk, t), lambda bb, i: (i, 0)),
        ],
        out_specs=pl.BlockSpec((1, tk, w), lambda bb, i: (bb, i, 0)),
        out_shape=jax.ShapeDtypeStruct((b, t, w), BF16),
        scratch_shapes=[pltpu.VMEM((t, w), BF16), pltpu.VMEM((t, w), BF16)],
        compiler_params=_cparams("parallel", "arbitrary"),
        name="fourier",
    )(p3, cs, ct.astype(BF16), st.astype(BF16))


def _router_kernel(x_ref, sh_ref, sc_ref, g_ref, wh_ref, wl_ref, fin_ref, info_ref, cnt_ref, run_ref):
    @pl.when(pl.program_id(0) == 0)
    def _():
        run_ref[...] = jnp.zeros_like(run_ref)

    u = _rms_mod(x_ref[...], g_ref[...], sh_ref[0], sc_ref[0])
    d = u.shape[1]
    fin_ref[...] = _pack_pair(u[:, :d // 2], u[:, d // 2:])
    u_hi = u.astype(BF16)
    u_lo = (u - u_hi.astype(F32)).astype(BF16)
    logits = (jnp.dot(u_hi, wh_ref[...], preferred_element_type=F32)
              + (jnp.dot(u_lo, wh_ref[...], preferred_element_type=F32)
                 + jnp.dot(u_hi, wl_ref[...], preferred_element_type=F32)))
    tr = logits.shape[0]
    lane = lax.broadcasted_iota(jnp.int32, (tr, LANES), 1).astype(F32)
    logits = jnp.where(lane < N_EXPERTS, logits, -jnp.inf)
    m1 = logits.max(axis=-1, keepdims=True)
    i1 = jnp.where(logits == m1, lane, float(LANES)).min(axis=-1, keepdims=True)
    rest = jnp.where(lane == i1, -jnp.inf, logits)
    m2 = rest.max(axis=-1, keepdims=True)
    i2 = jnp.where(rest == m2, lane, float(LANES)).min(axis=-1, keepdims=True)
    e21 = jnp.exp(m2 - m1)
    g1 = 1.0 / (1.0 + e21)
    g2 = e21 / (1.0 + e21)
    oh1 = lane == i1
    oh2 = lane == i2
    oh = (oh1 | oh2).astype(F32)
    r = lax.broadcasted_iota(jnp.int32, (tr, tr), 0)
    c = lax.broadcasted_iota(jnp.int32, (tr, tr), 1)
    before = (r > c).astype(BF16)
    prefix = jnp.dot(before, oh.astype(BF16), preferred_element_type=F32) + run_ref[...]
    r1 = jnp.sum(jnp.where(oh1, prefix, 0.0), axis=-1, keepdims=True)
    r2 = jnp.sum(jnp.where(oh2, prefix, 0.0), axis=-1, keepdims=True)
    run = run_ref[...] + jnp.sum(oh, axis=0, keepdims=True)
    run_ref[...] = run
    cnt_ref[...] = run
    info = jnp.zeros((tr, LANES), F32)
    for j, val in enumerate((i1, i2, r1, r2, g1, g2)):
        info = jnp.where(lane == j, val, info)
    info_ref[...] = info[:, :8]


def _router(x2, mod, mod_row, g, wr_hi, wr_lo, tr):
    m, d = x2.shape
    row = lambda w: pl.BlockSpec((1, 1, d), lambda i: (mod_row(i, tr) * 6 + w, 0, 0))
    const = lambda i: (0, 0)
    return pl.pallas_call(
        _router_kernel,
        grid=(m // tr,),
        in_specs=[
            pl.BlockSpec((tr, d), lambda i: (i, 0)),
            row(3), row(4),
            pl.BlockSpec((1, d), const),
            pl.BlockSpec((d, LANES), const),
            pl.BlockSpec((d, LANES), const),
        ],
        out_specs=[
            pl.BlockSpec((tr, d // 2), lambda i: (i, 0)),
            pl.BlockSpec((tr, 8), lambda i: (i, 0)),
            pl.BlockSpec((1, LANES), const),
        ],
        out_shape=[
            jax.ShapeDtypeStruct((m, d // 2), U32),
            jax.ShapeDtypeStruct((m, 8), F32),
            jax.ShapeDtypeStruct((1, LANES), F32),
        ],
        scratch_shapes=[pltpu.VMEM((1, LANES), F32)],
        compiler_params=_cparams("arbitrary"),
        name="moe_router",
    )(x2, mod, mod, g.reshape(1, d), wr_hi, wr_lo)


def _row_copy(src, src_row, dst, dst_row, sem):
    return pltpu.make_async_copy(src.at[pl.ds(src_row, 1)], dst.at[pl.ds(dst_row, 1)], sem)


def _dispatch_kernel(tg, pos_ref, fin_ref, init_ref, xs_ref, sem):
    del init_ref

    def issue(t, c):
        _row_copy(fin_ref, t, xs_ref, pos_ref[0, 2 * t], sem).start()
        _row_copy(fin_ref, t, xs_ref, pos_ref[0, 2 * t + 1], sem).start()
        return c

    lax.fori_loop(0, tg, issue, 0, unroll=8)
    for _ in range(2):
        pltpu.make_async_copy(fin_ref, xs_ref.at[pl.ds(0, tg)], sem).wait()


def _dispatch(fin, pos, cap, tg):
    m, w = fin.shape
    return pl.pallas_call(
        functools.partial(_dispatch_kernel, tg),
        grid=(m // tg,),
        in_specs=[
            pl.BlockSpec((None, 1, 2 * tg), lambda i: (i, 0, 0), memory_space=pltpu.SMEM),
            pl.BlockSpec((tg, w), lambda i: (i, 0)),
            pl.BlockSpec(memory_space=pl.ANY),
        ],
        out_specs=pl.BlockSpec(memory_space=pl.ANY),
        out_shape=jax.ShapeDtypeStruct((cap, w), U32),
        scratch_shapes=[pltpu.SemaphoreType.DMA(())],
        input_output_aliases={2: 0},
        compiler_params=_cparams("arbitrary"),
        name="moe_dispatch",
    )(pos.reshape(m // tg, 1, 2 * tg), fin, jnp.zeros((cap, w), U32))


def _moe_ffn_kernel(sub, te_ref, valid_ref, xs_ref, wg_ref, wu_ref, wd_ref, ys_ref, u_ref, acc_ref):
    i = pl.program_id(0)
    f = pl.program_id(1)
    tm, d = u_ref.shape
    valid = valid_ref[i]

    @pl.when(f == 0)
    def _():
        a, b = _unpack_pair(xs_ref[...])
        u_ref[:, :d // 2] = a.astype(BF16)
        u_ref[:, d // 2:] = b.astype(BF16)
        acc_ref[...] = jnp.zeros_like(acc_ref)

    def swiglu(nsub):
        wg = wg_ref[...].astype(BF16)
        wu = wu_ref[...].astype(BF16)
        wd = wd_ref[...].astype(BF16)
        hid = []
        for sb in range(nsub):
            u = u_ref[sb * sub:(sb + 1) * sub]
            hg = jnp.dot(u, wg, preferred_element_type=F32)
            hu = jnp.dot(u, wu, preferred_element_type=F32)
            hid.append((_silu(hg) * hu).astype(BF16))
        for sb in range(nsub):
            acc_ref[sb * sub:(sb + 1) * sub] += jnp.dot(hid[sb], wd, preferred_element_type=F32)

    nsub = tm // sub
    for n_valid in range(1, nsub + 1):
        hi = n_valid * sub if n_valid < nsub else tm
        @pl.when((valid > (n_valid - 1) * sub) & (valid <= hi))
        def _(n_valid=n_valid):
            swiglu(n_valid)

    @pl.when(f == pl.num_programs(1) - 1)
    def _():
        ys_ref[...] = _pack_pair(acc_ref[:, :d // 2], acc_ref[:, d // 2:])


def _moe_ffn(xs, tile_expert, tile_valid, wg, wu, wd, tm, tf, sub):
    cap, w = xs.shape
    d = 2 * w
    ff = wg.shape[3]
    nf = ff // tf

    def f_eff(i, f, valid):
        return jnp.where(valid[i] > 0, f, nf - 1)

    grid_spec = pltpu.PrefetchScalarGridSpec(
        num_scalar_prefetch=2,
        grid=(cap // tm, nf),
        in_specs=[
            pl.BlockSpec((tm, w), lambda i, f, te, valid: (i, 0)),
            pl.BlockSpec((None, None, d, tf), lambda i, f, te, valid: (0, te[i], 0, f_eff(i, f, valid))),
            pl.BlockSpec((None, None, d, tf), lambda i, f, te, valid: (0, te[i], 0, f_eff(i, f, valid))),
            pl.BlockSpec((None, None, tf, d), lambda i, f, te, valid: (0, te[i], f_eff(i, f, valid), 0)),
        ],
        out_specs=pl.BlockSpec((tm, w), lambda i, f, te, valid: (i, 0)),
        scratch_shapes=[pltpu.VMEM((tm, d), BF16), pltpu.VMEM((tm, d), F32)],
    )
    return pl.pallas_call(
        functools.partial(_moe_ffn_kernel, sub),
        grid_spec=grid_spec,
        out_shape=jax.ShapeDtypeStruct((cap, w), U32),
        compiler_params=_cparams("parallel", "arbitrary"),
        name="moe_ffn",
    )(tile_expert, tile_valid, xs, wg, wu, wd)


def _combine_kernel(tc, pos_ref, posn_ref, x_ref, info_ref, gate_ref, ys_ref, o_ref, buf_ref, sem):
    i = pl.program_id(0)
    slot = i % 2

    def issue(p_ref, s):
        def body(t, c):
            _row_copy(ys_ref, p_ref[0, 2 * t], buf_ref.at[s, 0], t, sem.at[s]).start()
            _row_copy(ys_ref, p_ref[0, 2 * t + 1], buf_ref.at[s, 1], t, sem.at[s]).start()
            return c

        lax.fori_loop(0, tc, body, 0, unroll=8)

    @pl.when(i == 0)
    def _():
        issue(pos_ref, 0)

    @pl.when(i + 1 < pl.num_programs(0))
    def _():
        issue(posn_ref, 1 - slot)

    for k in range(2):
        pltpu.make_async_copy(ys_ref.at[pl.ds(0, tc)], buf_ref.at[slot, k], sem.at[slot]).wait()
    w = buf_ref.shape[3]
    g1 = info_ref[:, 4:5]
    g2 = info_ref[:, 5:6]
    a1, b1 = _unpack_pair(buf_ref[slot, 0])
    a2, b2 = _unpack_pair(buf_ref[slot, 1])
    gate = gate_ref[0]
    o_ref[:, :w] = x_ref[:, :w] + gate[:, :w] * (g1 * a1 + g2 * a2)
    o_ref[:, w:] = x_ref[:, w:] + gate[:, w:] * (g1 * b1 + g2 * b2)


def _combine(x2, info, pos, ys, mod, mod_row, tc):
    m, d = x2.shape
    w = ys.shape[1]
    n = m // tc
    pos3 = pos.reshape(n, 1, 2 * tc)
    return pl.pallas_call(
        functools.partial(_combine_kernel, tc),
        grid=(n,),
        in_specs=[
            pl.BlockSpec((None, 1, 2 * tc), lambda i: (i, 0, 0), memory_space=pltpu.SMEM),
            pl.BlockSpec((None, 1, 2 * tc), lambda i: (jnp.minimum(i + 1, n - 1), 0, 0), memory_space=pltpu.SMEM),
            pl.BlockSpec((tc, d), lambda i: (i, 0)),
            pl.BlockSpec((tc, 8), lambda i: (i, 0)),
            pl.BlockSpec((1, 1, d), lambda i: (mod_row(i, tc) * 6 + 5, 0, 0)),
            pl.BlockSpec(memory_space=pl.ANY),
        ],
        out_specs=pl.BlockSpec((tc, d), lambda i: (i, 0)),
        out_shape=jax.ShapeDtypeStruct((m, d), F32),
        scratch_shapes=[pltpu.VMEM((2, 2, tc, w), U32), pltpu.SemaphoreType.DMA((2,))],
        compiler_params=_cparams("arbitrary"),
        name="moe_combine",
    )(pos3, pos3, x2, info, mod, ys)


MOE_SUB = 256


def _moe(x2, s, mod, mod_row, g, router_w, wg, wu, wd):
    m, d = x2.shape
    tm = _tile(m, 1024)
    sub = _tile(tm, MOE_SUB)
    tf = _tile(wg.shape[3], 512)
    tr = _tile(s, 512)
    tg = _tile(s, 256)
    wr = jnp.zeros((d, LANES), F32).at[:, :N_EXPERTS].set(router_w)
    wr_hi = wr.astype(BF16)
    wr_lo = (wr - wr_hi.astype(F32)).astype(BF16)
    fin, info, cnt = _router(x2, mod, mod_row, g, wr_hi, wr_lo, tr)

    counts = cnt[0, :N_EXPERTS].astype(jnp.int32)
    padded = (counts + tm - 1) // tm * tm
    ends = jnp.cumsum(padded)
    starts = ends - padded
    experts = info[:, 0:2].astype(jnp.int32)
    pos = starts[experts] + info[:, 2:4].astype(jnp.int32)
    ntiles = (2 * m) // tm + N_EXPERTS
    cap = ntiles * tm
    tile_start = jnp.arange(ntiles, dtype=jnp.int32) * tm
    tile_expert = jnp.sum(tile_start[:, None] >= ends[None, :], axis=1).astype(jnp.int32)
    active = tile_start < ends[-1]
    last_expert = tile_expert[ends[-1] // tm - 1]
    tile_expert = jnp.where(active, tile_expert, last_expert)
    group_end = (starts + counts)[tile_expert]
    tile_valid = jnp.where(active, jnp.clip(group_end - tile_start, 0, tm), 0).astype(jnp.int32)

    xs = _dispatch(fin, pos, cap, tg)
    ys = _moe_ffn(xs, tile_expert, tile_valid, wg, wu, wd, tm, tf, sub)
    return _combine(x2, info, pos, ys, mod, mod_row, tg)


def _rope_tables(t, rot_dim):
    rows = t // GRID_W
    row = jnp.repeat(jnp.arange(rows, dtype=F32), GRID_W)
    col = jnp.tile(jnp.arange(GRID_W, dtype=F32), rows)
    half = rot_dim // 2
    inv = ROPE_THETA ** (-jnp.arange(0, half, 2, dtype=F32) / half)
    ang_r = row[:, None] * inv[None, :]
    ang_c = col[:, None] * inv[None, :]
    pad = LANES - rot_dim
    cos = jnp.concatenate([jnp.cos(ang_r), jnp.cos(ang_r), jnp.cos(ang_c), jnp.cos(ang_c),
                           jnp.ones((t, pad), F32)], axis=1)
    sin = jnp.concatenate([-jnp.sin(ang_r), jnp.sin(ang_r), -jnp.sin(ang_c), jnp.sin(ang_c),
                           jnp.zeros((t, pad), F32)], axis=1)
    return cos, sin


def _identity_rope(t):
    return jnp.ones((t, LANES), F32), jnp.zeros((t, LANES), F32)


def _pad_heads(w, real, padded):
    k = w.shape[0]
    w = w.reshape(k, MLA_HEADS, real)
    return jnp.pad(w, ((0, 0), (0, 0), (0, padded - real))).reshape(k, MLA_HEADS * padded)


def kernel(x, c, ctx, c_ctx, ada_w, ada_b, mix_norm_g, ffn_norm_g, even_w_in, mla_q_a_norm_g, mla_w_q_b, mla_kv_a_norm_g, mla_w_kv_b, mla_q_norm_g, mla_k_norm_g, conv_dw_w, conv_dw_b, conv_ln_g, conv_ln_b, even_w_out, dense_w_gate, dense_w_up, dense_w_down, odd_w_in, swa_q_norm_g, swa_k_norm_g, swa_sink, odd_w_out, router_w, expert_w_gate, expert_w_up, expert_w_down):
    b, s, d = x.shape
    l = ctx.shape[1]
    assert ada_w.shape[0] == 2, "two layers: an even (MLA | conv, dense) then an odd (SWA | Fourier, MoE) one"

    r = (b + 1 + 7) // 8 * 8
    cvec = jnp.zeros((r, d), F32).at[:b].set(c).at[b].set(c_ctx)
    mod = _modulation(cvec, ada_w, ada_b).reshape(2, r * 6, 1, d)
    mod0, mod1 = mod[0], mod[1]

    tm = _tile(s, 512)
    tml = _tile(b * l, 512)
    lat_row = lambda i, tile: (i * tile) // s
    ctx_row = lambda i, tile: b
    x2 = x.reshape(b * s, d)
    h2 = ctx.reshape(b * l, d)

    w_in = even_w_in[0]
    w_in0 = jnp.concatenate(
        [w_in[:, MLA_IN:], w_in[:, :MLA_IN], jnp.zeros((d, MLA_IN_PAD - MLA_IN), F32)], axis=1).astype(BF16)
    p_lat = _norm_mod_matmul(x2, mod0, lat_row, 0, mix_norm_g[0], w_in0, tm)
    p_ctx = _norm_mod_matmul(h2, mod0, ctx_row, 0, mix_norm_g[0], w_in0, tml)

    mla_scale = MLA_QK ** -0.5
    wq = _pad_heads(mla_w_q_b[0], MLA_QK, MLA_QK_PAD).astype(BF16)
    wkv = mla_w_kv_b[0].astype(BF16)
    qg = jnp.pad(mla_q_norm_g[0] * mla_scale, (0, MLA_QK_PAD - MLA_QK)).reshape(1, -1)
    kg = jnp.pad(mla_k_norm_g[0], (0, MLA_QK_PAD - MLA_QK)).reshape(1, -1)
    qag = mla_q_a_norm_g[0].reshape(1, -1)
    kvag = mla_kv_a_norm_g[0].reshape(1, -1)
    mla_col = 2 * CONV_CH // MLA_IN_PAD
    cos_m, sin_m = _rope_tables(s, MLA_ROPE)
    q_l, k_l, v_l = _mla_prep(p_lat, b, s, mla_col, cos_m, sin_m, qag, kvag, wq, wkv, qg, kg)
    cos_i, sin_i = _identity_rope(l)
    q_c, k_c, v_c = _mla_prep(p_ctx, b, l, mla_col, cos_i, sin_i, qag, kvag, wq, wkv, qg, kg)
    att_l = _mla_attn(q_l, [(k_l, v_l), (k_c, v_c)]).reshape(b * s, -1)
    att_c = _mla_attn(q_c, [(k_c, v_c)]).reshape(b * l, -1)
    conv_l = _conv_module(p_lat, b, s, conv_dw_w[0], conv_dw_b[0], conv_ln_g[0], conv_ln_b[0])
    conv_c = _conv_module(p_ctx, b, l, conv_dw_w[0], conv_dw_b[0], conv_ln_g[0], conv_ln_b[0])

    w_out = even_w_out[0].astype(BF16)
    k_att = MLA_HEADS * MLA_V
    x2 = _out_proj_residual(att_l, conv_l, w_out[:k_att], w_out[k_att:], x2, mod0, lat_row, 2, tm)
    h2 = _out_proj_residual(att_c, conv_c, w_out[:k_att], w_out[k_att:], h2, mod0, ctx_row, 2, tml)

    wg = dense_w_gate[0].astype(BF16)
    wu = dense_w_up[0].astype(BF16)
    wd = dense_w_down[0].astype(BF16)
    tf = _tile(wg.shape[1], 512)
    x2 = _dense_ffn(x2, mod0, lat_row, ffn_norm_g[0], wg, wu, wd, _tile(s, 1024), tf)
    h2 = _dense_ffn(h2, mod0, ctx_row, ffn_norm_g[0], wg, wu, wd, _tile(b * l, 1024), tf)

    w_in1 = odd_w_in[0].astype(BF16)
    p = _norm_mod_matmul(x2, mod1, lat_row, 0, mix_norm_g[1], w_in1, tm)
    w_kv_c = w_in1[:, ODD_Q:ODD_Q + 2 * ODD_KV]
    pc = _norm_mod_matmul(h2, mod1, ctx_row, 0, mix_norm_g[1], w_kv_c, tml)

    cos_s, sin_s = _rope_tables(s, HEAD_DIM)
    gains = jnp.concatenate([jnp.tile(swa_q_norm_g[0] * HEAD_DIM ** -0.5, WIN_Q_HEADS),
                             jnp.tile(swa_k_norm_g[0], WIN_KV_HEADS)]).reshape(1, -1)
    qk = _swa_prep(p, b, s, WIN_Q_HEADS + WIN_KV_HEADS, 0, cos_s, sin_s, gains)
    kc = _swa_prep(pc, b, l, WIN_KV_HEADS, 0, cos_i, sin_i, gains[:, ODD_Q:])
    att = _swa_attn(qk, p, kc, pc, swa_sink[0], b, s, l).reshape(b * s, -1)
    fcol = (ODD_Q + 2 * ODD_KV) // (FNET_GROUPS * FNET_CH)
    four = _fourier(p, b, s, fcol).reshape(b * s, -1)

    w_out1 = odd_w_out[0].astype(BF16)
    x2 = _out_proj_residual(att, four, w_out1[:ODD_Q], w_out1[ODD_Q:], x2, mod1, lat_row, 2, tm)

    x2 = _moe(x2, s, mod1, lat_row, ffn_norm_g[1], router_w[0],
              expert_w_gate, expert_w_up, expert_w_down)
    return x2.reshape(b, s, d)
```

```python
import functools

import jax
import jax.numpy as jnp
import numpy as np
from jax import lax
from jax.experimental import pallas as pl
from jax.experimental.pallas import tpu as pltpu

F32 = jnp.float32
BF16 = jnp.bfloat16
U32 = jnp.uint32

EPS = 1e-6
ROPE_THETA = 10000.0
GRID_W = 64
NEG_INF = -1e30
LANES = 128

MLA_HEADS = 8
MLA_Q_RANK = 512
MLA_KV_RANK = 256
MLA_NOPE = 128
MLA_ROPE = 64
MLA_V = 128
MLA_IN = MLA_Q_RANK + MLA_KV_RANK + MLA_ROPE
MLA_QK = MLA_NOPE + MLA_ROPE
MLA_QK_PAD = 2 * LANES
MLA_IN_PAD = 1024
CONV_CH = 1024
CONV_WIDTH = 31
CONV_HALO = 16
HEAD_DIM = 128
WIN_Q_HEADS = 12
WIN_KV_HEADS = 4
WIN_GROUP = WIN_Q_HEADS // WIN_KV_HEADS
WINDOW = 128
BLOCK = 128
FNET_GROUPS = 4
FNET_CH = 128
ODD_Q = WIN_Q_HEADS * HEAD_DIM
ODD_KV = WIN_KV_HEADS * HEAD_DIM
N_EXPERTS = 8

VMEM_LIMIT = 60 * 1024 * 1024


def _cparams(*sem):
    return pltpu.CompilerParams(dimension_semantics=sem, vmem_limit_bytes=VMEM_LIMIT)


def _tile(n, pref):
    if n <= pref:
        return n
    t = pref - pref % 8
    while n % t:
        t -= 8
    return t


def _sigmoid(x):
    return 1.0 / (1.0 + jnp.exp(-x))


def _silu(x):
    return x * _sigmoid(x)


def _rms_mod(x, g, shift, scale):
    ms = jnp.mean(x * x, axis=-1, keepdims=True)
    return (x * lax.rsqrt(ms + EPS) * g) * (1.0 + scale) + shift


NORM_ROWS = 16


def _rms_mod_rows(x_ref, rows, gain, shift):
    x = x_ref[rows, :]
    ms = jnp.mean(x * x, axis=-1, keepdims=True)
    return (x * lax.rsqrt(ms + EPS) * gain + shift).astype(BF16)


def _rms_mod_loop(x_ref, g_ref, sh_ref, sc_ref, dst_ref):
    gain = g_ref[...] * (1.0 + sc_ref[0])
    shift = sh_ref[0]

    def body(c, carry):
        rows = pl.ds(pl.multiple_of(c * NORM_ROWS, NORM_ROWS), NORM_ROWS)
        dst_ref[rows, :] = _rms_mod_rows(x_ref, rows, gain, shift)
        return carry

    lax.fori_loop(0, x_ref.shape[0] // NORM_ROWS, body, 0, unroll=8)


def _pack_pair(a, b):
    ai = lax.bitcast_convert_type(a.astype(BF16).astype(F32), U32)
    bi = lax.bitcast_convert_type(b.astype(BF16).astype(F32), U32)
    return (ai >> 16) | bi


def _unpack_pair(w):
    a = lax.bitcast_convert_type(w << 16, F32)
    b = lax.bitcast_convert_type(w & jnp.uint32(0xFFFF0000), F32)
    return a, b


def _modulation_kernel(c_ref, w_ref, b_ref, o_ref):
    a = _silu(c_ref[...]).astype(BF16)
    acc = jnp.dot(a, w_ref[...].astype(BF16), preferred_element_type=F32)
    o_ref[...] = acc + b_ref[...]


def _modulation(cvec, ada_w, ada_b):
    depth, d, n = ada_w.shape
    r = cvec.shape[0]
    tn = _tile(n, 1024)
    return pl.pallas_call(
        _modulation_kernel,
        grid=(depth, n // tn),
        in_specs=[
            pl.BlockSpec((r, d), lambda l, j: (0, 0)),
            pl.BlockSpec((None, d, tn), lambda l, j: (l, 0, j)),
            pl.BlockSpec((None, 1, tn), lambda l, j: (l, 0, j)),
        ],
        out_specs=pl.BlockSpec((None, r, tn), lambda l, j: (l, 0, j)),
        out_shape=jax.ShapeDtypeStruct((depth, r, n), F32),
        compiler_params=_cparams("parallel", "parallel"),
        name="modulation",
    )(cvec, ada_w, ada_b.reshape(depth, 1, n))


def _nmm_kernel(x_ref, sh_ref, sc_ref, g_ref, w_ref, o_ref, u_ref):
    tm = x_ref.shape[0]
    sub = _tile(tm, FFN_SUB)

    @pl.when(pl.program_id(1) == 0)
    def _():
        gain = g_ref[...] * (1.0 + sc_ref[0])
        shift = sh_ref[0]
        for r0 in range(0, tm, sub):
            for c0 in range(r0, r0 + sub, NORM_ROWS):
                chunk = slice(c0, c0 + NORM_ROWS)
                u_ref[chunk, :] = _rms_mod_rows(x_ref, chunk, gain, shift)
            rows = slice(r0, r0 + sub)
            o_ref[rows, :] = jnp.dot(u_ref[rows, :], w_ref[...],
                                     preferred_element_type=F32).astype(o_ref.dtype)

    @pl.when(pl.program_id(1) != 0)
    def _():
        o_ref[...] = jnp.dot(u_ref[...], w_ref[...], preferred_element_type=F32).astype(o_ref.dtype)


def _norm_mod_matmul(x2, mod, mod_row, which, g, w, tm):
    m, d = x2.shape
    n = w.shape[1]
    tn = n // 2
    return pl.pallas_call(
        _nmm_kernel,
        grid=(m // tm, 2),
        in_specs=[
            pl.BlockSpec((tm, d), lambda i, j: (i, 0)),
            pl.BlockSpec((1, 1, d), lambda i, j: (mod_row(i, tm) * 6 + which, 0, 0)),
            pl.BlockSpec((1, 1, d), lambda i, j: (mod_row(i, tm) * 6 + which + 1, 0, 0)),
            pl.BlockSpec((1, d), lambda i, j: (0, 0)),
            pl.BlockSpec((d, tn), lambda i, j: (0, j)),
        ],
        out_specs=pl.BlockSpec((tm, tn), lambda i, j: (i, j)),
        out_shape=jax.ShapeDtypeStruct((m, n), BF16),
        scratch_shapes=[pltpu.VMEM((tm, d), BF16)],
        compiler_params=_cparams("parallel", "arbitrary"),
        name="norm_mod_matmul",
    )(x2, mod, mod, g.reshape(1, d), w)


def _split_dot(x, m01):
    hi = x.astype(BF16)
    lo = (x - hi.astype(F32)).astype(BF16)
    return jnp.dot(hi, m01, preferred_element_type=F32) + jnp.dot(lo, m01, preferred_element_type=F32)


def _rope_mxu(t, cos, sin, perm):
    return t * cos + _split_dot(t, perm) * sin


def _mla_prep_kernel(p_ref, cos_ref, sin_ref, qag_ref, kvag_ref, wq_ref, wkv_ref, qg_ref, kg_ref,
                     perm_ref, q_ref, k_ref, v_ref):
    p = p_ref[...].astype(F32)
    qa = p[:, :MLA_Q_RANK]
    kva = p[:, MLA_Q_RANK:MLA_Q_RANK + MLA_KV_RANK]
    kpe = p[:, MLA_Q_RANK + MLA_KV_RANK:MLA_Q_RANK + MLA_KV_RANK + LANES]
    qn = qa * lax.rsqrt(jnp.mean(qa * qa, axis=-1, keepdims=True) + EPS) * qag_ref[...]
    kvn = kva * lax.rsqrt(jnp.mean(kva * kva, axis=-1, keepdims=True) + EPS) * kvag_ref[...]
    q = jnp.dot(qn.astype(BF16), wq_ref[...], preferred_element_type=F32)
    kv = jnp.dot(kvn.astype(BF16), wkv_ref[...], preferred_element_type=F32)
    cos = cos_ref[...]
    sin = sin_ref[...]
    qg = qg_ref[...]
    kg = kg_ref[...]
    perm = perm_ref[...]
    kpe_ss = jnp.sum(kpe * kpe, axis=-1, keepdims=True)
    for h in range(MLA_HEADS):
        lo = h * MLA_QK_PAD
        qh = q[:, lo:lo + MLA_QK_PAD]
        rs = lax.rsqrt(jnp.sum(qh * qh, axis=-1, keepdims=True) * (1.0 / MLA_QK) + EPS)
        q_ref[0, h, :, :LANES] = (qh[:, :LANES] * rs * qg[:, :LANES]).astype(BF16)
        tail = _rope_mxu(qh[:, LANES:] * rs * qg[:, LANES:], cos, sin, perm)
        q_ref[0, h, :, LANES:] = tail.astype(BF16)
        kn = kv[:, lo:lo + MLA_NOPE]
        rs = lax.rsqrt((jnp.sum(kn * kn, axis=-1, keepdims=True) + kpe_ss) * (1.0 / MLA_QK) + EPS)
        k_ref[0, h, :, :LANES] = (kn * rs * kg[:, :LANES]).astype(BF16)
        tail = _rope_mxu(kpe * rs * kg[:, LANES:], cos, sin, perm)
        k_ref[0, h, :, LANES:] = tail.astype(BF16)
        v_ref[0, h] = kv[:, lo + MLA_NOPE:lo + MLA_NOPE + MLA_V].astype(BF16)


def _rope_perm(grp):
    lane = np.arange(LANES)
    src = np.where((lane // grp) % 2 == 0, lane + grp, lane - grp)
    perm = np.zeros((LANES, LANES), np.float32)
    perm[src, lane] = 1.0
    return jnp.asarray(perm, BF16)


def _mla_prep(p, b, t, col_block, cos, sin, qag, kvag, wq, wkv, qg, kg):
    tm = _tile(t, 256)
    nt = t // tm
    const = lambda bb, i: (0, 0)
    perm = _rope_perm(MLA_ROPE // 4)
    hs = lambda w: pl.BlockSpec((1, MLA_HEADS, tm, w), lambda bb, i: (bb, 0, i, 0))
    return pl.pallas_call(
        _mla_prep_kernel,
        grid=(b, nt),
        in_specs=[
            pl.BlockSpec((tm, MLA_IN_PAD), lambda bb, i: (bb * nt + i, col_block)),
            pl.BlockSpec((tm, LANES), lambda bb, i: (i, 0)),
            pl.BlockSpec((tm, LANES), lambda bb, i: (i, 0)),
            pl.BlockSpec((1, MLA_Q_RANK), const),
            pl.BlockSpec((1, MLA_KV_RANK), const),
            pl.BlockSpec(wq.shape, const),
            pl.BlockSpec(wkv.shape, const),
            pl.BlockSpec((1, MLA_QK_PAD), const),
            pl.BlockSpec((1, MLA_QK_PAD), const),
            pl.BlockSpec((LANES, LANES), const),
        ],
        out_specs=[hs(MLA_QK_PAD), hs(MLA_QK_PAD), hs(MLA_V)],
        out_shape=[
            jax.ShapeDtypeStruct((b, MLA_HEADS, t, MLA_QK_PAD), BF16),
            jax.ShapeDtypeStruct((b, MLA_HEADS, t, MLA_QK_PAD), BF16),
            jax.ShapeDtypeStruct((b, MLA_HEADS, t, MLA_V), BF16),
        ],
        compiler_params=_cparams("parallel", "parallel"),
        name="mla_prep",
    )(p, cos, sin, qag, kvag, wq, wkv, qg, kg, perm)


def _nt_dot(a, b):
    return lax.dot_general(a, b, (((1,), (1,)), ((), ())), preferred_element_type=F32)


MLA_HEADS_PER_STEP = 4


def _mla_attn_kernel(nseg, q_ref, *refs):
    o_ref = refs[2 * nseg]
    for hp in range(MLA_HEADS_PER_STEP):
        q = q_ref[0, hp]
        s = [_nt_dot(q, refs[2 * i][0, hp]) for i in range(nseg)]
        m = s[0].max(axis=-1, keepdims=True)
        for si in s[1:]:
            m = jnp.maximum(m, si.max(axis=-1, keepdims=True))
        den = 0.0
        acc = 0.0
        for i in range(nseg):
            e = jnp.exp(s[i] - m)
            den = den + jnp.sum(e, axis=-1, keepdims=True)
            acc = acc + jnp.dot(e.astype(BF16), refs[2 * i + 1][0, hp], preferred_element_type=F32)
        o_ref[0, :, hp * MLA_V:(hp + 1) * MLA_V] = (acc / den).astype(o_ref.dtype)


def _mla_attn(q, kvs):
    b, h, t, _ = q.shape
    tq = _tile(t, 256)
    hp = MLA_HEADS_PER_STEP
    in_specs = [pl.BlockSpec((1, hp, tq, MLA_QK_PAD), lambda bb, hh, i: (bb, hh, i, 0))]
    args = [q]
    for k, v in kvs:
        n = k.shape[2]
        in_specs.append(pl.BlockSpec((1, hp, n, MLA_QK_PAD), lambda bb, hh, i: (bb, hh, 0, 0)))
        in_specs.append(pl.BlockSpec((1, hp, n, MLA_V), lambda bb, hh, i: (bb, hh, 0, 0)))
        args += [k, v]
    return pl.pallas_call(
        functools.partial(_mla_attn_kernel, len(kvs)),
        grid=(b, h // hp, t // tq),
        in_specs=in_specs,
        out_specs=pl.BlockSpec((1, tq, hp * MLA_V), lambda bb, hh, i: (bb, i, hh)),
        out_shape=jax.ShapeDtypeStruct((b, t, h * MLA_V), BF16),
        compiler_params=_cparams("parallel", "parallel", "parallel"),
        name="mla_attn",
    )(*args)


CONV_ROWS = 64
CONV_WIN = CONV_ROWS + 2 * CONV_HALO


def _conv_kernel(t, p_ref, w_ref, b_ref, g_ref, beta_ref, o_ref, hp_ref, cv_ref):
    zeros = jnp.zeros((CONV_HALO, CONV_CH), F32)
    hp_ref[0:CONV_HALO, :] = zeros
    hp_ref[CONV_HALO + t:CONV_HALO + t + CONV_HALO, :] = zeros

    def glu(i, c):
        r0 = pl.multiple_of(i * CONV_ROWS, CONV_ROWS)
        a = p_ref[pl.ds(r0, CONV_ROWS), :CONV_CH].astype(F32)
        gate = p_ref[pl.ds(r0, CONV_ROWS), CONV_CH:].astype(F32)
        hp_ref[pl.ds(r0 + CONV_HALO, CONV_ROWS), :] = a * _sigmoid(gate)
        return c

    lax.fori_loop(0, t // CONV_ROWS, glu, 0)

    def tile(i, c):
        r0 = pl.multiple_of(i * CONV_ROWS, CONV_ROWS)
        for cc in range(CONV_CH // LANES):
            cs = slice(cc * LANES, (cc + 1) * LANES)
            win = hp_ref[pl.ds(r0, CONV_WIN), cs]
            acc = jnp.zeros((CONV_ROWS, LANES), F32) + b_ref[:, cs]
            for r in range(8):
                rolled = win if r == 0 else pltpu.roll(win, CONV_WIN - r, axis=0)
                for k in range(CONV_WIDTH):
                    off = k + CONV_HALO - CONV_WIDTH // 2
                    if off % 8 == r:
                        acc = acc + rolled[off - r:off - r + CONV_ROWS] * w_ref[k:k + 1, cs]
            cv_ref[:, cs] = acc
        h = cv_ref[...]
        mu = jnp.mean(h, axis=-1, keepdims=True)
        hc = h - mu
        y = hc * lax.rsqrt(jnp.mean(hc * hc, axis=-1, keepdims=True) + EPS) * g_ref[...] + beta_ref[...]
        o_ref[pl.ds(r0, CONV_ROWS), :] = _silu(y).astype(o_ref.dtype)
        return c

    lax.fori_loop(0, t // CONV_ROWS, tile, 0)


def _conv_module(p, b, t, dw_w, dw_b, ln_g, ln_b):
    const = lambda bb: (0, 0)
    return pl.pallas_call(
        functools.partial(_conv_kernel, t),
        grid=(b,),
        in_specs=[
            pl.BlockSpec((t, 2 * CONV_CH), lambda bb: (bb, 0)),
            pl.BlockSpec((CONV_WIDTH, CONV_CH), const),
            pl.BlockSpec((1, CONV_CH), const),
            pl.BlockSpec((1, CONV_CH), const),
            pl.BlockSpec((1, CONV_CH), const),
        ],
        out_specs=pl.BlockSpec((t, CONV_CH), lambda bb: (bb, 0)),
        out_shape=jax.ShapeDtypeStruct((b * t, CONV_CH), BF16),
        scratch_shapes=[pltpu.VMEM((t + 2 * CONV_HALO, CONV_CH), F32), pltpu.VMEM((CONV_ROWS, CONV_CH), F32)],
        compiler_params=_cparams("parallel"),
        name="conv_module",
    )(p, dw_w, dw_b.reshape(1, -1), ln_g.reshape(1, -1), ln_b.reshape(1, -1))


def _oproj_kernel(a1_ref, a2_ref, w1_ref, w2_ref, x_ref, gate_ref, o_ref):
    acc = jnp.dot(a1_ref[...], w1_ref[...], preferred_element_type=F32)
    acc = acc + jnp.dot(a2_ref[...], w2_ref[...], preferred_element_type=F32)
    o_ref[...] = x_ref[...] + gate_ref[0] * acc


def _out_proj_residual(a1, a2, w1, w2, x2, mod, mod_row, which, tm):
    m, d = x2.shape
    k1, k2 = w1.shape[0], w2.shape[0]
    const = lambda i: (0, 0)
    return pl.pallas_call(
        _oproj_kernel,
        grid=(m // tm,),
        in_specs=[
            pl.BlockSpec((tm, k1), lambda i: (i, 0)),
            pl.BlockSpec((tm, k2), lambda i: (i, 0)),
            pl.BlockSpec((k1, d), const),
            pl.BlockSpec((k2, d), const),
            pl.BlockSpec((tm, d), lambda i: (i, 0)),
            pl.BlockSpec((1, 1, d), lambda i: (mod_row(i, tm) * 6 + which, 0, 0)),
        ],
        out_specs=pl.BlockSpec((tm, d), lambda i: (i, 0)),
        out_shape=jax.ShapeDtypeStruct((m, d), F32),
        compiler_params=_cparams("parallel"),
        name="out_proj_residual",
    )(a1, a2, w1, w2, x2, mod)


FFN_SUB = 256


def _ffn_kernel(x_ref, sh_ref, sc_ref, gate_ref, g_ref, wg_ref, wu_ref, wd_ref, o_ref, u_ref):
    f = pl.program_id(1)
    acc_ref = o_ref
    tm = u_ref.shape[0]
    sub = _tile(tm, FFN_SUB)

    def swiglu_step(first):
        if first:
            gain = g_ref[...] * (1.0 + sc_ref[0])
            shift = sh_ref[0]
        hid = []
        for r0 in range(0, tm, sub):
            if first:
                for c0 in range(r0, r0 + sub, NORM_ROWS):
                    chunk = slice(c0, c0 + NORM_ROWS)
                    u_ref[chunk, :] = _rms_mod_rows(x_ref, chunk, gain, shift)
            u = u_ref[r0:r0 + sub]
            hg = jnp.dot(u, wg_ref[...], preferred_element_type=F32)
            hu = jnp.dot(u, wu_ref[...], preferred_element_type=F32)
            hid.append((_silu(hg) * hu).astype(BF16))
        for k, r0 in enumerate(range(0, tm, sub)):
            down = jnp.dot(hid[k], wd_ref[...], preferred_element_type=F32)
            if first:
                acc_ref[r0:r0 + sub] = down
            else:
                acc_ref[r0:r0 + sub] += down

    @pl.when(f == 0)
    def _():
        swiglu_step(True)

    @pl.when(f != 0)
    def _():
        swiglu_step(False)

    @pl.when(f == pl.num_programs(1) - 1)
    def _():
        o_ref[...] = x_ref[...] + gate_ref[0] * acc_ref[...]


def _dense_ffn(x2, mod, mod_row, g, wg, wu, wd, tm, tf):
    m, d = x2.shape
    ff = wg.shape[1]
    row = lambda w: pl.BlockSpec((1, 1, d), lambda i, f: (mod_row(i, tm) * 6 + w, 0, 0))
    return pl.pallas_call(
        _ffn_kernel,
        grid=(m // tm, ff // tf),
        in_specs=[
            pl.BlockSpec((tm, d), lambda i, f: (i, 0)),
            row(3), row(4), row(5),
            pl.BlockSpec((1, d), lambda i, f: (0, 0)),
            pl.BlockSpec((d, tf), lambda i, f: (0, f)),
            pl.BlockSpec((d, tf), lambda i, f: (0, f)),
            pl.BlockSpec((tf, d), lambda i, f: (f, 0)),
        ],
        out_specs=pl.BlockSpec((tm, d), lambda i, f: (i, 0)),
        out_shape=jax.ShapeDtypeStruct((m, d), F32),
        scratch_shapes=[pltpu.VMEM((tm, d), BF16)],
        compiler_params=_cparams("parallel", "arbitrary"),
        name="dense_ffn",
    )(x2, mod, mod, mod, g.reshape(1, d), wg, wu, wd)


def _swa_prep_kernel(nheads, p_ref, cos_ref, sin_ref, g_ref, perm_ref, ones_ref, o_ref):
    cos = cos_ref[...]
    sin = sin_ref[...]
    perm = perm_ref[...]
    ones = ones_ref[...]
    for h in range(nheads):
        cs = slice(h * HEAD_DIM, (h + 1) * HEAD_DIM)
        t = p_ref[:, cs].astype(F32)
        ms = _split_dot(t * t, ones) * (1.0 / HEAD_DIM)
        t = t * lax.rsqrt(ms + EPS) * g_ref[:, cs]
        o_ref[:, cs] = _rope_mxu(t, cos, sin, perm).astype(BF16)


def _swa_prep(p, b, t, nheads, col_block, cos, sin, gains):
    tm = _tile(t, 256)
    nt = t // tm
    w = nheads * HEAD_DIM
    return pl.pallas_call(
        functools.partial(_swa_prep_kernel, nheads),
        grid=(b, nt),
        in_specs=[
            pl.BlockSpec((tm, w), lambda bb, i: (bb * nt + i, col_block)),
            pl.BlockSpec((tm, LANES), lambda bb, i: (i, 0)),
            pl.BlockSpec((tm, LANES), lambda bb, i: (i, 0)),
            pl.BlockSpec((1, w), lambda bb, i: (0, 0)),
            pl.BlockSpec((LANES, LANES), lambda bb, i: (0, 0)),
            pl.BlockSpec((LANES, LANES), lambda bb, i: (0, 0)),
        ],
        out_specs=pl.BlockSpec((tm, w), lambda bb, i: (bb * nt + i, 0)),
        out_shape=jax.ShapeDtypeStruct((b * t, w), BF16),
        compiler_params=_cparams("parallel", "parallel"),
        name="swa_prep",
    )(p, cos, sin, gains, _rope_perm(HEAD_DIM // 4), jnp.ones((LANES, LANES), BF16))


def _swa_attn_kernel(s_len, sink_ref, q_ref, k_ref, v_ref, kc_ref, vc_ref, o_ref):
    blk = pl.program_id(1)
    span = BLOCK + 2 * WINDOW
    start = blk * BLOCK
    ws = pl.multiple_of(jnp.clip(start - WINDOW, 0, s_len - span), BLOCK)
    rows = WIN_GROUP * BLOCK
    row = lax.broadcasted_iota(jnp.int32, (rows, span), 0)
    col = lax.broadcasted_iota(jnp.int32, (rows, span), 1)
    in_window = jnp.abs((start + row % BLOCK) - (ws + col)) <= WINDOW
    rcol = lax.broadcasted_iota(jnp.int32, (rows, 1), 0) // BLOCK
    for n in range(WIN_KV_HEADS):
        hs = slice(n * HEAD_DIM, (n + 1) * HEAD_DIM)
        kw = k_ref[0, pl.ds(ws, span), hs]
        vw = v_ref[0, pl.ds(ws, span), hs]
        q0 = n * WIN_GROUP * HEAD_DIM
        q = jnp.concatenate(
            [q_ref[0, :, q0 + g * HEAD_DIM:q0 + (g + 1) * HEAD_DIM] for g in range(WIN_GROUP)], axis=0)
        s_w = jnp.where(in_window, _nt_dot(q, kw), NEG_INF)
        s_c = _nt_dot(q, kc_ref[0, :, hs])
        sink = jnp.zeros((rows, 1), F32)
        for g in range(WIN_GROUP):
            sink = jnp.where(rcol == g, sink_ref[n * WIN_GROUP + g], sink)
        m = jnp.maximum(jnp.maximum(s_w.max(axis=-1, keepdims=True), s_c.max(axis=-1, keepdims=True)), sink)
        e_w = jnp.exp(s_w - m)
        e_c = jnp.exp(s_c - m)
        den = jnp.sum(e_w, axis=-1, keepdims=True) + jnp.sum(e_c, axis=-1, keepdims=True) + jnp.exp(sink - m)
        acc = jnp.dot(e_w.astype(BF16), vw, preferred_element_type=F32)
        acc = acc + jnp.dot(e_c.astype(BF16), vc_ref[0, :, hs], preferred_element_type=F32)
        out = acc / den
        for g in range(WIN_GROUP):
            o_ref[0, :, q0 + g * HEAD_DIM:q0 + (g + 1) * HEAD_DIM] = (
                out[g * BLOCK:(g + 1) * BLOCK].astype(o_ref.dtype))


def _swa_attn(qk, p, kc, pc, sink, b, s_len, ctx_len):
    qk3 = qk.reshape(b, s_len, -1)
    p3 = p.reshape(b, s_len, -1)
    kc3 = kc.reshape(b, ctx_len, -1)
    pc3 = pc.reshape(b, ctx_len, -1)
    return pl.pallas_call(
        functools.partial(_swa_attn_kernel, s_len),
        grid=(b, s_len // BLOCK),
        in_specs=[
            pl.BlockSpec(memory_space=pltpu.SMEM),
            pl.BlockSpec((1, BLOCK, ODD_Q), lambda bb, i: (bb, i, 0)),
            pl.BlockSpec((1, s_len, ODD_KV), lambda bb, i: (bb, 0, ODD_Q // ODD_KV)),
            pl.BlockSpec((1, s_len, ODD_KV), lambda bb, i: (bb, 0, (ODD_Q + ODD_KV) // ODD_KV)),
            pl.BlockSpec((1, ctx_len, ODD_KV), lambda bb, i: (bb, 0, 0)),
            pl.BlockSpec((1, ctx_len, ODD_KV), lambda bb, i: (bb, 0, 1)),
        ],
        out_specs=pl.BlockSpec((1, BLOCK, ODD_Q), lambda bb, i: (bb, i, 0)),
        out_shape=jax.ShapeDtypeStruct((b, s_len, ODD_Q), BF16),
        compiler_params=_cparams("parallel", "parallel"),
        name="swa_attn",
    )(sink, qk3, qk3, p3, kc3, pc3)


def _fourier_kernel(scale, f_ref, cs_ref, ct_ref, st_ref, o_ref, xc_ref, xs_ref):
    @pl.when(pl.program_id(1) == 0)
    def _():
        for g in range(FNET_GROUPS):
            cs = slice(g * FNET_CH, (g + 1) * FNET_CH)
            r = jnp.dot(f_ref[0, :, cs], cs_ref[...], preferred_element_type=F32)
            xc_ref[:, cs] = r[:, :FNET_CH].astype(BF16)
            xs_ref[:, cs] = r[:, FNET_CH:].astype(BF16)

    y = jnp.dot(ct_ref[...], xc_ref[...], preferred_element_type=F32)
    y = y - jnp.dot(st_ref[...], xs_ref[...], preferred_element_type=F32)
    o_ref[0] = (y * scale).astype(o_ref.dtype)


def _dft_tables(n):
    k = jnp.arange(n, dtype=jnp.int32)
    ang = ((k[:, None] * k[None, :]) % n).astype(F32) * (2.0 * np.pi / n)
    return jnp.cos(ang), jnp.sin(ang)


def _fourier(p, b, t, col_block):
    w = FNET_GROUPS * FNET_CH
    p3 = p.reshape(b, t, -1)
    cc, sc = _dft_tables(FNET_CH)
    ct, st = _dft_tables(t)
    cs = jnp.concatenate([cc, sc], axis=1).astype(BF16)
    tk = _tile(t, 512)
    return pl.pallas_call(
        functools.partial(_fourier_kernel, float((t * FNET_CH) ** -0.5)),
        grid=(b, t // tk),
        in_specs=[
            pl.BlockSpec((1, t, w), lambda bb, i: (bb, 0, col_block)),
            pl.BlockSpec((FNET_CH, 2 * FNET_CH), lambda bb, i: (0, 0)),
            pl.BlockSpec((tk, t), lambda bb, i: (i, 0)),
            pl.BlockSpec((tk, t), lambda bb, i: (i, 0)),
        ],
        out_specs=pl.BlockSpec((1, tk, w), lambda bb, i: (bb, i, 0)),
        out_shape=jax.ShapeDtypeStruct((b, t, w), BF16),
        scratch_shapes=[pltpu.VMEM((t, w), BF16), pltpu.VMEM((t, w), BF16)],
        compiler_params=_cparams("parallel", "arbitrary"),
        name="fourier",
    )(p3, cs, ct.astype(BF16), st.astype(BF16))


def _router_kernel(x_ref, sh_ref, sc_ref, g_ref, wh_ref, wl_ref, fin_ref, info_ref, cnt_ref, run_ref):
    @pl.when(pl.program_id(0) == 0)
    def _():
        run_ref[...] = jnp.zeros_like(run_ref)

    u = _rms_mod(x_ref[...], g_ref[...], sh_ref[0], sc_ref[0])
    d = u.shape[1]
    fin_ref[...] = _pack_pair(u[:, :d // 2], u[:, d // 2:])
    u_hi = u.astype(BF16)
    u_lo = (u - u_hi.astype(F32)).astype(BF16)
    logits = (jnp.dot(u_hi, wh_ref[...], preferred_element_type=F32)
              + (jnp.dot(u_lo, wh_ref[...], preferred_element_type=F32)
                 + jnp.dot(u_hi, wl_ref[...], preferred_element_type=F32)))
    tr = logits.shape[0]
    lane = lax.broadcasted_iota(jnp.int32, (tr, LANES), 1).astype(F32)
    logits = jnp.where(lane < N_EXPERTS, logits, -jnp.inf)
    m1 = logits.max(axis=-1, keepdims=True)
    i1 = jnp.where(logits == m1, lane, float(LANES)).min(axis=-1, keepdims=True)
    rest = jnp.where(lane == i1, -jnp.inf, logits)
    m2 = rest.max(axis=-1, keepdims=True)
    i2 = jnp.where(rest == m2, lane, float(LANES)).min(axis=-1, keepdims=True)
    e21 = jnp.exp(m2 - m1)
    g1 = 1.0 / (1.0 + e21)
    g2 = e21 / (1.0 + e21)
    oh1 = lane == i1
    oh2 = lane == i2
    oh = (oh1 | oh2).astype(F32)
    r = lax.broadcasted_iota(jnp.int32, (tr, tr), 0)
    c = lax.broadcasted_iota(jnp.int32, (tr, tr), 1)
    before = (r > c).astype(BF16)
    prefix = jnp.dot(before, oh.astype(BF16), preferred_element_type=F32) + run_ref[...]
    r1 = jnp.sum(jnp.where(oh1, prefix, 0.0), axis=-1, keepdims=True)
    r2 = jnp.sum(jnp.where(oh2, prefix, 0.0), axis=-1, keepdims=True)
    run = run_ref[...] + jnp.sum(oh, axis=0, keepdims=True)
    run_ref[...] = run
    cnt_ref[...] = run
    info = jnp.zeros((tr, LANES), F32)
    for j, val in enumerate((i1, i2, r1, r2, g1, g2)):
        info = jnp.where(lane == j, val, info)
    info_ref[...] = info[:, :8]


def _router(x2, mod, mod_row, g, wr_hi, wr_lo, tr):
    m, d = x2.shape
    row = lambda w: pl.BlockSpec((1, 1, d), lambda i: (mod_row(i, tr) * 6 + w, 0, 0))
    const = lambda i: (0, 0)
    return pl.pallas_call(
        _router_kernel,
        grid=(m // tr,),
        in_specs=[
            pl.BlockSpec((tr, d), lambda i: (i, 0)),
            row(3), row(4),
            pl.BlockSpec((1, d), const),
            pl.BlockSpec((d, LANES), const),
            pl.BlockSpec((d, LANES), const),
        ],
        out_specs=[
            pl.BlockSpec((tr, d // 2), lambda i: (i, 0)),
            pl.BlockSpec((tr, 8), lambda i: (i, 0)),
            pl.BlockSpec((1, LANES), const),
        ],
        out_shape=[
            jax.ShapeDtypeStruct((m, d // 2), U32),
            jax.ShapeDtypeStruct((m, 8), F32),
            jax.ShapeDtypeStruct((1, LANES), F32),
        ],
        scratch_shapes=[pltpu.VMEM((1, LANES), F32)],
        compiler_params=_cparams("arbitrary"),
        name="moe_router",
    )(x2, mod, mod, g.reshape(1, d), wr_hi, wr_lo)


def _row_copy(src, src_row, dst, dst_row, sem):
    return pltpu.make_async_copy(src.at[pl.ds(src_row, 1)], dst.at[pl.ds(dst_row, 1)], sem)


def _dispatch_kernel(tg, pos_ref, fin_ref, init_ref, xs_ref, sem):
    del init_ref

    def issue(t, c):
        _row_copy(fin_ref, t, xs_ref, pos_ref[0, 2 * t], sem).start()
        _row_copy(fin_ref, t, xs_ref, pos_ref[0, 2 * t + 1], sem).start()
        return c

    lax.fori_loop(0, tg, issue, 0, unroll=8)
    for _ in range(2):
        pltpu.make_async_copy(fin_ref, xs_ref.at[pl.ds(0, tg)], sem).wait()


def _dispatch(fin, pos, cap, tg):
    m, w = fin.shape
    return pl.pallas_call(
        functools.partial(_dispatch_kernel, tg),
        grid=(m // tg,),
        in_specs=[
            pl.BlockSpec((None, 1, 2 * tg), lambda i: (i, 0, 0), memory_space=pltpu.SMEM),
            pl.BlockSpec((tg, w), lambda i: (i, 0)),
            pl.BlockSpec(memory_space=pl.ANY),
        ],
        out_specs=pl.BlockSpec(memory_space=pl.ANY),
        out_shape=jax.ShapeDtypeStruct((cap, w), U32),
        scratch_shapes=[pltpu.SemaphoreType.DMA(())],
        input_output_aliases={2: 0},
        compiler_params=_cparams("arbitrary"),
        name="moe_dispatch",
    )(pos.reshape(m // tg, 1, 2 * tg), fin, jnp.zeros((cap, w), U32))


def _moe_ffn_kernel(sub, te_ref, valid_ref, xs_ref, wg_ref, wu_ref, wd_ref, ys_ref, u_ref, acc_ref):
    i = pl.program_id(0)
    f = pl.program_id(1)
    tm, d = u_ref.shape
    valid = valid_ref[i]

    @pl.when(f == 0)
    def _():
        a, b = _unpack_pair(xs_ref[...])
        u_ref[:, :d // 2] = a.astype(BF16)
        u_ref[:, d // 2:] = b.astype(BF16)
        acc_ref[...] = jnp.zeros_like(acc_ref)

    def swiglu(nsub):
        wg = wg_ref[...].astype(BF16)
        wu = wu_ref[...].astype(BF16)
        wd = wd_ref[...].astype(BF16)
        hid = []
        for sb in range(nsub):
            u = u_ref[sb * sub:(sb + 1) * sub]
            hg = jnp.dot(u, wg, preferred_element_type=F32)
            hu = jnp.dot(u, wu, preferred_element_type=F32)
            hid.append((_silu(hg) * hu).astype(BF16))
        for sb in range(nsub):
            acc_ref[sb * sub:(sb + 1) * sub] += jnp.dot(hid[sb], wd, preferred_element_type=F32)

    nsub = tm // sub
    for n_valid in range(1, nsub + 1):
        hi = n_valid * sub if n_valid < nsub else tm
        @pl.when((valid > (n_valid - 1) * sub) & (valid <= hi))
        def _(n_valid=n_valid):
            swiglu(n_valid)

    @pl.when(f == pl.num_programs(1) - 1)
    def _():
        ys_ref[...] = _pack_pair(acc_ref[:, :d // 2], acc_ref[:, d // 2:])


def _moe_ffn(xs, tile_expert, tile_valid, wg, wu, wd, tm, tf, sub):
    cap, w = xs.shape
    d = 2 * w
    ff = wg.shape[3]
    nf = ff // tf

    def f_eff(i, f, valid):
        return jnp.where(valid[i] > 0, f, nf - 1)

    grid_spec = pltpu.PrefetchScalarGridSpec(
        num_scalar_prefetch=2,
        grid=(cap // tm, nf),
        in_specs=[
            pl.BlockSpec((tm, w), lambda i, f, te, valid: (i, 0)),
            pl.BlockSpec((None, None, d, tf), lambda i, f, te, valid: (0, te[i], 0, f_eff(i, f, valid))),
            pl.BlockSpec((None, None, d, tf), lambda i, f, te, valid: (0, te[i], 0, f_eff(i, f, valid))),
            pl.BlockSpec((None, None, tf, d), lambda i, f, te, valid: (0, te[i], f_eff(i, f, valid), 0)),
        ],
        out_specs=pl.BlockSpec((tm, w), lambda i, f, te, valid: (i, 0)),
        scratch_shapes=[pltpu.VMEM((tm, d), BF16), pltpu.VMEM((tm, d), F32)],
    )
    return pl.pallas_call(
        functools.partial(_moe_ffn_kernel, sub),
        grid_spec=grid_spec,
        out_shape=jax.ShapeDtypeStruct((cap, w), U32),
        compiler_params=_cparams("parallel", "arbitrary"),
        name="moe_ffn",
    )(tile_expert, tile_valid, xs, wg, wu, wd)


def _combine_kernel(tc, pos_ref, posn_ref, x_ref, info_ref, gate_ref, ys_ref, o_ref, buf_ref, sem):
    i = pl.program_id(0)
    slot = i % 2

    def issue(p_ref, s):
        def body(t, c):
            _row_copy(ys_ref, p_ref[0, 2 * t], buf_ref.at[s, 0], t, sem.at[s]).start()
            _row_copy(ys_ref, p_ref[0, 2 * t + 1], buf_ref.at[s, 1], t, sem.at[s]).start()
            return c

        lax.fori_loop(0, tc, body, 0, unroll=8)

    @pl.when(i == 0)
    def _():
        issue(pos_ref, 0)

    @pl.when(i + 1 < pl.num_programs(0))
    def _():
        issue(posn_ref, 1 - slot)

    for k in range(2):
        pltpu.make_async_copy(ys_ref.at[pl.ds(0, tc)], buf_ref.at[slot, k], sem.at[slot]).wait()
    w = buf_ref.shape[3]
    g1 = info_ref[:, 4:5]
    g2 = info_ref[:, 5:6]
    a1, b1 = _unpack_pair(buf_ref[slot, 0])
    a2, b2 = _unpack_pair(buf_ref[slot, 1])
    gate = gate_ref[0]
    o_ref[:, :w] = x_ref[:, :w] + gate[:, :w] * (g1 * a1 + g2 * a2)
    o_ref[:, w:] = x_ref[:, w:] + gate[:, w:] * (g1 * b1 + g2 * b2)


def _combine(x2, info, pos, ys, mod, mod_row, tc):
    m, d = x2.shape
    w = ys.shape[1]
    n = m // tc
    pos3 = pos.reshape(n, 1, 2 * tc)
    return pl.pallas_call(
        functools.partial(_combine_kernel, tc),
        grid=(n,),
        in_specs=[
            pl.BlockSpec((None, 1, 2 * tc), lambda i: (i, 0, 0), memory_space=pltpu.SMEM),
            pl.BlockSpec((None, 1, 2 * tc), lambda i: (jnp.minimum(i + 1, n - 1), 0, 0), memory_space=pltpu.SMEM),
            pl.BlockSpec((tc, d), lambda i: (i, 0)),
            pl.BlockSpec((tc, 8), lambda i: (i, 0)),
            pl.BlockSpec((1, 1, d), lambda i: (mod_row(i, tc) * 6 + 5, 0, 0)),
            pl.BlockSpec(memory_space=pl.ANY),
        ],
        out_specs=pl.BlockSpec((tc, d), lambda i: (i, 0)),
        out_shape=jax.ShapeDtypeStruct((m, d), F32),
        scratch_shapes=[pltpu.VMEM((2, 2, tc, w), U32), pltpu.SemaphoreType.DMA((2,))],
        compiler_params=_cparams("arbitrary"),
        name="moe_combine",
    )(pos3, pos3, x2, info, mod, ys)


MOE_SUB = 256


def _moe(x2, s, mod, mod_row, g, router_w, wg, wu, wd):
    m, d = x2.shape
    tm = _tile(m, 1024)
    sub = _tile(tm, MOE_SUB)
    tf = _tile(wg.shape[3], 512)
    tr = _tile(s, 512)
    tg = _tile(s, 256)
    wr = jnp.zeros((d, LANES), F32).at[:, :N_EXPERTS].set(router_w)
    wr_hi = wr.astype(BF16)
    wr_lo = (wr - wr_hi.astype(F32)).astype(BF16)
    fin, info, cnt = _router(x2, mod, mod_row, g, wr_hi, wr_lo, tr)

    counts = cnt[0, :N_EXPERTS].astype(jnp.int32)
    padded = (counts + tm - 1) // tm * tm
    ends = jnp.cumsum(padded)
    starts = ends - padded
    experts = info[:, 0:2].astype(jnp.int32)
    pos = starts[experts] + info[:, 2:4].astype(jnp.int32)
    ntiles = (2 * m) // tm + N_EXPERTS
    cap = ntiles * tm
    tile_start = jnp.arange(ntiles, dtype=jnp.int32) * tm
    tile_expert = jnp.sum(tile_start[:, None] >= ends[None, :], axis=1).astype(jnp.int32)
    active = tile_start < ends[-1]
    last_expert = tile_expert[ends[-1] // tm - 1]
    tile_expert = jnp.where(active, tile_expert, last_expert)
    group_end = (starts + counts)[tile_expert]
    tile_valid = jnp.where(active, jnp.clip(group_end - tile_start, 0, tm), 0).astype(jnp.int32)

    xs = _dispatch(fin, pos, cap, tg)
    ys = _moe_ffn(xs, tile_expert, tile_valid, wg, wu, wd, tm, tf, sub)
    return _combine(x2, info, pos, ys, mod, mod_row, tg)


def _rope_tables(t, rot_dim):
    rows = t // GRID_W
    row = jnp.repeat(jnp.arange(rows, dtype=F32), GRID_W)
    col = jnp.tile(jnp.arange(GRID_W, dtype=F32), rows)
    half = rot_dim // 2
    inv = ROPE_THETA ** (-jnp.arange(0, half, 2, dtype=F32) / half)
    ang_r = row[:, None] * inv[None, :]
    ang_c = col[:, None] * inv[None, :]
    pad = LANES - rot_dim
    cos = jnp.concatenate([jnp.cos(ang_r), jnp.cos(ang_r), jnp.cos(ang_c), jnp.cos(ang_c),
                           jnp.ones((t, pad), F32)], axis=1)
    sin = jnp.concatenate([-jnp.sin(ang_r), jnp.sin(ang_r), -jnp.sin(ang_c), jnp.sin(ang_c),
                           jnp.zeros((t, pad), F32)], axis=1)
    return cos, sin


def _identity_rope(t):
    return jnp.ones((t, LANES), F32), jnp.zeros((t, LANES), F32)


def _pad_heads(w, real, padded):
    k = w.shape[0]
    w = w.reshape(k, MLA_HEADS, real)
    return jnp.pad(w, ((0, 0), (0, 0), (0, padded - real))).reshape(k, MLA_HEADS * padded)


def kernel(x, c, ctx, c_ctx, ada_w, ada_b, mix_norm_g, ffn_norm_g, even_w_in, mla_q_a_norm_g, mla_w_q_b, mla_kv_a_norm_g, mla_w_kv_b, mla_q_norm_g, mla_k_norm_g, conv_dw_w, conv_dw_b, conv_ln_g, conv_ln_b, even_w_out, dense_w_gate, dense_w_up, dense_w_down, odd_w_in, swa_q_norm_g, swa_k_norm_g, swa_sink, odd_w_out, router_w, expert_w_gate, expert_w_up, expert_w_down):
    b, s, d = x.shape
    l = ctx.shape[1]
    assert ada_w.shape[0] == 2, "two layers: an even (MLA | conv, dense) then an odd (SWA | Fourier, MoE) one"

    r = (b + 1 + 7) // 8 * 8
    cvec = jnp.zeros((r, d), F32).at[:b].set(c).at[b].set(c_ctx)
    mod = _modulation(cvec, ada_w, ada_b).reshape(2, r * 6, 1, d)
    mod0, mod1 = mod[0], mod[1]

    tm = _tile(s, 512)
    tml = _tile(b * l, 512)
    tm_big = _tile(s, 1024)
    tml_big = _tile(b * l, 1024)
    lat_row = lambda i, tile: (i * tile) // s
    ctx_row = lambda i, tile: b
    x2 = x.reshape(b * s, d)
    h2 = ctx.reshape(b * l, d)

    w_in = even_w_in[0]
    w_in0 = jnp.concatenate(
        [w_in[:, MLA_IN:], w_in[:, :MLA_IN], jnp.zeros((d, MLA_IN_PAD - MLA_IN), F32)], axis=1).astype(BF16)
    p_lat = _norm_mod_matmul(x2, mod0, lat_row, 0, mix_norm_g[0], w_in0, tm_big)
    p_ctx = _norm_mod_matmul(h2, mod0, ctx_row, 0, mix_norm_g[0], w_in0, tml_big)

    mla_scale = MLA_QK ** -0.5
    wq = _pad_heads(mla_w_q_b[0], MLA_QK, MLA_QK_PAD).astype(BF16)
    wkv = mla_w_kv_b[0].astype(BF16)
    qg = jnp.pad(mla_q_norm_g[0] * mla_scale, (0, MLA_QK_PAD - MLA_QK)).reshape(1, -1)
    kg = jnp.pad(mla_k_norm_g[0], (0, MLA_QK_PAD - MLA_QK)).reshape(1, -1)
    qag = mla_q_a_norm_g[0].reshape(1, -1)
    kvag = mla_kv_a_norm_g[0].reshape(1, -1)
    mla_col = 2 * CONV_CH // MLA_IN_PAD
    cos_m, sin_m = _rope_tables(s, MLA_ROPE)
    q_l, k_l, v_l = _mla_prep(p_lat, b, s, mla_col, cos_m, sin_m, qag, kvag, wq, wkv, qg, kg)
    cos_i, sin_i = _identity_rope(l)
    q_c, k_c, v_c = _mla_prep(p_ctx, b, l, mla_col, cos_i, sin_i, qag, kvag, wq, wkv, qg, kg)
    att_l = _mla_attn(q_l, [(k_l, v_l), (k_c, v_c)]).reshape(b * s, -1)
    att_c = _mla_attn(q_c, [(k_c, v_c)]).reshape(b * l, -1)
    conv_l = _conv_module(p_lat, b, s, conv_dw_w[0], conv_dw_b[0], conv_ln_g[0], conv_ln_b[0])
    conv_c = _conv_module(p_ctx, b, l, conv_dw_w[0], conv_dw_b[0], conv_ln_g[0], conv_ln_b[0])

    w_out = even_w_out[0].astype(BF16)
    k_att = MLA_HEADS * MLA_V
    x2 = _out_proj_residual(att_l, conv_l, w_out[:k_att], w_out[k_att:], x2, mod0, lat_row, 2, tm)
    h2 = _out_proj_residual(att_c, conv_c, w_out[:k_att], w_out[k_att:], h2, mod0, ctx_row, 2, tml)

    wg = dense_w_gate[0].astype(BF16)
    wu = dense_w_up[0].astype(BF16)
    wd = dense_w_down[0].astype(BF16)
    tf = _tile(wg.shape[1], 512)
    x2 = _dense_ffn(x2, mod0, lat_row, ffn_norm_g[0], wg, wu, wd, tm_big, tf)
    h2 = _dense_ffn(h2, mod0, ctx_row, ffn_norm_g[0], wg, wu, wd, tml_big, tf)

    w_in1 = odd_w_in[0].astype(BF16)
    p = _norm_mod_matmul(x2, mod1, lat_row, 0, mix_norm_g[1], w_in1, tm_big)
    w_kv_c = w_in1[:, ODD_Q:ODD_Q + 2 * ODD_KV]
    pc = _norm_mod_matmul(h2, mod1, ctx_row, 0, mix_norm_g[1], w_kv_c, tml_big)

    cos_s, sin_s = _rope_tables(s, HEAD_DIM)
    gains = jnp.concatenate([jnp.tile(swa_q_norm_g[0] * HEAD_DIM ** -0.5, WIN_Q_HEADS),
                             jnp.tile(swa_k_norm_g[0], WIN_KV_HEADS)]).reshape(1, -1)
    qk = _swa_prep(p, b, s, WIN_Q_HEADS + WIN_KV_HEADS, 0, cos_s, sin_s, gains)
    kc = _swa_prep(pc, b, l, WIN_KV_HEADS, 0, cos_i, sin_i, gains[:, ODD_Q:])
    att = _swa_attn(qk, p, kc, pc, swa_sink[0], b, s, l).reshape(b * s, -1)
    fcol = (ODD_Q + 2 * ODD_KV) // (FNET_GROUPS * FNET_CH)
    four = _fourier(p, b, s, fcol).reshape(b * s, -1)

    w_out1 = odd_w_out[0].astype(BF16)
    x2 = _out_proj_residual(att, four, w_out1[:ODD_Q], w_out1[ODD_Q:], x2, mod1, lat_row, 2, tm)

    x2 = _moe(x2, s, mod1, lat_row, ffn_norm_g[1], router_w[0],
              expert_w_gate, expert_w_up, expert_w_down)
    return x2.reshape(b, s, d)
```

```python
import functools

import jax
import jax.numpy as jnp
import numpy as np
from jax import lax
from jax.experimental import pallas as pl
from jax.experimental.pallas import tpu as pltpu

F32 = jnp.float32
BF16 = jnp.bfloat16
U32 = jnp.uint32

EPS = 1e-6
ROPE_THETA = 10000.0
GRID_W = 64
NEG_INF = -1e30
LANES = 128

MLA_HEADS = 8
MLA_Q_RANK = 512
MLA_KV_RANK = 256
MLA_NOPE = 128
MLA_ROPE = 64
MLA_V = 128
MLA_IN = MLA_Q_RANK + MLA_KV_RANK + MLA_ROPE
MLA_QK = MLA_NOPE + MLA_ROPE
MLA_QK_PAD = 2 * LANES
MLA_IN_PAD = 1024
CONV_CH = 1024
CONV_WIDTH = 31
CONV_HALO = 16
HEAD_DIM = 128
WIN_Q_HEADS = 12
WIN_KV_HEADS = 4
WIN_GROUP = WIN_Q_HEADS // WIN_KV_HEADS
WINDOW = 128
BLOCK = 128
FNET_GROUPS = 4
FNET_CH = 128
ODD_Q = WIN_Q_HEADS * HEAD_DIM
ODD_KV = WIN_KV_HEADS * HEAD_DIM
N_EXPERTS = 8

VMEM_LIMIT = 60 * 1024 * 1024


def _cparams(*sem):
    return pltpu.CompilerParams(dimension_semantics=sem, vmem_limit_bytes=VMEM_LIMIT)


def _tile(n, pref):
    if n <= pref:
        return n
    t = pref - pref % 8
    while n % t:
        t -= 8
    return t


def _sigmoid(x):
    return 1.0 / (1.0 + jnp.exp(-x))


def _silu(x):
    return x * _sigmoid(x)


def _rms_mod(x, g, shift, scale):
    ms = jnp.mean(x * x, axis=-1, keepdims=True)
    return (x * lax.rsqrt(ms + EPS) * g) * (1.0 + scale) + shift


NORM_ROWS = 16


def _rms_mod_rows(x_ref, rows, gain, shift):
    x = x_ref[rows, :]
    ms = jnp.mean(x * x, axis=-1, keepdims=True)
    return (x * lax.rsqrt(ms + EPS) * gain + shift).astype(BF16)


def _rms_mod_loop(x_ref, g_ref, sh_ref, sc_ref, dst_ref):
    gain = g_ref[...] * (1.0 + sc_ref[0])
    shift = sh_ref[0]

    def body(c, carry):
        rows = pl.ds(pl.multiple_of(c * NORM_ROWS, NORM_ROWS), NORM_ROWS)
        dst_ref[rows, :] = _rms_mod_rows(x_ref, rows, gain, shift)
        return carry

    lax.fori_loop(0, x_ref.shape[0] // NORM_ROWS, body, 0, unroll=8)


def _pack_pair(a, b):
    ai = lax.bitcast_convert_type(a.astype(BF16).astype(F32), U32)
    bi = lax.bitcast_convert_type(b.astype(BF16).astype(F32), U32)
    return (ai >> 16) | bi


def _unpack_pair(w):
    a = lax.bitcast_convert_type(w << 16, F32)
    b = lax.bitcast_convert_type(w & jnp.uint32(0xFFFF0000), F32)
    return a, b


def _modulation_kernel(c_ref, w_ref, b_ref, o_ref):
    a = _silu(c_ref[...]).astype(BF16)
    acc = jnp.dot(a, w_ref[...].astype(BF16), preferred_element_type=F32)
    o_ref[...] = acc + b_ref[...]


def _modulation(cvec, ada_w, ada_b):
    depth, d, n = ada_w.shape
    r = cvec.shape[0]
    tn = _tile(n, 1024)
    return pl.pallas_call(
        _modulation_kernel,
        grid=(depth, n // tn),
        in_specs=[
            pl.BlockSpec((r, d), lambda l, j: (0, 0)),
            pl.BlockSpec((None, d, tn), lambda l, j: (l, 0, j)),
            pl.BlockSpec((None, 1, tn), lambda l, j: (l, 0, j)),
        ],
        out_specs=pl.BlockSpec((None, r, tn), lambda l, j: (l, 0, j)),
        out_shape=jax.ShapeDtypeStruct((depth, r, n), F32),
        compiler_params=_cparams("parallel", "parallel"),
        name="modulation",
    )(cvec, ada_w, ada_b.reshape(depth, 1, n))


def _nmm_kernel(x_ref, sh_ref, sc_ref, g_ref, w_ref, o_ref, u_ref):
    tm = x_ref.shape[0]
    sub = _tile(tm, FFN_SUB)

    @pl.when(pl.program_id(1) == 0)
    def _():
        gain = g_ref[...] * (1.0 + sc_ref[0])
        shift = sh_ref[0]
        for r0 in range(0, tm, sub):
            for c0 in range(r0, r0 + sub, NORM_ROWS):
                chunk = slice(c0, c0 + NORM_ROWS)
                u_ref[chunk, :] = _rms_mod_rows(x_ref, chunk, gain, shift)
            rows = slice(r0, r0 + sub)
            o_ref[rows, :] = jnp.dot(u_ref[rows, :], w_ref[...],
                                     preferred_element_type=F32).astype(o_ref.dtype)

    @pl.when(pl.program_id(1) != 0)
    def _():
        o_ref[...] = jnp.dot(u_ref[...], w_ref[...], preferred_element_type=F32).astype(o_ref.dtype)


def _norm_mod_matmul(x2, mod, mod_row, which, g, w, tm):
    m, d = x2.shape
    n = w.shape[1]
    tn = n // 2
    return pl.pallas_call(
        _nmm_kernel,
        grid=(m // tm, 2),
        in_specs=[
            pl.BlockSpec((tm, d), lambda i, j: (i, 0)),
            pl.BlockSpec((1, 1, d), lambda i, j: (mod_row(i, tm) * 6 + which, 0, 0)),
            pl.BlockSpec((1, 1, d), lambda i, j: (mod_row(i, tm) * 6 + which + 1, 0, 0)),
            pl.BlockSpec((1, d), lambda i, j: (0, 0)),
            pl.BlockSpec((d, tn), lambda i, j: (0, j)),
        ],
        out_specs=pl.BlockSpec((tm, tn), lambda i, j: (i, j)),
        out_shape=jax.ShapeDtypeStruct((m, n), BF16),
        scratch_shapes=[pltpu.VMEM((tm, d), BF16)],
        compiler_params=_cparams("parallel", "arbitrary"),
        name="norm_mod_matmul",
    )(x2, mod, mod, g.reshape(1, d), w)


def _split_dot(x, m01):
    hi = x.astype(BF16)
    lo = (x - hi.astype(F32)).astype(BF16)
    return jnp.dot(hi, m01, preferred_element_type=F32) + jnp.dot(lo, m01, preferred_element_type=F32)


def _rope_mxu(t, cos, sin, perm):
    return t * cos + _split_dot(t, perm) * sin


def _mla_prep_kernel(p_ref, cos_ref, sin_ref, qag_ref, kvag_ref, wq_ref, wkv_ref, qg_ref, kg_ref,
                     perm_ref, q_ref, k_ref, v_ref):
    p = p_ref[...].astype(F32)
    qa = p[:, :MLA_Q_RANK]
    kva = p[:, MLA_Q_RANK:MLA_Q_RANK + MLA_KV_RANK]
    kpe = p[:, MLA_Q_RANK + MLA_KV_RANK:MLA_Q_RANK + MLA_KV_RANK + LANES]
    qn = qa * lax.rsqrt(jnp.mean(qa * qa, axis=-1, keepdims=True) + EPS) * qag_ref[...]
    kvn = kva * lax.rsqrt(jnp.mean(kva * kva, axis=-1, keepdims=True) + EPS) * kvag_ref[...]
    q = jnp.dot(qn.astype(BF16), wq_ref[...], preferred_element_type=F32)
    kv = jnp.dot(kvn.astype(BF16), wkv_ref[...], preferred_element_type=F32)
    cos = cos_ref[...]
    sin = sin_ref[...]
    qg = qg_ref[...]
    kg = kg_ref[...]
    perm = perm_ref[...]
    kpe_ss = jnp.sum(kpe * kpe, axis=-1, keepdims=True)
    kpe_rot = _rope_mxu(kpe * kg[:, LANES:], cos, sin, perm)
    for h in range(MLA_HEADS):
        lo = h * MLA_QK_PAD
        qh = q[:, lo:lo + MLA_QK_PAD]
        rs = lax.rsqrt(jnp.sum(qh * qh, axis=-1, keepdims=True) * (1.0 / MLA_QK) + EPS)
        q_ref[0, h, :, :LANES] = (qh[:, :LANES] * rs * qg[:, :LANES]).astype(BF16)
        tail = _rope_mxu(qh[:, LANES:] * rs * qg[:, LANES:], cos, sin, perm)
        q_ref[0, h, :, LANES:] = tail.astype(BF16)
        kn = kv[:, lo:lo + MLA_NOPE]
        rs = lax.rsqrt((jnp.sum(kn * kn, axis=-1, keepdims=True) + kpe_ss) * (1.0 / MLA_QK) + EPS)
        k_ref[0, h, :, :LANES] = (kn * rs * kg[:, :LANES]).astype(BF16)
        k_ref[0, h, :, LANES:] = (kpe_rot * rs).astype(BF16)
        v_ref[0, h] = kv[:, lo + MLA_NOPE:lo + MLA_NOPE + MLA_V].astype(BF16)


def _rope_perm(grp):
    lane = np.arange(LANES)
    src = np.where((lane // grp) % 2 == 0, lane + grp, lane - grp)
    perm = np.zeros((LANES, LANES), np.float32)
    perm[src, lane] = 1.0
    return jnp.asarray(perm, BF16)


def _mla_prep(p, b, t, col_block, cos, sin, qag, kvag, wq, wkv, qg, kg):
    tm = _tile(t, 256)
    nt = t // tm
    const = lambda bb, i: (0, 0)
    perm = _rope_perm(MLA_ROPE // 4)
    hs = lambda w: pl.BlockSpec((1, MLA_HEADS, tm, w), lambda bb, i: (bb, 0, i, 0))
    return pl.pallas_call(
        _mla_prep_kernel,
        grid=(b, nt),
        in_specs=[
            pl.BlockSpec((tm, MLA_IN_PAD), lambda bb, i: (bb * nt + i, col_block)),
            pl.BlockSpec((tm, LANES), lambda bb, i: (i, 0)),
            pl.BlockSpec((tm, LANES), lambda bb, i: (i, 0)),
            pl.BlockSpec((1, MLA_Q_RANK), const),
            pl.BlockSpec((1, MLA_KV_RANK), const),
            pl.BlockSpec(wq.shape, const),
            pl.BlockSpec(wkv.shape, const),
            pl.BlockSpec((1, MLA_QK_PAD), const),
            pl.BlockSpec((1, MLA_QK_PAD), const),
            pl.BlockSpec((LANES, LANES), const),
        ],
        out_specs=[hs(MLA_QK_PAD), hs(MLA_QK_PAD), hs(MLA_V)],
        out_shape=[
            jax.ShapeDtypeStruct((b, MLA_HEADS, t, MLA_QK_PAD), BF16),
            jax.ShapeDtypeStruct((b, MLA_HEADS, t, MLA_QK_PAD), BF16),
            jax.ShapeDtypeStruct((b, MLA_HEADS, t, MLA_V), BF16),
        ],
        compiler_params=_cparams("parallel", "parallel"),
        name="mla_prep",
    )(p, cos, sin, qag, kvag, wq, wkv, qg, kg, perm)


def _nt_dot(a, b):
    return lax.dot_general(a, b, (((1,), (1,)), ((), ())), preferred_element_type=F32)


MLA_HEADS_PER_STEP = 4


def _mla_attn_kernel(nseg, q_ref, *refs):
    o_ref = refs[2 * nseg]
    scores = [[_nt_dot(q_ref[0, hp], refs[2 * i][0, hp]) for i in range(nseg)]
              for hp in range(MLA_HEADS_PER_STEP)]
    for hp in range(MLA_HEADS_PER_STEP):
        s = scores[hp]
        m = s[0].max(axis=-1, keepdims=True)
        for si in s[1:]:
            m = jnp.maximum(m, si.max(axis=-1, keepdims=True))
        den = 0.0
        acc = 0.0
        for i in range(nseg):
            e = jnp.exp(s[i] - m)
            den = den + jnp.sum(e, axis=-1, keepdims=True)
            acc = acc + jnp.dot(e.astype(BF16), refs[2 * i + 1][0, hp], preferred_element_type=F32)
        o_ref[0, :, hp * MLA_V:(hp + 1) * MLA_V] = (acc / den).astype(o_ref.dtype)


def _mla_attn(q, kvs):
    b, h, t, _ = q.shape
    tq = _tile(t, 256)
    hp = MLA_HEADS_PER_STEP
    in_specs = [pl.BlockSpec((1, hp, tq, MLA_QK_PAD), lambda bb, hh, i: (bb, hh, i, 0))]
    args = [q]
    for k, v in kvs:
        n = k.shape[2]
        in_specs.append(pl.BlockSpec((1, hp, n, MLA_QK_PAD), lambda bb, hh, i: (bb, hh, 0, 0)))
        in_specs.append(pl.BlockSpec((1, hp, n, MLA_V), lambda bb, hh, i: (bb, hh, 0, 0)))
        args += [k, v]
    return pl.pallas_call(
        functools.partial(_mla_attn_kernel, len(kvs)),
        grid=(b, h // hp, t // tq),
        in_specs=in_specs,
        out_specs=pl.BlockSpec((1, tq, hp * MLA_V), lambda bb, hh, i: (bb, i, hh)),
        out_shape=jax.ShapeDtypeStruct((b, t, h * MLA_V), BF16),
        compiler_params=_cparams("parallel", "parallel", "parallel"),
        name="mla_attn",
    )(*args)


CONV_ROWS = 64
CONV_WIN = CONV_ROWS + 2 * CONV_HALO


def _conv_kernel(t, p_ref, w_ref, b_ref, g_ref, beta_ref, o_ref, hp_ref, cv_ref):
    zeros = jnp.zeros((CONV_HALO, CONV_CH), F32)
    hp_ref[0:CONV_HALO, :] = zeros
    hp_ref[CONV_HALO + t:CONV_HALO + t + CONV_HALO, :] = zeros

    def glu(i, c):
        r0 = pl.multiple_of(i * CONV_ROWS, CONV_ROWS)
        a = p_ref[pl.ds(r0, CONV_ROWS), :CONV_CH].astype(F32)
        gate = p_ref[pl.ds(r0, CONV_ROWS), CONV_CH:].astype(F32)
        hp_ref[pl.ds(r0 + CONV_HALO, CONV_ROWS), :] = a * _sigmoid(gate)
        return c

    lax.fori_loop(0, t // CONV_ROWS, glu, 0)

    def tile(i, c):
        r0 = pl.multiple_of(i * CONV_ROWS, CONV_ROWS)
        for cc in range(CONV_CH // LANES):
            cs = slice(cc * LANES, (cc + 1) * LANES)
            win = hp_ref[pl.ds(r0, CONV_WIN), cs]
            acc = jnp.zeros((CONV_ROWS, LANES), F32) + b_ref[:, cs]
            for r in range(8):
                rolled = win if r == 0 else pltpu.roll(win, CONV_WIN - r, axis=0)
                for k in range(CONV_WIDTH):
                    off = k + CONV_HALO - CONV_WIDTH // 2
                    if off % 8 == r:
                        acc = acc + rolled[off - r:off - r + CONV_ROWS] * w_ref[k:k + 1, cs]
            cv_ref[:, cs] = acc
        h = cv_ref[...]
        mu = jnp.mean(h, axis=-1, keepdims=True)
        hc = h - mu
        y = hc * lax.rsqrt(jnp.mean(hc * hc, axis=-1, keepdims=True) + EPS) * g_ref[...] + beta_ref[...]
        o_ref[pl.ds(r0, CONV_ROWS), :] = _silu(y).astype(o_ref.dtype)
        return c

    lax.fori_loop(0, t // CONV_ROWS, tile, 0)


def _conv_module(p, b, t, dw_w, dw_b, ln_g, ln_b):
    const = lambda bb: (0, 0)
    return pl.pallas_call(
        functools.partial(_conv_kernel, t),
        grid=(b,),
        in_specs=[
            pl.BlockSpec((t, 2 * CONV_CH), lambda bb: (bb, 0)),
            pl.BlockSpec((CONV_WIDTH, CONV_CH), const),
            pl.BlockSpec((1, CONV_CH), const),
            pl.BlockSpec((1, CONV_CH), const),
            pl.BlockSpec((1, CONV_CH), const),
        ],
        out_specs=pl.BlockSpec((t, CONV_CH), lambda bb: (bb, 0)),
        out_shape=jax.ShapeDtypeStruct((b * t, CONV_CH), BF16),
        scratch_shapes=[pltpu.VMEM((t + 2 * CONV_HALO, CONV_CH), F32), pltpu.VMEM((CONV_ROWS, CONV_CH), F32)],
        compiler_params=_cparams("parallel"),
        name="conv_module",
    )(p, dw_w, dw_b.reshape(1, -1), ln_g.reshape(1, -1), ln_b.reshape(1, -1))


def _oproj_kernel(a1_ref, a2_ref, w1_ref, w2_ref, x_ref, gate_ref, o_ref):
    acc = jnp.dot(a1_ref[...], w1_ref[...], preferred_element_type=F32)
    acc = acc + jnp.dot(a2_ref[...], w2_ref[...], preferred_element_type=F32)
    o_ref[...] = x_ref[...] + gate_ref[0] * acc


def _out_proj_residual(a1, a2, w1, w2, x2, mod, mod_row, which, tm):
    m, d = x2.shape
    k1, k2 = w1.shape[0], w2.shape[0]
    const = lambda i: (0, 0)
    return pl.pallas_call(
        _oproj_kernel,
        grid=(m // tm,),
        in_specs=[
            pl.BlockSpec((tm, k1), lambda i: (i, 0)),
            pl.BlockSpec((tm, k2), lambda i: (i, 0)),
            pl.BlockSpec((k1, d), const),
            pl.BlockSpec((k2, d), const),
            pl.BlockSpec((tm, d), lambda i: (i, 0)),
            pl.BlockSpec((1, 1, d), lambda i: (mod_row(i, tm) * 6 + which, 0, 0)),
        ],
        out_specs=pl.BlockSpec((tm, d), lambda i: (i, 0)),
        out_shape=jax.ShapeDtypeStruct((m, d), F32),
        compiler_params=_cparams("parallel"),
        name="out_proj_residual",
    )(a1, a2, w1, w2, x2, mod)


FFN_SUB = 256


def _ffn_kernel(x_ref, sh_ref, sc_ref, gate_ref, g_ref, wg_ref, wu_ref, wd_ref, o_ref, u_ref):
    f = pl.program_id(1)
    acc_ref = o_ref
    tm = u_ref.shape[0]
    sub = _tile(tm, FFN_SUB)

    def swiglu_step(first):
        if first:
            gain = g_ref[...] * (1.0 + sc_ref[0])
            shift = sh_ref[0]
        hid = []
        for r0 in range(0, tm, sub):
            if first:
                for c0 in range(r0, r0 + sub, NORM_ROWS):
                    chunk = slice(c0, c0 + NORM_ROWS)
                    u_ref[chunk, :] = _rms_mod_rows(x_ref, chunk, gain, shift)
            u = u_ref[r0:r0 + sub]
            hg = jnp.dot(u, wg_ref[...], preferred_element_type=F32)
            hu = jnp.dot(u, wu_ref[...], preferred_element_type=F32)
            hid.append((_silu(hg) * hu).astype(BF16))
        for k, r0 in enumerate(range(0, tm, sub)):
            down = jnp.dot(hid[k], wd_ref[...], preferred_element_type=F32)
            if first:
                acc_ref[r0:r0 + sub] = down
            else:
                acc_ref[r0:r0 + sub] += down

    @pl.when(f == 0)
    def _():
        swiglu_step(True)

    @pl.when(f != 0)
    def _():
        swiglu_step(False)

    @pl.when(f == pl.num_programs(1) - 1)
    def _():
        o_ref[...] = x_ref[...] + gate_ref[0] * acc_ref[...]


def _dense_ffn(x2, mod, mod_row, g, wg, wu, wd, tm, tf):
    m, d = x2.shape
    ff = wg.shape[1]
    row = lambda w: pl.BlockSpec((1, 1, d), lambda i, f: (mod_row(i, tm) * 6 + w, 0, 0))
    return pl.pallas_call(
        _ffn_kernel,
        grid=(m // tm, ff // tf),
        in_specs=[
            pl.BlockSpec((tm, d), lambda i, f: (i, 0)),
            row(3), row(4), row(5),
            pl.BlockSpec((1, d), lambda i, f: (0, 0)),
            pl.BlockSpec((d, tf), lambda i, f: (0, f)),
            pl.BlockSpec((d, tf), lambda i, f: (0, f)),
            pl.BlockSpec((tf, d), lambda i, f: (f, 0)),
        ],
        out_specs=pl.BlockSpec((tm, d), lambda i, f: (i, 0)),
        out_shape=jax.ShapeDtypeStruct((m, d), F32),
        scratch_shapes=[pltpu.VMEM((tm, d), BF16)],
        compiler_params=_cparams("parallel", "arbitrary"),
        name="dense_ffn",
    )(x2, mod, mod, mod, g.reshape(1, d), wg, wu, wd)


def _swa_prep_kernel(nheads, p_ref, cos_ref, sin_ref, g_ref, perm_ref, ones_ref, o_ref):
    cos = cos_ref[...]
    sin = sin_ref[...]
    perm = perm_ref[...]
    ones = ones_ref[...]
    for h in range(nheads):
        cs = slice(h * HEAD_DIM, (h + 1) * HEAD_DIM)
        t = p_ref[:, cs].astype(F32)
        ms = _split_dot(t * t, ones) * (1.0 / HEAD_DIM)
        t = t * lax.rsqrt(ms + EPS) * g_ref[:, cs]
        o_ref[:, cs] = _rope_mxu(t, cos, sin, perm).astype(BF16)


def _swa_prep(p, b, t, nheads, col_block, cos, sin, gains):
    tm = _tile(t, 256)
    nt = t // tm
    w = nheads * HEAD_DIM
    return pl.pallas_call(
        functools.partial(_swa_prep_kernel, nheads),
        grid=(b, nt),
        in_specs=[
            pl.BlockSpec((tm, w), lambda bb, i: (bb * nt + i, col_block)),
            pl.BlockSpec((tm, LANES), lambda bb, i: (i, 0)),
            pl.BlockSpec((tm, LANES), lambda bb, i: (i, 0)),
            pl.BlockSpec((1, w), lambda bb, i: (0, 0)),
            pl.BlockSpec((LANES, LANES), lambda bb, i: (0, 0)),
            pl.BlockSpec((LANES, LANES), lambda bb, i: (0, 0)),
        ],
        out_specs=pl.BlockSpec((tm, w), lambda bb, i: (bb * nt + i, 0)),
        out_shape=jax.ShapeDtypeStruct((b * t, w), BF16),
        compiler_params=_cparams("parallel", "parallel"),
        name="swa_prep",
    )(p, cos, sin, gains, _rope_perm(HEAD_DIM // 4), jnp.ones((LANES, LANES), BF16))


def _swa_attn_kernel(s_len, sink_ref, q_ref, k_ref, v_ref, kc_ref, vc_ref, o_ref):
    blk = pl.program_id(1)
    span = BLOCK + 2 * WINDOW
    start = blk * BLOCK
    ws = pl.multiple_of(jnp.clip(start - WINDOW, 0, s_len - span), BLOCK)
    rows = WIN_GROUP * BLOCK
    row = lax.broadcasted_iota(jnp.int32, (rows, span), 0)
    col = lax.broadcasted_iota(jnp.int32, (rows, span), 1)
    in_window = jnp.abs((start + row % BLOCK) - (ws + col)) <= WINDOW
    rcol = lax.broadcasted_iota(jnp.int32, (rows, 1), 0) // BLOCK
    scores = []
    for n in range(WIN_KV_HEADS):
        hs = slice(n * HEAD_DIM, (n + 1) * HEAD_DIM)
        q0 = n * WIN_GROUP * HEAD_DIM
        q = jnp.concatenate(
            [q_ref[0, :, q0 + g * HEAD_DIM:q0 + (g + 1) * HEAD_DIM] for g in range(WIN_GROUP)], axis=0)
        s_w = jnp.where(in_window, _nt_dot(q, k_ref[0, pl.ds(ws, span), hs]), NEG_INF)
        s_c = _nt_dot(q, kc_ref[0, :, hs])
        scores.append((s_w, s_c))
    for n in range(WIN_KV_HEADS):
        hs = slice(n * HEAD_DIM, (n + 1) * HEAD_DIM)
        q0 = n * WIN_GROUP * HEAD_DIM
        vw = v_ref[0, pl.ds(ws, span), hs]
        s_w, s_c = scores[n]
        sink = jnp.zeros((rows, 1), F32)
        for g in range(WIN_GROUP):
            sink = jnp.where(rcol == g, sink_ref[n * WIN_GROUP + g], sink)
        m = jnp.maximum(jnp.maximum(s_w.max(axis=-1, keepdims=True), s_c.max(axis=-1, keepdims=True)), sink)
        e_w = jnp.exp(s_w - m)
        e_c = jnp.exp(s_c - m)
        den = jnp.sum(e_w, axis=-1, keepdims=True) + jnp.sum(e_c, axis=-1, keepdims=True) + jnp.exp(sink - m)
        acc = jnp.dot(e_w.astype(BF16), vw, preferred_element_type=F32)
        acc = acc + jnp.dot(e_c.astype(BF16), vc_ref[0, :, hs], preferred_element_type=F32)
        out = acc / den
        for g in range(WIN_GROUP):
            o_ref[0, :, q0 + g * HEAD_DIM:q0 + (g + 1) * HEAD_DIM] = (
                out[g * BLOCK:(g + 1) * BLOCK].astype(o_ref.dtype))


def _swa_attn(qk, p, kc, pc, sink, b, s_len, ctx_len):
    qk3 = qk.reshape(b, s_len, -1)
    p3 = p.reshape(b, s_len, -1)
    kc3 = kc.reshape(b, ctx_len, -1)
    pc3 = pc.reshape(b, ctx_len, -1)
    return pl.pallas_call(
        functools.partial(_swa_attn_kernel, s_len),
        grid=(b, s_len // BLOCK),
        in_specs=[
            pl.BlockSpec(memory_space=pltpu.SMEM),
            pl.BlockSpec((1, BLOCK, ODD_Q), lambda bb, i: (bb, i, 0)),
            pl.BlockSpec((1, s_len, ODD_KV), lambda bb, i: (bb, 0, ODD_Q // ODD_KV)),
            pl.BlockSpec((1, s_len, ODD_KV), lambda bb, i: (bb, 0, (ODD_Q + ODD_KV) // ODD_KV)),
            pl.BlockSpec((1, ctx_len, ODD_KV), lambda bb, i: (bb, 0, 0)),
            pl.BlockSpec((1, ctx_len, ODD_KV), lambda bb, i: (bb, 0, 1)),
        ],
        out_specs=pl.BlockSpec((1, BLOCK, ODD_Q), lambda bb, i: (bb, i, 0)),
        out_shape=jax.ShapeDtypeStruct((b, s_len, ODD_Q), BF16),
        compiler_params=_cparams("parallel", "parallel"),
        name="swa_attn",
    )(sink, qk3, qk3, p3, kc3, pc3)


def _fourier_kernel(scale, f_ref, cs_ref, ct_ref, st_ref, o_ref, xc_ref, xs_ref):
    @pl.when(pl.program_id(1) == 0)
    def _():
        for g in range(FNET_GROUPS):
            cs = slice(g * FNET_CH, (g + 1) * FNET_CH)
            r = jnp.dot(f_ref[0, :, cs], cs_ref[...], preferred_element_type=F32)
            xc_ref[:, cs] = r[:, :FNET_CH].astype(BF16)
            xs_ref[:, cs] = r[:, FNET_CH:].astype(BF16)

    y = jnp.dot(ct_ref[...], xc_ref[...], preferred_element_type=F32)
    y = y - jnp.dot(st_ref[...], xs_ref[...], preferred_element_type=F32)
    o_ref[0] = (y * scale).astype(o_ref.dtype)


def _dft_tables(n):
    k = jnp.arange(n, dtype=jnp.int32)
    ang = ((k[:, None] * k[None, :]) % n).astype(F32) * (2.0 * np.pi / n)
    return jnp.cos(ang), jnp.sin(ang)


def _fourier(p, b, t, col_block):
    w = FNET_GROUPS * FNET_CH
    p3 = p.reshape(b, t, -1)
    cc, sc = _dft_tables(FNET_CH)
    ct, st = _dft_tables(t)
    cs = jnp.concatenate([cc, sc], axis=1).astype(BF16)
    tk = _tile(t, 512)
    return pl.pallas_call(
        functools.partial(_fourier_kernel, float((t * FNET_CH) ** -0.5)),
        grid=(b, t // tk),
        in_specs=[
            pl.BlockSpec((1, t, w), lambda bb, i: (bb, 0, col_block)),
            pl.BlockSpec((FNET_CH, 2 * FNET_CH), lambda bb, i: (0, 0)),
            pl.BlockSpec((tk, t), lambda bb, i: (i, 0)),
            pl.BlockSpec((tk, t), lambda bb, i: (i, 0)),
        ],
        out_specs=pl.BlockSpec((1, tk, w), lambda bb, i: (bb, i, 0)),
        out_shape=jax.ShapeDtypeStruct((b, t, w), BF16),
        scratch_shapes=[pltpu.VMEM((t, w), BF16), pltpu.VMEM((t, w), BF16)],
        compiler_params=_cparams("parallel", "arbitrary"),
        name="fourier",
    )(p3, cs, ct.astype(BF16), st.astype(BF16))


def _router_kernel(x_ref, sh_ref, sc_ref, g_ref, wh_ref, wl_ref, fin_ref, info_ref, cnt_ref, run_ref):
    @pl.when(pl.program_id(0) == 0)
    def _():
        run_ref[...] = jnp.zeros_like(run_ref)

    u = _rms_mod(x_ref[...], g_ref[...], sh_ref[0], sc_ref[0])
    d = u.shape[1]
    fin_ref[...] = _pack_pair(u[:, :d // 2], u[:, d // 2:])
    u_hi = u.astype(BF16)
    u_lo = (u - u_hi.astype(F32)).astype(BF16)
    logits = (jnp.dot(u_hi, wh_ref[...], preferred_element_type=F32)
              + (jnp.dot(u_lo, wh_ref[...], preferred_element_type=F32)
                 + jnp.dot(u_hi, wl_ref[...], preferred_element_type=F32)))
    tr = logits.shape[0]
    lane = lax.broadcasted_iota(jnp.int32, (tr, LANES), 1).astype(F32)
    logits = jnp.where(lane < N_EXPERTS, logits, -jnp.inf)
    m1 = logits.max(axis=-1, keepdims=True)
    i1 = jnp.where(logits == m1, lane, float(LANES)).min(axis=-1, keepdims=True)
    rest = jnp.where(lane == i1, -jnp.inf, logits)
    m2 = rest.max(axis=-1, keepdims=True)
    i2 = jnp.where(rest == m2, lane, float(LANES)).min(axis=-1, keepdims=True)
    e21 = jnp.exp(m2 - m1)
    g1 = 1.0 / (1.0 + e21)
    g2 = e21 / (1.0 + e21)
    oh1 = lane == i1
    oh2 = lane == i2
    oh = (oh1 | oh2).astype(F32)
    r = lax.broadcasted_iota(jnp.int32, (tr, tr), 0)
    c = lax.broadcasted_iota(jnp.int32, (tr, tr), 1)
    before = (r > c).astype(BF16)
    prefix = jnp.dot(before, oh.astype(BF16), preferred_element_type=F32) + run_ref[...]
    r1 = jnp.sum(jnp.where(oh1, prefix, 0.0), axis=-1, keepdims=True)
    r2 = jnp.sum(jnp.where(oh2, prefix, 0.0), axis=-1, keepdims=True)
    run = run_ref[...] + jnp.sum(oh, axis=0, keepdims=True)
    run_ref[...] = run
    cnt_ref[...] = run
    info = jnp.zeros((tr, LANES), F32)
    for j, val in enumerate((i1, i2, r1, r2, g1, g2)):
        info = jnp.where(lane == j, val, info)
    info_ref[...] = info[:, :8]


def _router(x2, mod, mod_row, g, wr_hi, wr_lo, tr):
    m, d = x2.shape
    row = lambda w: pl.BlockSpec((1, 1, d), lambda i: (mod_row(i, tr) * 6 + w, 0, 0))
    const = lambda i: (0, 0)
    return pl.pallas_call(
        _router_kernel,
        grid=(m // tr,),
        in_specs=[
            pl.BlockSpec((tr, d), lambda i: (i, 0)),
            row(3), row(4),
            pl.BlockSpec((1, d), const),
            pl.BlockSpec((d, LANES), const),
            pl.BlockSpec((d, LANES), const),
        ],
        out_specs=[
            pl.BlockSpec((tr, d // 2), lambda i: (i, 0)),
            pl.BlockSpec((tr, 8), lambda i: (i, 0)),
            pl.BlockSpec((1, LANES), const),
        ],
        out_shape=[
            jax.ShapeDtypeStruct((m, d // 2), U32),
            jax.ShapeDtypeStruct((m, 8), F32),
            jax.ShapeDtypeStruct((1, LANES), F32),
        ],
        scratch_shapes=[pltpu.VMEM((1, LANES), F32)],
        compiler_params=_cparams("arbitrary"),
        name="moe_router",
    )(x2, mod, mod, g.reshape(1, d), wr_hi, wr_lo)


def _row_copy(src, src_row, dst, dst_row, sem):
    return pltpu.make_async_copy(src.at[pl.ds(src_row, 1)], dst.at[pl.ds(dst_row, 1)], sem)


def _dispatch_kernel(tg, pos_ref, fin_ref, init_ref, xs_ref, sem):
    del init_ref

    def issue(t, c):
        _row_copy(fin_ref, t, xs_ref, pos_ref[0, 2 * t], sem).start()
        _row_copy(fin_ref, t, xs_ref, pos_ref[0, 2 * t + 1], sem).start()
        return c

    lax.fori_loop(0, tg, issue, 0, unroll=8)
    for _ in range(2):
        pltpu.make_async_copy(fin_ref, xs_ref.at[pl.ds(0, tg)], sem).wait()


def _dispatch(fin, pos, cap, tg):
    m, w = fin.shape
    return pl.pallas_call(
        functools.partial(_dispatch_kernel, tg),
        grid=(m // tg,),
        in_specs=[
            pl.BlockSpec((None, 1, 2 * tg), lambda i: (i, 0, 0), memory_space=pltpu.SMEM),
            pl.BlockSpec((tg, w), lambda i: (i, 0)),
            pl.BlockSpec(memory_space=pl.ANY),
        ],
        out_specs=pl.BlockSpec(memory_space=pl.ANY),
        out_shape=jax.ShapeDtypeStruct((cap, w), U32),
        scratch_shapes=[pltpu.SemaphoreType.DMA(())],
        input_output_aliases={2: 0},
        compiler_params=_cparams("arbitrary"),
        name="moe_dispatch",
    )(pos.reshape(m // tg, 1, 2 * tg), fin, jnp.zeros((cap, w), U32))


def _moe_ffn_kernel(sub, te_ref, valid_ref, xs_ref, wg_ref, wu_ref, wd_ref, ys_ref, u_ref, acc_ref):
    i = pl.program_id(0)
    f = pl.program_id(1)
    tm, d = u_ref.shape
    valid = valid_ref[i]

    @pl.when(f == 0)
    def _():
        a, b = _unpack_pair(xs_ref[...])
        u_ref[:, :d // 2] = a.astype(BF16)
        u_ref[:, d // 2:] = b.astype(BF16)
        acc_ref[...] = jnp.zeros_like(acc_ref)

    def swiglu(nsub):
        wg = wg_ref[...].astype(BF16)
        wu = wu_ref[...].astype(BF16)
        wd = wd_ref[...].astype(BF16)
        hid = []
        for sb in range(nsub):
            u = u_ref[sb * sub:(sb + 1) * sub]
            hg = jnp.dot(u, wg, preferred_element_type=F32)
            hu = jnp.dot(u, wu, preferred_element_type=F32)
            hid.append((_silu(hg) * hu).astype(BF16))
        for sb in range(nsub):
            acc_ref[sb * sub:(sb + 1) * sub] += jnp.dot(hid[sb], wd, preferred_element_type=F32)

    nsub = tm // sub
    for n_valid in range(1, nsub + 1):
        hi = n_valid * sub if n_valid < nsub else tm
        @pl.when((valid > (n_valid - 1) * sub) & (valid <= hi))
        def _(n_valid=n_valid):
            swiglu(n_valid)

    @pl.when(f == pl.num_programs(1) - 1)
    def _():
        ys_ref[...] = _pack_pair(acc_ref[:, :d // 2], acc_ref[:, d // 2:])


def _moe_ffn(xs, tile_expert, tile_valid, wg, wu, wd, tm, tf, sub):
    cap, w = xs.shape
    d = 2 * w
    ff = wg.shape[3]
    nf = ff // tf

    def f_eff(i, f, valid):
        return jnp.where(valid[i] > 0, f, nf - 1)

    grid_spec = pltpu.PrefetchScalarGridSpec(
        num_scalar_prefetch=2,
        grid=(cap // tm, nf),
        in_specs=[
            pl.BlockSpec((tm, w), lambda i, f, te, valid: (i, 0)),
            pl.BlockSpec((None, None, d, tf), lambda i, f, te, valid: (0, te[i], 0, f_eff(i, f, valid))),
            pl.BlockSpec((None, None, d, tf), lambda i, f, te, valid: (0, te[i], 0, f_eff(i, f, valid))),
            pl.BlockSpec((None, None, tf, d), lambda i, f, te, valid: (0, te[i], f_eff(i, f, valid), 0)),
        ],
        out_specs=pl.BlockSpec((tm, w), lambda i, f, te, valid: (i, 0)),
        scratch_shapes=[pltpu.VMEM((tm, d), BF16), pltpu.VMEM((tm, d), F32)],
    )
    return pl.pallas_call(
        functools.partial(_moe_ffn_kernel, sub),
        grid_spec=grid_spec,
        out_shape=jax.ShapeDtypeStruct((cap, w), U32),
        compiler_params=_cparams("parallel", "arbitrary"),
        name="moe_ffn",
    )(tile_expert, tile_valid, xs, wg, wu, wd)


def _combine_kernel(tc, pos_ref, posn_ref, x_ref, info_ref, gate_ref, ys_ref, o_ref, buf_ref, sem):
    i = pl.program_id(0)
    slot = i % 2

    def issue(p_ref, s):
        def body(t, c):
            _row_copy(ys_ref, p_ref[0, 2 * t], buf_ref.at[s, 0], t, sem.at[s]).start()
            _row_copy(ys_ref, p_ref[0, 2 * t + 1], buf_ref.at[s, 1], t, sem.at[s]).start()
            return c

        lax.fori_loop(0, tc, body, 0, unroll=8)

    @pl.when(i == 0)
    def _():
        issue(pos_ref, 0)

    @pl.when(i + 1 < pl.num_programs(0))
    def _():
        issue(posn_ref, 1 - slot)

    for k in range(2):
        pltpu.make_async_copy(ys_ref.at[pl.ds(0, tc)], buf_ref.at[slot, k], sem.at[slot]).wait()
    w = buf_ref.shape[3]
    g1 = info_ref[:, 4:5]
    g2 = info_ref[:, 5:6]
    a1, b1 = _unpack_pair(buf_ref[slot, 0])
    a2, b2 = _unpack_pair(buf_ref[slot, 1])
    gate = gate_ref[0]
    o_ref[:, :w] = x_ref[:, :w] + gate[:, :w] * (g1 * a1 + g2 * a2)
    o_ref[:, w:] = x_ref[:, w:] + gate[:, w:] * (g1 * b1 + g2 * b2)


def _combine(x2, info, pos, ys, mod, mod_row, tc):
    m, d = x2.shape
    w = ys.shape[1]
    n = m // tc
    pos3 = pos.reshape(n, 1, 2 * tc)
    return pl.pallas_call(
        functools.partial(_combine_kernel, tc),
        grid=(n,),
        in_specs=[
            pl.BlockSpec((None, 1, 2 * tc), lambda i: (i, 0, 0), memory_space=pltpu.SMEM),
            pl.BlockSpec((None, 1, 2 * tc), lambda i: (jnp.minimum(i + 1, n - 1), 0, 0), memory_space=pltpu.SMEM),
            pl.BlockSpec((tc, d), lambda i: (i, 0)),
            pl.BlockSpec((tc, 8), lambda i: (i, 0)),
            pl.BlockSpec((1, 1, d), lambda i: (mod_row(i, tc) * 6 + 5, 0, 0)),
            pl.BlockSpec(memory_space=pl.ANY),
        ],
        out_specs=pl.BlockSpec((tc, d), lambda i: (i, 0)),
        out_shape=jax.ShapeDtypeStruct((m, d), F32),
        scratch_shapes=[pltpu.VMEM((2, 2, tc, w), U32), pltpu.SemaphoreType.DMA((2,))],
        compiler_params=_cparams("arbitrary"),
        name="moe_combine",
    )(pos3, pos3, x2, info, mod, ys)


MOE_SUB = 256


def _moe(x2, s, mod, mod_row, g, router_w, wg, wu, wd):
    m, d = x2.shape
    tm = _tile(m, 1024)
    sub = _tile(tm, MOE_SUB)
    tf = _tile(wg.shape[3], 512)
    tr = _tile(s, 512)
    tg = _tile(s, 256)
    wr = jnp.zeros((d, LANES), F32).at[:, :N_EXPERTS].set(router_w)
    wr_hi = wr.astype(BF16)
    wr_lo = (wr - wr_hi.astype(F32)).astype(BF16)
    fin, info, cnt = _router(x2, mod, mod_row, g, wr_hi, wr_lo, tr)

    counts = cnt[0, :N_EXPERTS].astype(jnp.int32)
    padded = (counts + tm - 1) // tm * tm
    ends = jnp.cumsum(padded)
    starts = ends - padded
    experts = info[:, 0:2].astype(jnp.int32)
    pos = starts[experts] + info[:, 2:4].astype(jnp.int32)
    ntiles = (2 * m) // tm + N_EXPERTS
    cap = ntiles * tm
    tile_start = jnp.arange(ntiles, dtype=jnp.int32) * tm
    tile_expert = jnp.sum(tile_start[:, None] >= ends[None, :], axis=1).astype(jnp.int32)
    active = tile_start < ends[-1]
    last_expert = tile_expert[ends[-1] // tm - 1]
    tile_expert = jnp.where(active, tile_expert, last_expert)
    group_end = (starts + counts)[tile_expert]
    tile_valid = jnp.where(active, jnp.clip(group_end - tile_start, 0, tm), 0).astype(jnp.int32)

    xs = _dispatch(fin, pos, cap, tg)
    ys = _moe_ffn(xs, tile_expert, tile_valid, wg, wu, wd, tm, tf, sub)
    return _combine(x2, info, pos, ys, mod, mod_row, tg)


def _rope_tables(t, rot_dim):
    rows = t // GRID_W
    row = jnp.repeat(jnp.arange(rows, dtype=F32), GRID_W)
    col = jnp.tile(jnp.arange(GRID_W, dtype=F32), rows)
    half = rot_dim // 2
    inv = ROPE_THETA ** (-jnp.arange(0, half, 2, dtype=F32) / half)
    ang_r = row[:, None] * inv[None, :]
    ang_c = col[:, None] * inv[None, :]
    pad = LANES - rot_dim
    cos = jnp.concatenate([jnp.cos(ang_r), jnp.cos(ang_r), jnp.cos(ang_c), jnp.cos(ang_c),
                           jnp.ones((t, pad), F32)], axis=1)
    sin = jnp.concatenate([-jnp.sin(ang_r), jnp.sin(ang_r), -jnp.sin(ang_c), jnp.sin(ang_c),
                           jnp.zeros((t, pad), F32)], axis=1)
    return cos, sin


def _identity_rope(t):
    return jnp.ones((t, LANES), F32), jnp.zeros((t, LANES), F32)


def _pad_heads(w, real, padded):
    k = w.shape[0]
    w = w.reshape(k, MLA_HEADS, real)
    return jnp.pad(w, ((0, 0), (0, 0), (0, padded - real))).reshape(k, MLA_HEADS * padded)


def kernel(x, c, ctx, c_ctx, ada_w, ada_b, mix_norm_g, ffn_norm_g, even_w_in, mla_q_a_norm_g, mla_w_q_b, mla_kv_a_norm_g, mla_w_kv_b, mla_q_norm_g, mla_k_norm_g, conv_dw_w, conv_dw_b, conv_ln_g, conv_ln_b, even_w_out, dense_w_gate, dense_w_up, dense_w_down, odd_w_in, swa_q_norm_g, swa_k_norm_g, swa_sink, odd_w_out, router_w, expert_w_gate, expert_w_up, expert_w_down):
    b, s, d = x.shape
    l = ctx.shape[1]
    assert ada_w.shape[0] == 2, "two layers: an even (MLA | conv, dense) then an odd (SWA | Fourier, MoE) one"

    r = (b + 1 + 7) // 8 * 8
    cvec = jnp.zeros((r, d), F32).at[:b].set(c).at[b].set(c_ctx)
    mod = _modulation(cvec, ada_w, ada_b).reshape(2, r * 6, 1, d)
    mod0, mod1 = mod[0], mod[1]

    tm = _tile(s, 512)
    tml = _tile(b * l, 512)
    tm_big = _tile(s, 1024)
    tml_big = _tile(b * l, 1024)
    lat_row = lambda i, tile: (i * tile) // s
    ctx_row = lambda i, tile: b
    x2 = x.reshape(b * s, d)
    h2 = ctx.reshape(b * l, d)

    w_in = even_w_in[0]
    w_in0 = jnp.concatenate(
        [w_in[:, MLA_IN:], w_in[:, :MLA_IN], jnp.zeros((d, MLA_IN_PAD - MLA_IN), F32)], axis=1).astype(BF16)
    p_lat = _norm_mod_matmul(x2, mod0, lat_row, 0, mix_norm_g[0], w_in0, tm_big)
    p_ctx = _norm_mod_matmul(h2, mod0, ctx_row, 0, mix_norm_g[0], w_in0, tml_big)

    mla_scale = MLA_QK ** -0.5
    wq = _pad_heads(mla_w_q_b[0], MLA_QK, MLA_QK_PAD).astype(BF16)
    wkv = mla_w_kv_b[0].astype(BF16)
    qg = jnp.pad(mla_q_norm_g[0] * mla_scale, (0, MLA_QK_PAD - MLA_QK)).reshape(1, -1)
    kg = jnp.pad(mla_k_norm_g[0], (0, MLA_QK_PAD - MLA_QK)).reshape(1, -1)
    qag = mla_q_a_norm_g[0].reshape(1, -1)
    kvag = mla_kv_a_norm_g[0].reshape(1, -1)
    mla_col = 2 * CONV_CH // MLA_IN_PAD
    cos_m, sin_m = _rope_tables(s, MLA_ROPE)
    q_l, k_l, v_l = _mla_prep(p_lat, b, s, mla_col, cos_m, sin_m, qag, kvag, wq, wkv, qg, kg)
    cos_i, sin_i = _identity_rope(l)
    q_c, k_c, v_c = _mla_prep(p_ctx, b, l, mla_col, cos_i, sin_i, qag, kvag, wq, wkv, qg, kg)
    att_l = _mla_attn(q_l, [(k_l, v_l), (k_c, v_c)]).reshape(b * s, -1)
    att_c = _mla_attn(q_c, [(k_c, v_c)]).reshape(b * l, -1)
    conv_l = _conv_module(p_lat, b, s, conv_dw_w[0], conv_dw_b[0], conv_ln_g[0], conv_ln_b[0])
    conv_c = _conv_module(p_ctx, b, l, conv_dw_w[0], conv_dw_b[0], conv_ln_g[0], conv_ln_b[0])

    w_out = even_w_out[0].astype(BF16)
    k_att = MLA_HEADS * MLA_V
    x2 = _out_proj_residual(att_l, conv_l, w_out[:k_att], w_out[k_att:], x2, mod0, lat_row, 2, tm)
    h2 = _out_proj_residual(att_c, conv_c, w_out[:k_att], w_out[k_att:], h2, mod0, ctx_row, 2, tml)

    wg = dense_w_gate[0].astype(BF16)
    wu = dense_w_up[0].astype(BF16)
    wd = dense_w_down[0].astype(BF16)
    tf = _tile(wg.shape[1], 512)
    x2 = _dense_ffn(x2, mod0, lat_row, ffn_norm_g[0], wg, wu, wd, tm_big, tf)
    h2 = _dense_ffn(h2, mod0, ctx_row, ffn_norm_g[0], wg, wu, wd, tml_big, tf)

    w_in1 = odd_w_in[0].astype(BF16)
    p = _norm_mod_matmul(x2, mod1, lat_row, 0, mix_norm_g[1], w_in1, tm_big)
    w_kv_c = w_in1[:, ODD_Q:ODD_Q + 2 * ODD_KV]
    pc = _norm_mod_matmul(h2, mod1, ctx_row, 0, mix_norm_g[1], w_kv_c, tml_big)

    cos_s, sin_s = _rope_tables(s, HEAD_DIM)
    gains = jnp.concatenate([jnp.tile(swa_q_norm_g[0] * HEAD_DIM ** -0.5, WIN_Q_HEADS),
                             jnp.tile(swa_k_norm_g[0], WIN_KV_HEADS)]).reshape(1, -1)
    qk = _swa_prep(p, b, s, WIN_Q_HEADS + WIN_KV_HEADS, 0, cos_s, sin_s, gains)
    kc = _swa_prep(pc, b, l, WIN_KV_HEADS, 0, cos_i, sin_i, gains[:, ODD_Q:])
    att = _swa_attn(qk, p, kc, pc, swa_sink[0], b, s, l).reshape(b * s, -1)
    fcol = (ODD_Q + 2 * ODD_KV) // (FNET_GROUPS * FNET_CH)
    four = _fourier(p, b, s, fcol).reshape(b * s, -1)

    w_out1 = odd_w_out[0].astype(BF16)
    x2 = _out_proj_residual(att, four, w_out1[:ODD_Q], w_out1[ODD_Q:], x2, mod1, lat_row, 2, tm)

    x2 = _moe(x2, s, mod1, lat_row, ffn_norm_g[1], router_w[0],
              expert_w_gate, expert_w_up, expert_w_down)
    return x2.reshape(b, s, d)
```

```python
import functools

import jax
import jax.numpy as jnp
import numpy as np
from jax import lax
from jax.experimental import pallas as pl
from jax.experimental.pallas import tpu as pltpu

F32 = jnp.float32
BF16 = jnp.bfloat16
U32 = jnp.uint32

EPS = 1e-6
ROPE_THETA = 10000.0
GRID_W = 64
NEG_INF = -1e30
LANES = 128

MLA_HEADS = 8
MLA_Q_RANK = 512
MLA_KV_RANK = 256
MLA_NOPE = 128
MLA_ROPE = 64
MLA_V = 128
MLA_IN = MLA_Q_RANK + MLA_KV_RANK + MLA_ROPE
MLA_QK = MLA_NOPE + MLA_ROPE
MLA_QK_PAD = 2 * LANES
MLA_IN_PAD = 1024
CONV_CH = 1024
CONV_WIDTH = 31
CONV_HALO = 16
HEAD_DIM = 128
WIN_Q_HEADS = 12
WIN_KV_HEADS = 4
WIN_GROUP = WIN_Q_HEADS // WIN_KV_HEADS
WINDOW = 128
BLOCK = 128
FNET_GROUPS = 4
FNET_CH = 128
ODD_Q = WIN_Q_HEADS * HEAD_DIM
ODD_KV = WIN_KV_HEADS * HEAD_DIM
N_EXPERTS = 8

VMEM_LIMIT = 60 * 1024 * 1024


def _cparams(*sem):
    return pltpu.CompilerParams(dimension_semantics=sem, vmem_limit_bytes=VMEM_LIMIT)


def _tile(n, pref):
    if n <= pref:
        return n
    t = pref - pref % 8
    while n % t:
        t -= 8
    return t


def _sigmoid(x):
    return 1.0 / (1.0 + jnp.exp(-x))


def _silu(x):
    return x * _sigmoid(x)


def _rms_mod(x, g, shift, scale):
    ms = jnp.mean(x * x, axis=-1, keepdims=True)
    return (x * lax.rsqrt(ms + EPS) * g) * (1.0 + scale) + shift


NORM_ROWS = 16


def _rms_mod_rows(x_ref, rows, gain, shift):
    x = x_ref[rows, :]
    ms = jnp.mean(x * x, axis=-1, keepdims=True)
    return (x * lax.rsqrt(ms + EPS) * gain + shift).astype(BF16)


def _rms_mod_loop(x_ref, g_ref, sh_ref, sc_ref, dst_ref):
    gain = g_ref[...] * (1.0 + sc_ref[0])
    shift = sh_ref[0]

    def body(c, carry):
        rows = pl.ds(pl.multiple_of(c * NORM_ROWS, NORM_ROWS), NORM_ROWS)
        dst_ref[rows, :] = _rms_mod_rows(x_ref, rows, gain, shift)
        return carry

    lax.fori_loop(0, x_ref.shape[0] // NORM_ROWS, body, 0, unroll=8)


def _pack_pair(a, b):
    ai = lax.bitcast_convert_type(a.astype(BF16).astype(F32), U32)
    bi = lax.bitcast_convert_type(b.astype(BF16).astype(F32), U32)
    return (ai >> 16) | bi


def _unpack_pair(w):
    a = lax.bitcast_convert_type(w << 16, F32)
    b = lax.bitcast_convert_type(w & jnp.uint32(0xFFFF0000), F32)
    return a, b


def _modulation_kernel(c_ref, w_ref, b_ref, o_ref):
    a = _silu(c_ref[...]).astype(BF16)
    acc = jnp.dot(a, w_ref[...].astype(BF16), preferred_element_type=F32)
    o_ref[...] = acc + b_ref[...]


def _modulation(cvec, ada_w, ada_b):
    depth, d, n = ada_w.shape
    r = cvec.shape[0]
    tn = _tile(n, 1024)
    return pl.pallas_call(
        _modulation_kernel,
        grid=(depth, n // tn),
        in_specs=[
            pl.BlockSpec((r, d), lambda l, j: (0, 0)),
            pl.BlockSpec((None, d, tn), lambda l, j: (l, 0, j)),
            pl.BlockSpec((None, 1, tn), lambda l, j: (l, 0, j)),
        ],
        out_specs=pl.BlockSpec((None, r, tn), lambda l, j: (l, 0, j)),
        out_shape=jax.ShapeDtypeStruct((depth, r, n), F32),
        compiler_params=_cparams("parallel", "parallel"),
        name="modulation",
    )(cvec, ada_w, ada_b.reshape(depth, 1, n))


def _nmm_kernel(x_ref, sh_ref, sc_ref, g_ref, w_ref, o_ref, u_ref):
    tm = x_ref.shape[0]
    sub = _tile(tm, FFN_SUB)

    @pl.when(pl.program_id(1) == 0)
    def _():
        gain = g_ref[...] * (1.0 + sc_ref[0])
        shift = sh_ref[0]
        for r0 in range(0, tm, sub):
            for c0 in range(r0, r0 + sub, NORM_ROWS):
                chunk = slice(c0, c0 + NORM_ROWS)
                u_ref[chunk, :] = _rms_mod_rows(x_ref, chunk, gain, shift)
            rows = slice(r0, r0 + sub)
            o_ref[rows, :] = jnp.dot(u_ref[rows, :], w_ref[...],
                                     preferred_element_type=F32).astype(o_ref.dtype)

    @pl.when(pl.program_id(1) != 0)
    def _():
        o_ref[...] = jnp.dot(u_ref[...], w_ref[...], preferred_element_type=F32).astype(o_ref.dtype)


def _norm_mod_matmul(x2, mod, mod_row, which, g, w, tm):
    m, d = x2.shape
    n = w.shape[1]
    tn = n // 2
    return pl.pallas_call(
        _nmm_kernel,
        grid=(m // tm, 2),
        in_specs=[
            pl.BlockSpec((tm, d), lambda i, j: (i, 0)),
            pl.BlockSpec((1, 1, d), lambda i, j: (mod_row(i, tm) * 6 + which, 0, 0)),
            pl.BlockSpec((1, 1, d), lambda i, j: (mod_row(i, tm) * 6 + which + 1, 0, 0)),
            pl.BlockSpec((1, d), lambda i, j: (0, 0)),
            pl.BlockSpec((d, tn), lambda i, j: (0, j)),
        ],
        out_specs=pl.BlockSpec((tm, tn), lambda i, j: (i, j)),
        out_shape=jax.ShapeDtypeStruct((m, n), BF16),
        scratch_shapes=[pltpu.VMEM((tm, d), BF16)],
        compiler_params=_cparams("parallel", "arbitrary"),
        name="norm_mod_matmul",
    )(x2, mod, mod, g.reshape(1, d), w)


def _split_dot(x, m01):
    hi = x.astype(BF16)
    lo = (x - hi.astype(F32)).astype(BF16)
    return jnp.dot(hi, m01, preferred_element_type=F32) + jnp.dot(lo, m01, preferred_element_type=F32)


def _rope_mxu(t, cos, sin, perm):
    return t * cos + _split_dot(t, perm) * sin


def _mla_prep_kernel(p_ref, cos_ref, sin_ref, qag_ref, kvag_ref, wq_ref, wkv_ref, qg_ref, kg_ref,
                     perm_ref, q_ref, k_ref, v_ref):
    p = p_ref[...].astype(F32)
    qa = p[:, :MLA_Q_RANK]
    kva = p[:, MLA_Q_RANK:MLA_Q_RANK + MLA_KV_RANK]
    kpe = p[:, MLA_Q_RANK + MLA_KV_RANK:MLA_Q_RANK + MLA_KV_RANK + LANES]
    qn = qa * lax.rsqrt(jnp.mean(qa * qa, axis=-1, keepdims=True) + EPS) * qag_ref[...]
    kvn = kva * lax.rsqrt(jnp.mean(kva * kva, axis=-1, keepdims=True) + EPS) * kvag_ref[...]
    q = jnp.dot(qn.astype(BF16), wq_ref[...], preferred_element_type=F32)
    kv = jnp.dot(kvn.astype(BF16), wkv_ref[...], preferred_element_type=F32)
    cos = cos_ref[...]
    sin = sin_ref[...]
    qg = qg_ref[...]
    kg = kg_ref[...]
    perm = perm_ref[...]
    kpe_ss = jnp.sum(kpe * kpe, axis=-1, keepdims=True)
    kpe_rot = _rope_mxu(kpe * kg[:, LANES:], cos, sin, perm)
    for h in range(MLA_HEADS):
        lo = h * MLA_QK_PAD
        qh = q[:, lo:lo + MLA_QK_PAD]
        rs = lax.rsqrt(jnp.sum(qh * qh, axis=-1, keepdims=True) * (1.0 / MLA_QK) + EPS)
        q_ref[0, h, :, :LANES] = (qh[:, :LANES] * rs * qg[:, :LANES]).astype(BF16)
        tail = _rope_mxu(qh[:, LANES:] * rs * qg[:, LANES:], cos, sin, perm)
        q_ref[0, h, :, LANES:] = tail.astype(BF16)
        kn = kv[:, lo:lo + MLA_NOPE]
        rs = lax.rsqrt((jnp.sum(kn * kn, axis=-1, keepdims=True) + kpe_ss) * (1.0 / MLA_QK) + EPS)
        k_ref[0, h, :, :LANES] = (kn * rs * kg[:, :LANES]).astype(BF16)
        k_ref[0, h, :, LANES:] = (kpe_rot * rs).astype(BF16)
        v_ref[0, h] = kv[:, lo + MLA_NOPE:lo + MLA_NOPE + MLA_V].astype(BF16)


def _rope_perm(grp):
    lane = np.arange(LANES)
    src = np.where((lane // grp) % 2 == 0, lane + grp, lane - grp)
    perm = np.zeros((LANES, LANES), np.float32)
    perm[src, lane] = 1.0
    return jnp.asarray(perm, BF16)


def _mla_prep(p, b, t, col_block, cos, sin, qag, kvag, wq, wkv, qg, kg):
    tm = _tile(t, 256)
    nt = t // tm
    const = lambda bb, i: (0, 0)
    perm = _rope_perm(MLA_ROPE // 4)
    hs = lambda w: pl.BlockSpec((1, MLA_HEADS, tm, w), lambda bb, i: (bb, 0, i, 0))
    return pl.pallas_call(
        _mla_prep_kernel,
        grid=(b, nt),
        in_specs=[
            pl.BlockSpec((tm, MLA_IN_PAD), lambda bb, i: (bb * nt + i, col_block)),
            pl.BlockSpec((tm, LANES), lambda bb, i: (i, 0)),
            pl.BlockSpec((tm, LANES), lambda bb, i: (i, 0)),
            pl.BlockSpec((1, MLA_Q_RANK), const),
            pl.BlockSpec((1, MLA_KV_RANK), const),
            pl.BlockSpec(wq.shape, const),
            pl.BlockSpec(wkv.shape, const),
            pl.BlockSpec((1, MLA_QK_PAD), const),
            pl.BlockSpec((1, MLA_QK_PAD), const),
            pl.BlockSpec((LANES, LANES), const),
        ],
        out_specs=[hs(MLA_QK_PAD), hs(MLA_QK_PAD), hs(MLA_V)],
        out_shape=[
            jax.ShapeDtypeStruct((b, MLA_HEADS, t, MLA_QK_PAD), BF16),
            jax.ShapeDtypeStruct((b, MLA_HEADS, t, MLA_QK_PAD), BF16),
            jax.ShapeDtypeStruct((b, MLA_HEADS, t, MLA_V), BF16),
        ],
        compiler_params=_cparams("parallel", "parallel"),
        name="mla_prep",
    )(p, cos, sin, qag, kvag, wq, wkv, qg, kg, perm)


def _nt_dot(a, b):
    return lax.dot_general(a, b, (((1,), (1,)), ((), ())), preferred_element_type=F32)


MLA_HEADS_PER_STEP = 4


def _mla_attn_kernel(nseg, q_ref, *refs):
    o_ref = refs[2 * nseg]
    scores = [[_nt_dot(q_ref[0, hp], refs[2 * i][0, hp]) for i in range(nseg)]
              for hp in range(MLA_HEADS_PER_STEP)]
    for hp in range(MLA_HEADS_PER_STEP):
        s = scores[hp]
        m = s[0].max(axis=-1, keepdims=True)
        for si in s[1:]:
            m = jnp.maximum(m, si.max(axis=-1, keepdims=True))
        den = 0.0
        acc = 0.0
        for i in range(nseg):
            e = jnp.exp(s[i] - m)
            den = den + jnp.sum(e, axis=-1, keepdims=True)
            acc = acc + jnp.dot(e.astype(BF16), refs[2 * i + 1][0, hp], preferred_element_type=F32)
        o_ref[0, :, hp * MLA_V:(hp + 1) * MLA_V] = (acc / den).astype(o_ref.dtype)


def _mla_attn(q, kvs):
    b, h, t, _ = q.shape
    tq = _tile(t, 512)
    hp = MLA_HEADS_PER_STEP
    in_specs = [pl.BlockSpec((1, hp, tq, MLA_QK_PAD), lambda bb, hh, i: (bb, hh, i, 0))]
    args = [q]
    for k, v in kvs:
        n = k.shape[2]
        in_specs.append(pl.BlockSpec((1, hp, n, MLA_QK_PAD), lambda bb, hh, i: (bb, hh, 0, 0)))
        in_specs.append(pl.BlockSpec((1, hp, n, MLA_V), lambda bb, hh, i: (bb, hh, 0, 0)))
        args += [k, v]
    return pl.pallas_call(
        functools.partial(_mla_attn_kernel, len(kvs)),
        grid=(b, h // hp, t // tq),
        in_specs=in_specs,
        out_specs=pl.BlockSpec((1, tq, hp * MLA_V), lambda bb, hh, i: (bb, i, hh)),
        out_shape=jax.ShapeDtypeStruct((b, t, h * MLA_V), BF16),
        compiler_params=_cparams("parallel", "parallel", "parallel"),
        name="mla_attn",
    )(*args)


CONV_ROWS = 64
CONV_WIN = CONV_ROWS + 2 * CONV_HALO


def _conv_kernel(t, p_ref, w_ref, b_ref, g_ref, beta_ref, o_ref, hp_ref, cv_ref):
    zeros = jnp.zeros((CONV_HALO, CONV_CH), F32)
    hp_ref[0:CONV_HALO, :] = zeros
    hp_ref[CONV_HALO + t:CONV_HALO + t + CONV_HALO, :] = zeros

    def glu(i, c):
        r0 = pl.multiple_of(i * CONV_ROWS, CONV_ROWS)
        a = p_ref[pl.ds(r0, CONV_ROWS), :CONV_CH].astype(F32)
        gate = p_ref[pl.ds(r0, CONV_ROWS), CONV_CH:].astype(F32)
        hp_ref[pl.ds(r0 + CONV_HALO, CONV_ROWS), :] = a * _sigmoid(gate)
        return c

    lax.fori_loop(0, t // CONV_ROWS, glu, 0)

    def tile(i, c):
        r0 = pl.multiple_of(i * CONV_ROWS, CONV_ROWS)
        for cc in range(CONV_CH // LANES):
            cs = slice(cc * LANES, (cc + 1) * LANES)
            win = hp_ref[pl.ds(r0, CONV_WIN), cs]
            acc = jnp.zeros((CONV_ROWS, LANES), F32) + b_ref[:, cs]
            for r in range(8):
                rolled = win if r == 0 else pltpu.roll(win, CONV_WIN - r, axis=0)
                for k in range(CONV_WIDTH):
                    off = k + CONV_HALO - CONV_WIDTH // 2
                    if off % 8 == r:
                        acc = acc + rolled[off - r:off - r + CONV_ROWS] * w_ref[k:k + 1, cs]
            cv_ref[:, cs] = acc
        h = cv_ref[...]
        mu = jnp.mean(h, axis=-1, keepdims=True)
        hc = h - mu
        y = hc * lax.rsqrt(jnp.mean(hc * hc, axis=-1, keepdims=True) + EPS) * g_ref[...] + beta_ref[...]
        o_ref[pl.ds(r0, CONV_ROWS), :] = _silu(y).astype(o_ref.dtype)
        return c

    lax.fori_loop(0, t // CONV_ROWS, tile, 0)


def _conv_module(p, b, t, dw_w, dw_b, ln_g, ln_b):
    const = lambda bb: (0, 0)
    return pl.pallas_call(
        functools.partial(_conv_kernel, t),
        grid=(b,),
        in_specs=[
            pl.BlockSpec((t, 2 * CONV_CH), lambda bb: (bb, 0)),
            pl.BlockSpec((CONV_WIDTH, CONV_CH), const),
            pl.BlockSpec((1, CONV_CH), const),
            pl.BlockSpec((1, CONV_CH), const),
            pl.BlockSpec((1, CONV_CH), const),
        ],
        out_specs=pl.BlockSpec((t, CONV_CH), lambda bb: (bb, 0)),
        out_shape=jax.ShapeDtypeStruct((b * t, CONV_CH), BF16),
        scratch_shapes=[pltpu.VMEM((t + 2 * CONV_HALO, CONV_CH), F32), pltpu.VMEM((CONV_ROWS, CONV_CH), F32)],
        compiler_params=_cparams("parallel"),
        name="conv_module",
    )(p, dw_w, dw_b.reshape(1, -1), ln_g.reshape(1, -1), ln_b.reshape(1, -1))


def _oproj_kernel(a1_ref, a2_ref, w1_ref, w2_ref, x_ref, gate_ref, o_ref):
    acc = jnp.dot(a1_ref[...], w1_ref[...], preferred_element_type=F32)
    acc = acc + jnp.dot(a2_ref[...], w2_ref[...], preferred_element_type=F32)
    o_ref[...] = x_ref[...] + gate_ref[0] * acc


def _out_proj_residual(a1, a2, w1, w2, x2, mod, mod_row, which, tm):
    m, d = x2.shape
    k1, k2 = w1.shape[0], w2.shape[0]
    const = lambda i: (0, 0)
    return pl.pallas_call(
        _oproj_kernel,
        grid=(m // tm,),
        in_specs=[
            pl.BlockSpec((tm, k1), lambda i: (i, 0)),
            pl.BlockSpec((tm, k2), lambda i: (i, 0)),
            pl.BlockSpec((k1, d), const),
            pl.BlockSpec((k2, d), const),
            pl.BlockSpec((tm, d), lambda i: (i, 0)),
            pl.BlockSpec((1, 1, d), lambda i: (mod_row(i, tm) * 6 + which, 0, 0)),
        ],
        out_specs=pl.BlockSpec((tm, d), lambda i: (i, 0)),
        out_shape=jax.ShapeDtypeStruct((m, d), F32),
        compiler_params=_cparams("parallel"),
        name="out_proj_residual",
    )(a1, a2, w1, w2, x2, mod)


FFN_SUB = 256


def _ffn_kernel(x_ref, sh_ref, sc_ref, gate_ref, g_ref, wg_ref, wu_ref, wd_ref, o_ref, u_ref):
    f = pl.program_id(1)
    acc_ref = o_ref
    tm = u_ref.shape[0]
    sub = _tile(tm, FFN_SUB)

    def swiglu_step(first):
        if first:
            gain = g_ref[...] * (1.0 + sc_ref[0])
            shift = sh_ref[0]
        hid = []
        for r0 in range(0, tm, sub):
            if first:
                for c0 in range(r0, r0 + sub, NORM_ROWS):
                    chunk = slice(c0, c0 + NORM_ROWS)
                    u_ref[chunk, :] = _rms_mod_rows(x_ref, chunk, gain, shift)
            u = u_ref[r0:r0 + sub]
            hg = jnp.dot(u, wg_ref[...], preferred_element_type=F32)
            hu = jnp.dot(u, wu_ref[...], preferred_element_type=F32)
            hid.append((_silu(hg) * hu).astype(BF16))
        for k, r0 in enumerate(range(0, tm, sub)):
            down = jnp.dot(hid[k], wd_ref[...], preferred_element_type=F32)
            if first:
                acc_ref[r0:r0 + sub] = down
            else:
                acc_ref[r0:r0 + sub] += down

    @pl.when(f == 0)
    def _():
        swiglu_step(True)

    @pl.when(f != 0)
    def _():
        swiglu_step(False)

    @pl.when(f == pl.num_programs(1) - 1)
    def _():
        o_ref[...] = x_ref[...] + gate_ref[0] * acc_ref[...]


def _dense_ffn(x2, mod, mod_row, g, wg, wu, wd, tm, tf):
    m, d = x2.shape
    ff = wg.shape[1]
    row = lambda w: pl.BlockSpec((1, 1, d), lambda i, f: (mod_row(i, tm) * 6 + w, 0, 0))
    return pl.pallas_call(
        _ffn_kernel,
        grid=(m // tm, ff // tf),
        in_specs=[
            pl.BlockSpec((tm, d), lambda i, f: (i, 0)),
            row(3), row(4), row(5),
            pl.BlockSpec((1, d), lambda i, f: (0, 0)),
            pl.BlockSpec((d, tf), lambda i, f: (0, f)),
            pl.BlockSpec((d, tf), lambda i, f: (0, f)),
            pl.BlockSpec((tf, d), lambda i, f: (f, 0)),
        ],
        out_specs=pl.BlockSpec((tm, d), lambda i, f: (i, 0)),
        out_shape=jax.ShapeDtypeStruct((m, d), F32),
        scratch_shapes=[pltpu.VMEM((tm, d), BF16)],
        compiler_params=_cparams("parallel", "arbitrary"),
        name="dense_ffn",
    )(x2, mod, mod, mod, g.reshape(1, d), wg, wu, wd)


def _swa_prep_kernel(nheads, p_ref, cos_ref, sin_ref, g_ref, perm_ref, ones_ref, o_ref):
    cos = cos_ref[...]
    sin = sin_ref[...]
    perm = perm_ref[...]
    ones = ones_ref[...]
    for h in range(nheads):
        cs = slice(h * HEAD_DIM, (h + 1) * HEAD_DIM)
        t = p_ref[:, cs].astype(F32)
        ms = _split_dot(t * t, ones) * (1.0 / HEAD_DIM)
        t = t * lax.rsqrt(ms + EPS) * g_ref[:, cs]
        o_ref[:, cs] = _rope_mxu(t, cos, sin, perm).astype(BF16)


def _swa_prep(p, b, t, nheads, col_block, cos, sin, gains):
    tm = _tile(t, 256)
    nt = t // tm
    w = nheads * HEAD_DIM
    return pl.pallas_call(
        functools.partial(_swa_prep_kernel, nheads),
        grid=(b, nt),
        in_specs=[
            pl.BlockSpec((tm, w), lambda bb, i: (bb * nt + i, col_block)),
            pl.BlockSpec((tm, LANES), lambda bb, i: (i, 0)),
            pl.BlockSpec((tm, LANES), lambda bb, i: (i, 0)),
            pl.BlockSpec((1, w), lambda bb, i: (0, 0)),
            pl.BlockSpec((LANES, LANES), lambda bb, i: (0, 0)),
            pl.BlockSpec((LANES, LANES), lambda bb, i: (0, 0)),
        ],
        out_specs=pl.BlockSpec((tm, w), lambda bb, i: (bb * nt + i, 0)),
        out_shape=jax.ShapeDtypeStruct((b * t, w), BF16),
        compiler_params=_cparams("parallel", "parallel"),
        name="swa_prep",
    )(p, cos, sin, gains, _rope_perm(HEAD_DIM // 4), jnp.ones((LANES, LANES), BF16))


def _swa_attn_kernel(s_len, sink_ref, q_ref, k_ref, v_ref, kc_ref, vc_ref, o_ref):
    blk = pl.program_id(1)
    span = BLOCK + 2 * WINDOW
    start = blk * BLOCK
    ws = pl.multiple_of(jnp.clip(start - WINDOW, 0, s_len - span), BLOCK)
    rows = WIN_GROUP * BLOCK
    row = lax.broadcasted_iota(jnp.int32, (rows, span), 0)
    col = lax.broadcasted_iota(jnp.int32, (rows, span), 1)
    in_window = jnp.abs((start + row % BLOCK) - (ws + col)) <= WINDOW
    rcol = lax.broadcasted_iota(jnp.int32, (rows, 1), 0) // BLOCK
    scores = []
    for n in range(WIN_KV_HEADS):
        hs = slice(n * HEAD_DIM, (n + 1) * HEAD_DIM)
        q0 = n * WIN_GROUP * HEAD_DIM
        q = jnp.concatenate(
            [q_ref[0, :, q0 + g * HEAD_DIM:q0 + (g + 1) * HEAD_DIM] for g in range(WIN_GROUP)], axis=0)
        s_w = jnp.where(in_window, _nt_dot(q, k_ref[0, pl.ds(ws, span), hs]), NEG_INF)
        s_c = _nt_dot(q, kc_ref[0, :, hs])
        scores.append((s_w, s_c))
    for n in range(WIN_KV_HEADS):
        hs = slice(n * HEAD_DIM, (n + 1) * HEAD_DIM)
        q0 = n * WIN_GROUP * HEAD_DIM
        vw = v_ref[0, pl.ds(ws, span), hs]
        s_w, s_c = scores[n]
        sink = jnp.zeros((rows, 1), F32)
        for g in range(WIN_GROUP):
            sink = jnp.where(rcol == g, sink_ref[n * WIN_GROUP + g], sink)
        m = jnp.maximum(jnp.maximum(s_w.max(axis=-1, keepdims=True), s_c.max(axis=-1, keepdims=True)), sink)
        e_w = jnp.exp(s_w - m)
        e_c = jnp.exp(s_c - m)
        den = jnp.sum(e_w, axis=-1, keepdims=True) + jnp.sum(e_c, axis=-1, keepdims=True) + jnp.exp(sink - m)
        acc = jnp.dot(e_w.astype(BF16), vw, preferred_element_type=F32)
        acc = acc + jnp.dot(e_c.astype(BF16), vc_ref[0, :, hs], preferred_element_type=F32)
        out = acc / den
        for g in range(WIN_GROUP):
            o_ref[0, :, q0 + g * HEAD_DIM:q0 + (g + 1) * HEAD_DIM] = (
                out[g * BLOCK:(g + 1) * BLOCK].astype(o_ref.dtype))


def _swa_attn(qk, p, kc, pc, sink, b, s_len, ctx_len):
    qk3 = qk.reshape(b, s_len, -1)
    p3 = p.reshape(b, s_len, -1)
    kc3 = kc.reshape(b, ctx_len, -1)
    pc3 = pc.reshape(b, ctx_len, -1)
    return pl.pallas_call(
        functools.partial(_swa_attn_kernel, s_len),
        grid=(b, s_len // BLOCK),
        in_specs=[
            pl.BlockSpec(memory_space=pltpu.SMEM),
            pl.BlockSpec((1, BLOCK, ODD_Q), lambda bb, i: (bb, i, 0)),
            pl.BlockSpec((1, s_len, ODD_KV), lambda bb, i: (bb, 0, ODD_Q // ODD_KV)),
            pl.BlockSpec((1, s_len, ODD_KV), lambda bb, i: (bb, 0, (ODD_Q + ODD_KV) // ODD_KV)),
            pl.BlockSpec((1, ctx_len, ODD_KV), lambda bb, i: (bb, 0, 0)),
            pl.BlockSpec((1, ctx_len, ODD_KV), lambda bb, i: (bb, 0, 1)),
        ],
        out_specs=pl.BlockSpec((1, BLOCK, ODD_Q), lambda bb, i: (bb, i, 0)),
        out_shape=jax.ShapeDtypeStruct((b, s_len, ODD_Q), BF16),
        compiler_params=_cparams("parallel", "parallel"),
        name="swa_attn",
    )(sink, qk3, qk3, p3, kc3, pc3)


def _fourier_kernel(scale, f_ref, cs_ref, ct_ref, st_ref, o_ref, xc_ref, xs_ref):
    @pl.when(pl.program_id(1) == 0)
    def _():
        for g in range(FNET_GROUPS):
            cs = slice(g * FNET_CH, (g + 1) * FNET_CH)
            r = jnp.dot(f_ref[0, :, cs], cs_ref[...], preferred_element_type=F32)
            xc_ref[:, cs] = r[:, :FNET_CH].astype(BF16)
            xs_ref[:, cs] = r[:, FNET_CH:].astype(BF16)

    y = jnp.dot(ct_ref[...], xc_ref[...], preferred_element_type=F32)
    y = y - jnp.dot(st_ref[...], xs_ref[...], preferred_element_type=F32)
    o_ref[0] = (y * scale).astype(o_ref.dtype)


def _dft_tables(n):
    k = jnp.arange(n, dtype=jnp.int32)
    ang = ((k[:, None] * k[None, :]) % n).astype(F32) * (2.0 * np.pi / n)
    return jnp.cos(ang), jnp.sin(ang)


def _fourier(p, b, t, col_block):
    w = FNET_GROUPS * FNET_CH
    p3 = p.reshape(b, t, -1)
    cc, sc = _dft_tables(FNET_CH)
    ct, st = _dft_tables(t)
    cs = jnp.concatenate([cc, sc], axis=1).astype(BF16)
    tk = _tile(t, 512)
    return pl.pallas_call(
        functools.partial(_fourier_kernel, float((t * FNET_CH) ** -0.5)),
        grid=(b, t // tk),
        in_specs=[
            pl.BlockSpec((1, t, w), lambda bb, i: (bb, 0, col_block)),
            pl.BlockSpec((FNET_CH, 2 * FNET_CH), lambda bb, i: (0, 0)),
            pl.BlockSpec((tk, t), lambda bb, i: (i, 0)),
            pl.BlockSpec((tk, t), lambda bb, i: (i, 0)),
        ],
        out_specs=pl.BlockSpec((1, tk, w), lambda bb, i: (bb, i, 0)),
        out_shape=jax.ShapeDtypeStruct((b, t, w), BF16),
        scratch_shapes=[pltpu.VMEM((t, w), BF16), pltpu.VMEM((t, w), BF16)],
        compiler_params=_cparams("parallel", "arbitrary"),
        name="fourier",
    )(p3, cs, ct.astype(BF16), st.astype(BF16))


def _router_kernel(x_ref, sh_ref, sc_ref, g_ref, wh_ref, wl_ref, fin_ref, info_ref, cnt_ref, run_ref):
    @pl.when(pl.program_id(0) == 0)
    def _():
        run_ref[...] = jnp.zeros_like(run_ref)

    u = _rms_mod(x_ref[...], g_ref[...], sh_ref[0], sc_ref[0])
    d = u.shape[1]
    fin_ref[...] = _pack_pair(u[:, :d // 2], u[:, d // 2:])
    u_hi = u.astype(BF16)
    u_lo = (u - u_hi.astype(F32)).astype(BF16)
    logits = (jnp.dot(u_hi, wh_ref[...], preferred_element_type=F32)
              + (jnp.dot(u_lo, wh_ref[...], preferred_element_type=F32)
                 + jnp.dot(u_hi, wl_ref[...], preferred_element_type=F32)))
    tr = logits.shape[0]
    lane = lax.broadcasted_iota(jnp.int32, (tr, LANES), 1).astype(F32)
    logits = jnp.where(lane < N_EXPERTS, logits, -jnp.inf)
    m1 = logits.max(axis=-1, keepdims=True)
    i1 = jnp.where(logits == m1, lane, float(LANES)).min(axis=-1, keepdims=True)
    rest = jnp.where(lane == i1, -jnp.inf, logits)
    m2 = rest.max(axis=-1, keepdims=True)
    i2 = jnp.where(rest == m2, lane, float(LANES)).min(axis=-1, keepdims=True)
    e21 = jnp.exp(m2 - m1)
    g1 = 1.0 / (1.0 + e21)
    g2 = e21 / (1.0 + e21)
    oh1 = lane == i1
    oh2 = lane == i2
    oh = (oh1 | oh2).astype(F32)
    r = lax.broadcasted_iota(jnp.int32, (tr, tr), 0)
    c = lax.broadcasted_iota(jnp.int32, (tr, tr), 1)
    before = (r > c).astype(BF16)
    prefix = jnp.dot(before, oh.astype(BF16), preferred_element_type=F32) + run_ref[...]
    r1 = jnp.sum(jnp.where(oh1, prefix, 0.0), axis=-1, keepdims=True)
    r2 = jnp.sum(jnp.where(oh2, prefix, 0.0), axis=-1, keepdims=True)
    run = run_ref[...] + jnp.sum(oh, axis=0, keepdims=True)
    run_ref[...] = run
    cnt_ref[...] = run
    info = jnp.zeros((tr, LANES), F32)
    for j, val in enumerate((i1, i2, r1, r2, g1, g2)):
        info = jnp.where(lane == j, val, info)
    info_ref[...] = info[:, :8]


def _router(x2, mod, mod_row, g, wr_hi, wr_lo, tr):
    m, d = x2.shape
    row = lambda w: pl.BlockSpec((1, 1, d), lambda i: (mod_row(i, tr) * 6 + w, 0, 0))
    const = lambda i: (0, 0)
    return pl.pallas_call(
        _router_kernel,
        grid=(m // tr,),
        in_specs=[
            pl.BlockSpec((tr, d), lambda i: (i, 0)),
            row(3), row(4),
            pl.BlockSpec((1, d), const),
            pl.BlockSpec((d, LANES), const),
            pl.BlockSpec((d, LANES), const),
        ],
        out_specs=[
            pl.BlockSpec((tr, d // 2), lambda i: (i, 0)),
            pl.BlockSpec((tr, 8), lambda i: (i, 0)),
            pl.BlockSpec((1, LANES), const),
        ],
        out_shape=[
            jax.ShapeDtypeStruct((m, d // 2), U32),
            jax.ShapeDtypeStruct((m, 8), F32),
            jax.ShapeDtypeStruct((1, LANES), F32),
        ],
        scratch_shapes=[pltpu.VMEM((1, LANES), F32)],
        compiler_params=_cparams("arbitrary"),
        name="moe_router",
    )(x2, mod, mod, g.reshape(1, d), wr_hi, wr_lo)


def _row_copy(src, src_row, dst, dst_row, sem):
    return pltpu.make_async_copy(src.at[pl.ds(src_row, 1)], dst.at[pl.ds(dst_row, 1)], sem)


def _dispatch_kernel(tg, pos_ref, fin_ref, init_ref, xs_ref, sem):
    del init_ref

    def issue(t, c):
        _row_copy(fin_ref, t, xs_ref, pos_ref[0, 2 * t], sem).start()
        _row_copy(fin_ref, t, xs_ref, pos_ref[0, 2 * t + 1], sem).start()
        return c

    lax.fori_loop(0, tg, issue, 0, unroll=8)
    for _ in range(2):
        pltpu.make_async_copy(fin_ref, xs_ref.at[pl.ds(0, tg)], sem).wait()


def _dispatch(fin, pos, cap, tg):
    m, w = fin.shape
    return pl.pallas_call(
        functools.partial(_dispatch_kernel, tg),
        grid=(m // tg,),
        in_specs=[
            pl.BlockSpec((None, 1, 2 * tg), lambda i: (i, 0, 0), memory_space=pltpu.SMEM),
            pl.BlockSpec((tg, w), lambda i: (i, 0)),
            pl.BlockSpec(memory_space=pl.ANY),
        ],
        out_specs=pl.BlockSpec(memory_space=pl.ANY),
        out_shape=jax.ShapeDtypeStruct((cap, w), U32),
        scratch_shapes=[pltpu.SemaphoreType.DMA(())],
        input_output_aliases={2: 0},
        compiler_params=_cparams("arbitrary"),
        name="moe_dispatch",
    )(pos.reshape(m // tg, 1, 2 * tg), fin, jnp.zeros((cap, w), U32))


def _moe_ffn_kernel(sub, te_ref, valid_ref, xs_ref, wg_ref, wu_ref, wd_ref, ys_ref, u_ref, acc_ref):
    i = pl.program_id(0)
    f = pl.program_id(1)
    tm, d = u_ref.shape
    valid = valid_ref[i]

    @pl.when(f == 0)
    def _():
        a, b = _unpack_pair(xs_ref[...])
        u_ref[:, :d // 2] = a.astype(BF16)
        u_ref[:, d // 2:] = b.astype(BF16)
        acc_ref[...] = jnp.zeros_like(acc_ref)

    def swiglu(nsub):
        wg = wg_ref[...].astype(BF16)
        wu = wu_ref[...].astype(BF16)
        wd = wd_ref[...].astype(BF16)
        hid = []
        for sb in range(nsub):
            u = u_ref[sb * sub:(sb + 1) * sub]
            hg = jnp.dot(u, wg, preferred_element_type=F32)
            hu = jnp.dot(u, wu, preferred_element_type=F32)
            hid.append((_silu(hg) * hu).astype(BF16))
        for sb in range(nsub):
            acc_ref[sb * sub:(sb + 1) * sub] += jnp.dot(hid[sb], wd, preferred_element_type=F32)

    nsub = tm // sub
    for n_valid in range(1, nsub + 1):
        hi = n_valid * sub if n_valid < nsub else tm
        @pl.when((valid > (n_valid - 1) * sub) & (valid <= hi))
        def _(n_valid=n_valid):
            swiglu(n_valid)

    @pl.when(f == pl.num_programs(1) - 1)
    def _():
        ys_ref[...] = _pack_pair(acc_ref[:, :d // 2], acc_ref[:, d // 2:])


def _moe_ffn(xs, tile_expert, tile_valid, wg, wu, wd, tm, tf, sub):
    cap, w = xs.shape
    d = 2 * w
    ff = wg.shape[3]
    nf = ff // tf

    def f_eff(i, f, valid):
        return jnp.where(valid[i] > 0, f, nf - 1)

    grid_spec = pltpu.PrefetchScalarGridSpec(
        num_scalar_prefetch=2,
        grid=(cap // tm, nf),
        in_specs=[
            pl.BlockSpec((tm, w), lambda i, f, te, valid: (i, 0)),
            pl.BlockSpec((None, None, d, tf), lambda i, f, te, valid: (0, te[i], 0, f_eff(i, f, valid))),
            pl.BlockSpec((None, None, d, tf), lambda i, f, te, valid: (0, te[i], 0, f_eff(i, f, valid))),
            pl.BlockSpec((None, None, tf, d), lambda i, f, te, valid: (0, te[i], f_eff(i, f, valid), 0)),
        ],
        out_specs=pl.BlockSpec((tm, w), lambda i, f, te, valid: (i, 0)),
        scratch_shapes=[pltpu.VMEM((tm, d), BF16), pltpu.VMEM((tm, d), F32)],
    )
    return pl.pallas_call(
        functools.partial(_moe_ffn_kernel, sub),
        grid_spec=grid_spec,
        out_shape=jax.ShapeDtypeStruct((cap, w), U32),
        compiler_params=_cparams("parallel", "arbitrary"),
        name="moe_ffn",
    )(tile_expert, tile_valid, xs, wg, wu, wd)


def _combine_kernel(tc, pos_ref, posn_ref, x_ref, info_ref, gate_ref, ys_ref, o_ref, buf_ref, sem):
    i = pl.program_id(0)
    slot = i % 2

    def issue(p_ref, s):
        def body(t, c):
            _row_copy(ys_ref, p_ref[0, 2 * t], buf_ref.at[s, 0], t, sem.at[s]).start()
            _row_copy(ys_ref, p_ref[0, 2 * t + 1], buf_ref.at[s, 1], t, sem.at[s]).start()
            return c

        lax.fori_loop(0, tc, body, 0, unroll=8)

    @pl.when(i == 0)
    def _():
        issue(pos_ref, 0)

    @pl.when(i + 1 < pl.num_programs(0))
    def _():
        issue(posn_ref, 1 - slot)

    for k in range(2):
        pltpu.make_async_copy(ys_ref.at[pl.ds(0, tc)], buf_ref.at[slot, k], sem.at[slot]).wait()
    w = buf_ref.shape[3]
    g1 = info_ref[:, 4:5]
    g2 = info_ref[:, 5:6]
    a1, b1 = _unpack_pair(buf_ref[slot, 0])
    a2, b2 = _unpack_pair(buf_ref[slot, 1])
    gate = gate_ref[0]
    o_ref[:, :w] = x_ref[:, :w] + gate[:, :w] * (g1 * a1 + g2 * a2)
    o_ref[:, w:] = x_ref[:, w:] + gate[:, w:] * (g1 * b1 + g2 * b2)


def _combine(x2, info, pos, ys, mod, mod_row, tc):
    m, d = x2.shape
    w = ys.shape[1]
    n = m // tc
    pos3 = pos.reshape(n, 1, 2 * tc)
    return pl.pallas_call(
        functools.partial(_combine_kernel, tc),
        grid=(n,),
        in_specs=[
            pl.BlockSpec((None, 1, 2 * tc), lambda i: (i, 0, 0), memory_space=pltpu.SMEM),
            pl.BlockSpec((None, 1, 2 * tc), lambda i: (jnp.minimum(i + 1, n - 1), 0, 0), memory_space=pltpu.SMEM),
            pl.BlockSpec((tc, d), lambda i: (i, 0)),
            pl.BlockSpec((tc, 8), lambda i: (i, 0)),
            pl.BlockSpec((1, 1, d), lambda i: (mod_row(i, tc) * 6 + 5, 0, 0)),
            pl.BlockSpec(memory_space=pl.ANY),
        ],
        out_specs=pl.BlockSpec((tc, d), lambda i: (i, 0)),
        out_shape=jax.ShapeDtypeStruct((m, d), F32),
        scratch_shapes=[pltpu.VMEM((2, 2, tc, w), U32), pltpu.SemaphoreType.DMA((2,))],
        compiler_params=_cparams("arbitrary"),
        name="moe_combine",
    )(pos3, pos3, x2, info, mod, ys)


MOE_SUB = 256


def _moe(x2, s, mod, mod_row, g, router_w, wg, wu, wd):
    m, d = x2.shape
    tm = _tile(m, 1024)
    sub = _tile(tm, MOE_SUB)
    tf = _tile(wg.shape[3], 512)
    tr = _tile(s, 512)
    tg = _tile(s, 256)
    wr = jnp.zeros((d, LANES), F32).at[:, :N_EXPERTS].set(router_w)
    wr_hi = wr.astype(BF16)
    wr_lo = (wr - wr_hi.astype(F32)).astype(BF16)
    fin, info, cnt = _router(x2, mod, mod_row, g, wr_hi, wr_lo, tr)

    counts = cnt[0, :N_EXPERTS].astype(jnp.int32)
    padded = (counts + tm - 1) // tm * tm
    ends = jnp.cumsum(padded)
    starts = ends - padded
    experts = info[:, 0:2].astype(jnp.int32)
    pos = starts[experts] + info[:, 2:4].astype(jnp.int32)
    ntiles = (2 * m) // tm + N_EXPERTS
    cap = ntiles * tm
    tile_start = jnp.arange(ntiles, dtype=jnp.int32) * tm
    tile_expert = jnp.sum(tile_start[:, None] >= ends[None, :], axis=1).astype(jnp.int32)
    active = tile_start < ends[-1]
    last_expert = tile_expert[ends[-1] // tm - 1]
    tile_expert = jnp.where(active, tile_expert, last_expert)
    group_end = (starts + counts)[tile_expert]
    tile_valid = jnp.where(active, jnp.clip(group_end - tile_start, 0, tm), 0).astype(jnp.int32)

    xs = _dispatch(fin, pos, cap, tg)
    ys = _moe_ffn(xs, tile_expert, tile_valid, wg, wu, wd, tm, tf, sub)
    return _combine(x2, info, pos, ys, mod, mod_row, tg)


def _rope_tables(t, rot_dim):
    rows = t // GRID_W
    row = jnp.repeat(jnp.arange(rows, dtype=F32), GRID_W)
    col = jnp.tile(jnp.arange(GRID_W, dtype=F32), rows)
    half = rot_dim // 2
    inv = ROPE_THETA ** (-jnp.arange(0, half, 2, dtype=F32) / half)
    ang_r = row[:, None] * inv[None, :]
    ang_c = col[:, None] * inv[None, :]
    pad = LANES - rot_dim
    cos = jnp.concatenate([jnp.cos(ang_r), jnp.cos(ang_r), jnp.cos(ang_c), jnp.cos(ang_c),
                           jnp.ones((t, pad), F32)], axis=1)
    sin = jnp.concatenate([-jnp.sin(ang_r), jnp.sin(ang_r), -jnp.sin(ang_c), jnp.sin(ang_c),
                           jnp.zeros((t, pad), F32)], axis=1)
    return cos, sin


def _identity_rope(t):
    return jnp.ones((t, LANES), F32), jnp.zeros((t, LANES), F32)


def _pad_heads(w, real, padded):
    k = w.shape[0]
    w = w.reshape(k, MLA_HEADS, real)
    return jnp.pad(w, ((0, 0), (0, 0), (0, padded - real))).reshape(k, MLA_HEADS * padded)


def kernel(x, c, ctx, c_ctx, ada_w, ada_b, mix_norm_g, ffn_norm_g, even_w_in, mla_q_a_norm_g, mla_w_q_b, mla_kv_a_norm_g, mla_w_kv_b, mla_q_norm_g, mla_k_norm_g, conv_dw_w, conv_dw_b, conv_ln_g, conv_ln_b, even_w_out, dense_w_gate, dense_w_up, dense_w_down, odd_w_in, swa_q_norm_g, swa_k_norm_g, swa_sink, odd_w_out, router_w, expert_w_gate, expert_w_up, expert_w_down):
    b, s, d = x.shape
    l = ctx.shape[1]
    assert ada_w.shape[0] == 2, "two layers: an even (MLA | conv, dense) then an odd (SWA | Fourier, MoE) one"

    r = (b + 1 + 7) // 8 * 8
    cvec = jnp.zeros((r, d), F32).at[:b].set(c).at[b].set(c_ctx)
    mod = _modulation(cvec, ada_w, ada_b).reshape(2, r * 6, 1, d)
    mod0, mod1 = mod[0], mod[1]

    tm = _tile(s, 512)
    tml = _tile(b * l, 512)
    tm_big = _tile(s, 1024)
    tml_big = _tile(b * l, 1024)
    lat_row = lambda i, tile: (i * tile) // s
    ctx_row = lambda i, tile: b
    x2 = x.reshape(b * s, d)
    h2 = ctx.reshape(b * l, d)

    w_in = even_w_in[0]
    w_in0 = jnp.concatenate(
        [w_in[:, MLA_IN:], w_in[:, :MLA_IN], jnp.zeros((d, MLA_IN_PAD - MLA_IN), F32)], axis=1).astype(BF16)
    p_lat = _norm_mod_matmul(x2, mod0, lat_row, 0, mix_norm_g[0], w_in0, tm_big)
    p_ctx = _norm_mod_matmul(h2, mod0, ctx_row, 0, mix_norm_g[0], w_in0, tml_big)

    mla_scale = MLA_QK ** -0.5
    wq = _pad_heads(mla_w_q_b[0], MLA_QK, MLA_QK_PAD).astype(BF16)
    wkv = mla_w_kv_b[0].astype(BF16)
    qg = jnp.pad(mla_q_norm_g[0] * mla_scale, (0, MLA_QK_PAD - MLA_QK)).reshape(1, -1)
    kg = jnp.pad(mla_k_norm_g[0], (0, MLA_QK_PAD - MLA_QK)).reshape(1, -1)
    qag = mla_q_a_norm_g[0].reshape(1, -1)
    kvag = mla_kv_a_norm_g[0].reshape(1, -1)
    mla_col = 2 * CONV_CH // MLA_IN_PAD
    cos_m, sin_m = _rope_tables(s, MLA_ROPE)
    q_l, k_l, v_l = _mla_prep(p_lat, b, s, mla_col, cos_m, sin_m, qag, kvag, wq, wkv, qg, kg)
    cos_i, sin_i = _identity_rope(l)
    q_c, k_c, v_c = _mla_prep(p_ctx, b, l, mla_col, cos_i, sin_i, qag, kvag, wq, wkv, qg, kg)
    att_l = _mla_attn(q_l, [(k_l, v_l), (k_c, v_c)]).reshape(b * s, -1)
    att_c = _mla_attn(q_c, [(k_c, v_c)]).reshape(b * l, -1)
    conv_l = _conv_module(p_lat, b, s, conv_dw_w[0], conv_dw_b[0], conv_ln_g[0], conv_ln_b[0])
    conv_c = _conv_module(p_ctx, b, l, conv_dw_w[0], conv_dw_b[0], conv_ln_g[0], conv_ln_b[0])

    w_out = even_w_out[0].astype(BF16)
    k_att = MLA_HEADS * MLA_V
    x2 = _out_proj_residual(att_l, conv_l, w_out[:k_att], w_out[k_att:], x2, mod0, lat_row, 2, tm)
    h2 = _out_proj_residual(att_c, conv_c, w_out[:k_att], w_out[k_att:], h2, mod0, ctx_row, 2, tml)

    wg = dense_w_gate[0].astype(BF16)
    wu = dense_w_up[0].astype(BF16)
    wd = dense_w_down[0].astype(BF16)
    tf = _tile(wg.shape[1], 512)
    x2 = _dense_ffn(x2, mod0, lat_row, ffn_norm_g[0], wg, wu, wd, tm_big, tf)
    h2 = _dense_ffn(h2, mod0, ctx_row, ffn_norm_g[0], wg, wu, wd, tml_big, tf)

    w_in1 = odd_w_in[0].astype(BF16)
    p = _norm_mod_matmul(x2, mod1, lat_row, 0, mix_norm_g[1], w_in1, tm_big)
    w_kv_c = w_in1[:, ODD_Q:ODD_Q + 2 * ODD_KV]
    pc = _norm_mod_matmul(h2, mod1, ctx_row, 0, mix_norm_g[1], w_kv_c, tml_big)

    cos_s, sin_s = _rope_tables(s, HEAD_DIM)
    gains = jnp.concatenate([jnp.tile(swa_q_norm_g[0] * HEAD_DIM ** -0.5, WIN_Q_HEADS),
                             jnp.tile(swa_k_norm_g[0], WIN_KV_HEADS)]).reshape(1, -1)
    qk = _swa_prep(p, b, s, WIN_Q_HEADS + WIN_KV_HEADS, 0, cos_s, sin_s, gains)
    kc = _swa_prep(pc, b, l, WIN_KV_HEADS, 0, cos_i, sin_i, gains[:, ODD_Q:])
    att = _swa_attn(qk, p, kc, pc, swa_sink[0], b, s, l).reshape(b * s, -1)
    fcol = (ODD_Q + 2 * ODD_KV) // (FNET_GROUPS * FNET_CH)
    four = _fourier(p, b, s, fcol).reshape(b * s, -1)

    w_out1 = odd_w_out[0].astype(BF16)
    x2 = _out_proj_residual(att, four, w_out1[:ODD_Q], w_out1[ODD_Q:], x2, mod1, lat_row, 2, tm)

    x2 = _moe(x2, s, mod1, lat_row, ffn_norm_g[1], router_w[0],
              expert_w_gate, expert_w_up, expert_w_down)
    return x2.reshape(b, s, d)
```
